```python
import jax, jax.numpy as jnp
from jax import lax
import numpy as np

D_MODEL = 1024
BATCH = 8
SEQ = 2048
DEPTH = 1

PLE_DIM = 256
ROPE_THETA = 500000.0
EPS = 1e-6
NEG_INF = -1e30

MIX_WIDTH = D_MODEL
MLA_V_DIM = 64
MLA_NOPE_DIM = 64
MLA_ROPE_DIM = 32
MLA_QK_DIM = MLA_NOPE_DIM + MLA_ROPE_DIM
MLA_WIDTH = MIX_WIDTH // 2
MLA_HEADS = MLA_WIDTH // MLA_V_DIM
MLA_Q_RANK = 3 * D_MODEL // 8
MLA_KV_RANK = 4 * MLA_V_DIM
Q_BLOCK = 128
DIL_HEAD_DIM = 64
DIL_WIDTH = MIX_WIDTH - MLA_WIDTH
DIL_HEADS = DIL_WIDTH // DIL_HEAD_DIM
DIL_ROT_DIM = DIL_HEAD_DIM // 4
DIL_PATTERNS = ((128, 1), (512, 4), (2048, 16))
IN_SPLITS = (MLA_Q_RANK, MLA_KV_RANK, MLA_ROPE_DIM, DIL_WIDTH, DIL_WIDTH, DIL_WIDTH)
IN_WIDTH = sum(IN_SPLITS)
PEER_KEYS = 128
PEER_EXPERTS = PEER_KEYS * PEER_KEYS
PEER_HEADS = 8
PEER_QDIM = 128
PEER_HALF = PEER_QDIM // 2
PEER_TOPK = 16
PEER_CHUNK = 128

kernel_name = "hybrid_mla_dilswa_peer_encoder"


def rms_norm(x, g):
    xf = x.astype(jnp.float32)
    y = xf * lax.rsqrt(jnp.mean(xf * xf, axis=-1, keepdims=True) + EPS)
    return (y * g.astype(jnp.float32)).astype(x.dtype)


def rope_tables(seq, rot_dim):
    inv = ROPE_THETA ** (-jnp.arange(0, rot_dim, 2, dtype=jnp.float32) / rot_dim)
    ang = jnp.arange(seq, dtype=jnp.float32)[:, None] * inv[None, :]
    return jnp.cos(ang), jnp.sin(ang)


def apply_rope(x, cos, sin, rot_dim):
    half = rot_dim // 2
    shape = (x.shape[1],) + (1,) * (x.ndim - 3) + (half,)
    c = cos.reshape(shape).astype(x.dtype)
    s = sin.reshape(shape).astype(x.dtype)
    x1 = x[..., :half]
    x2 = x[..., half:rot_dim]
    return jnp.concatenate([x1 * c - x2 * s, x2 * c + x1 * s, x[..., rot_dim:]], axis=-1)


def mla_attention(q_nope, q_rope, k_nope, k_rope, v):
    B, S, H, _ = q_nope.shape
    nq = S // Q_BLOCK
    scale = MLA_QK_DIM ** -0.5

    def blocks(t):
        return jnp.moveaxis(t.reshape((B, nq, Q_BLOCK) + t.shape[2:]), 1, 0)

    def one_block(args):
        qn, qr = args
        s = jnp.einsum('bqhd,bkhd->bhqk', qn, k_nope) + jnp.einsum('bqhr,bkr->bhqk', qr, k_rope)
        w = jax.nn.softmax(s.astype(jnp.float32) * scale, axis=-1).astype(v.dtype)
        return jnp.einsum('bhqk,bkhd->bqhd', w, v)

    out = lax.map(one_block, (blocks(q_nope), blocks(q_rope)))
    return jnp.moveaxis(out, 0, 1).reshape(B, S, H, v.shape[-1])


def banded_attention(q, k, v, radius):
    N, L, hd = q.shape
    blk = radius
    nb = -(-L // blk)
    lp = nb * blk
    qb = jnp.pad(q, ((0, 0), (0, lp - L), (0, 0))).reshape(N, nb, blk, hd)

    def windows(t):
        tp = jnp.pad(t, ((0, 0), (blk, lp - L + blk), (0, 0))).reshape(N, nb + 2, blk, hd)
        return jnp.concatenate([tp[:, :-2], tp[:, 1:-1], tp[:, 2:]], axis=2)

    kw, vw = windows(k), windows(v)
    qpos = jnp.arange(nb)[:, None] * blk + jnp.arange(blk)[None, :]
    kpos = jnp.arange(nb)[:, None] * blk - blk + jnp.arange(3 * blk)[None, :]
    kp = kpos[:, None, :]
    mask = (jnp.abs(qpos[:, :, None] - kp) <= radius) & (kp >= 0) & (kp < L)
    s = jnp.einsum('nbqd,nbkd->nbqk', qb, kw).astype(jnp.float32) * (hd ** -0.5)
    s = jnp.where(mask, s, NEG_INF)
    m = jnp.max(s, axis=-1, keepdims=True)
    e = jnp.exp(s - m)
    den = jnp.sum(e, axis=-1, keepdims=True)
    out = jnp.einsum('nbqk,nbkd->nbqd', (e / den).astype(v.dtype), vw)
    lse = (m + jnp.log(den))[..., 0]
    return out.reshape(N, lp, hd)[:, :L], lse.reshape(N, lp)[:, :L]


def dilated_pattern(q, k, v, window, dil):
    B, S, H, hd = q.shape
    L = S // dil

    def to_sub(t):
        return t.reshape(B, L, dil, H, hd).transpose(0, 2, 3, 1, 4).reshape(B * dil * H, L, hd)

    o, l = banded_attention(to_sub(q), to_sub(k), to_sub(v), window // (2 * dil))
    o = o.reshape(B, dil, H, L, hd).transpose(0, 3, 1, 2, 4).reshape(B, S, H, hd)
    l = l.reshape(B, dil, H, L).transpose(0, 3, 1, 2).reshape(B, S, H)
    return o, l


def dilated_attention(q, k, v):
    outs, lses = [], []
    for window, dil in DIL_PATTERNS:
        o, l = dilated_pattern(q, k, v, window, dil)
        outs.append(o)
        lses.append(l)
    wts = jax.nn.softmax(jnp.stack(lses, axis=0), axis=0).astype(q.dtype)
    return jnp.einsum('pbsh,pbshd->bshd', wts, jnp.stack(outs, axis=0))


def peer_ffn(xn, w_q, keys1, keys2, u_tab, v_tab):
    B, S, D = xn.shape
    T = B * S
    xt = xn.reshape(T, D)
    q = (xt @ w_q).reshape(T, PEER_HEADS, 2, PEER_HALF)
    s1 = jnp.einsum('thd,kd->thk', q[:, :, 0], keys1)
    s2 = jnp.einsum('thd,kd->thk', q[:, :, 1], keys2)
    v1, i1 = lax.top_k(s1, PEER_TOPK)
    v2, i2 = lax.top_k(s2, PEER_TOPK)
    cand = (v1[..., :, None] + v2[..., None, :]).reshape(T, PEER_HEADS, PEER_TOPK * PEER_TOPK)
    cidx = (i1[..., :, None] * PEER_KEYS + i2[..., None, :]).reshape(T, PEER_HEADS, PEER_TOPK * PEER_TOPK)
    top, pos = lax.top_k(cand, PEER_TOPK)
    eidx = jnp.take_along_axis(cidx, pos, axis=-1)
    g = jax.nn.softmax(top.astype(jnp.float32), axis=-1).astype(xn.dtype)
    nc = T // PEER_CHUNK

    def chunk(args):
        xc, ec, gc = args
        a = jax.nn.gelu(jnp.einsum('td,thkd->thk', xc, u_tab[ec]), approximate=False)
        return jnp.einsum('thk,thkd->td', gc * a, v_tab[ec])

    y = lax.map(chunk, (xt.reshape(nc, PEER_CHUNK, D),
                        eidx.reshape(nc, PEER_CHUNK, PEER_HEADS, PEER_TOPK),
                        g.reshape(nc, PEER_CHUNK, PEER_HEADS, PEER_TOPK)))
    return y.reshape(B, S, D)


def setup_inputs(seed: int = 0) -> dict:
    key = jax.random.key(seed)
    ks = jax.random.split(key, 24)
    f32 = jnp.float32

    def nrm(k, shape, scale):
        return jax.random.normal(k, shape, f32) * scale

    def gain(k, shape):
        return 1.0 + 0.01 * jax.random.normal(k, shape, f32)

    L = DEPTH
    return {
        "x": nrm(ks[0], (BATCH, SEQ, D_MODEL), 1.0),
        "p": nrm(ks[1], (DEPTH, BATCH, SEQ, PLE_DIM), 1.0),
        "g_mix": gain(ks[2], (L, D_MODEL)),
        "w_in": nrm(ks[3], (L, D_MODEL, IN_WIDTH), D_MODEL ** -0.5),
        "g_cq": gain(ks[4], (L, MLA_Q_RANK)),
        "w_uq": nrm(ks[5], (L, MLA_Q_RANK, MLA_HEADS * MLA_QK_DIM), MLA_Q_RANK ** -0.5),
        "g_ckv": gain(ks[6], (L, MLA_KV_RANK)),
        "w_ukv": nrm(ks[7], (L, MLA_KV_RANK, MLA_HEADS * (MLA_NOPE_DIM + MLA_V_DIM)), MLA_KV_RANK ** -0.5),
        "g_out_mla": gain(ks[8], (L, MLA_WIDTH)),
        "g_out_dil": gain(ks[9], (L, DIL_WIDTH)),
        "w_out": nrm(ks[10], (L, MIX_WIDTH, D_MODEL), MIX_WIDTH ** -0.5),
        "g_ffn": gain(ks[11], (L, D_MODEL)),
        "w_peer_q": nrm(ks[12], (L, D_MODEL, PEER_HEADS * PEER_QDIM), D_MODEL ** -0.5),
        "peer_keys1": nrm(ks[13], (L, PEER_KEYS, PEER_HALF), PEER_HALF ** -0.5),
        "peer_keys2": nrm(ks[14], (L, PEER_KEYS, PEER_HALF), PEER_HALF ** -0.5),
        "peer_u": nrm(ks[15], (L, PEER_EXPERTS, D_MODEL), D_MODEL ** -0.5),
        "peer_v": nrm(ks[16], (L, PEER_EXPERTS, D_MODEL), 0.5),
        "g_ple": gain(ks[17], (L, D_MODEL)),
        "w_ple_gate": nrm(ks[18], (L, D_MODEL, D_MODEL), D_MODEL ** -0.5),
        "w_ple_proj": nrm(ks[19], (L, PLE_DIM, D_MODEL), PLE_DIM ** -0.5),
        "g_final": gain(ks[20], (D_MODEL,)),
    }


def reference(x, p, g_mix, w_in, g_cq, w_uq, g_ckv, w_ukv, g_out_mla, g_out_dil, w_out,
              g_ffn, w_peer_q, peer_keys1, peer_keys2, peer_u, peer_v,
              g_ple, w_ple_gate, w_ple_proj, g_final):
    B, S, _ = x.shape
    cos_m, sin_m = rope_tables(S, MLA_ROPE_DIM)
    cos_d, sin_d = rope_tables(S, DIL_ROT_DIM)
    offs = np.cumsum(IN_SPLITS)[:-1].tolist()
    h = x
    for i in range(DEPTH):
        hn = rms_norm(h, g_mix[i])
        c_q, c_kv, k_r, q_d, k_d, v_d = jnp.split(hn @ w_in[i], offs, axis=-1)
        q = (rms_norm(c_q, g_cq[i]) @ w_uq[i]).reshape(B, S, MLA_HEADS, MLA_QK_DIM)
        q_nope = q[..., :MLA_NOPE_DIM]
        q_rope = apply_rope(q[..., MLA_NOPE_DIM:], cos_m, sin_m, MLA_ROPE_DIM)
        kv = (rms_norm(c_kv, g_ckv[i]) @ w_ukv[i]).reshape(B, S, MLA_HEADS, MLA_NOPE_DIM + MLA_V_DIM)
        k_nope = kv[..., :MLA_NOPE_DIM]
        v_m = kv[..., MLA_NOPE_DIM:]
        k_rope = apply_rope(k_r, cos_m, sin_m, MLA_ROPE_DIM)
        o_mla = mla_attention(q_nope, q_rope, k_nope, k_rope, v_m).reshape(B, S, MLA_WIDTH)
        qd = apply_rope(q_d.reshape(B, S, DIL_HEADS, DIL_HEAD_DIM), cos_d, sin_d, DIL_ROT_DIM)
        kd = apply_rope(k_d.reshape(B, S, DIL_HEADS, DIL_HEAD_DIM), cos_d, sin_d, DIL_ROT_DIM)
        vd = v_d.reshape(B, S, DIL_HEADS, DIL_HEAD_DIM)
        o_dil = dilated_attention(qd, kd, vd).reshape(B, S, DIL_WIDTH)
        mixed = jnp.concatenate([rms_norm(o_mla, g_out_mla[i]), rms_norm(o_dil, g_out_dil[i])], axis=-1)
        h = h + mixed @ w_out[i]
        h = h + peer_ffn(rms_norm(h, g_ffn[i]), w_peer_q[i], peer_keys1[i], peer_keys2[i],
                         peer_u[i], peer_v[i])
        gate = jax.nn.sigmoid(rms_norm(h, g_ple[i]) @ w_ple_gate[i])
        h = h + gate * (p[i] @ w_ple_proj[i])
    return rms_norm(h, g_final)
```

```python
import functools

import numpy as np
import jax
import jax.numpy as jnp
from jax import lax
from jax.experimental import pallas as pl
from jax.experimental.pallas import tpu as pltpu

F32 = jnp.float32
BF16 = jnp.bfloat16

EPS = 1e-6
MASKED_SCORE = -1e30
ROPE_THETA = 500000.0

LANES = 128
D_MODEL = 1024
PLE_DIM = 256
MLA_HEADS = 8
MLA_NOPE = 64
MLA_ROPE = 32
MLA_V = 64
MLA_QK = MLA_NOPE + MLA_ROPE
MLA_Q_RANK = 384
MLA_KV_RANK = 256
DIL_HEADS = 8
DIL_HEAD_DIM = 64
DIL_ROT = 16
DIL_WIDTH = DIL_HEADS * DIL_HEAD_DIM
DIL_PATTERNS = ((128, 1), (512, 4), (2048, 16))
PEER_KEYS = 128
PEER_HEADS = 8
PEER_HALF = 64
PEER_TOPK = 16
PEER_SLOTS = PEER_HEADS * PEER_TOPK
PEER_EXPERTS = PEER_KEYS * PEER_KEYS

VMEM_LIMIT = 48 * 1024 * 1024


def _params(semantics):
    return pltpu.CompilerParams(dimension_semantics=semantics, vmem_limit_bytes=VMEM_LIMIT)


def _rms(x, g):
    return x * lax.rsqrt(jnp.mean(x * x, axis=-1, keepdims=True) + EPS) * g


def _rope(x, tab_ref, half):
    return (x * tab_ref[0]
            + pltpu.roll(x, half, 1) * tab_ref[1]
            + pltpu.roll(x, LANES - half, 1) * tab_ref[2])


def _dot(a, b):
    return jnp.dot(a, b, preferred_element_type=F32)


def _dot_nt(a, b):
    return lax.dot_general(a, b, (((1,), (1,)), ((), ())), preferred_element_type=F32)


def _in_proj_kernel(x_ref, gmix_ref, win_ref, gcq_ref, wuq_ref, gckv_ref, wuk_ref, wuv_ref,
                    tmq_ref, tmk_ref, tdq_ref, tdk_ref,
                    qm_ref, km_ref, vm_ref, qd_ref, kd_ref, vd_ref):
    hn = _rms(x_ref[...], gmix_ref[...]).astype(BF16)
    y = _dot(hn, win_ref[...])
    o = 0
    c_q = y[:, o:o + MLA_Q_RANK]; o += MLA_Q_RANK
    c_kv = y[:, o:o + MLA_KV_RANK]; o += MLA_KV_RANK
    k_r = y[:, o:o + LANES]; o += LANES
    q_d = y[:, o:o + DIL_WIDTH]; o += DIL_WIDTH
    k_d = y[:, o:o + DIL_WIDTH]; o += DIL_WIDTH
    v_d = y[:, o:o + DIL_WIDTH]

    q = _dot(_rms(c_q, gcq_ref[...]).astype(BF16), wuq_ref[...])
    ckvn = _rms(c_kv, gckv_ref[...]).astype(BF16)
    k_n = _dot(ckvn, wuk_ref[...])
    vm_ref[...] = _dot(ckvn, wuv_ref[...]).astype(BF16)
    k_rope = _rope(k_r, tmk_ref, MLA_ROPE // 2)
    for h in range(MLA_HEADS):
        sl = slice(h * LANES, (h + 1) * LANES)
        qm_ref[:, sl] = _rope(q[:, sl], tmq_ref, MLA_ROPE // 2).astype(BF16)
        km_ref[:, sl] = (k_n[:, sl] + k_rope).astype(BF16)
    for c in range(DIL_WIDTH // LANES):
        sl = slice(c * LANES, (c + 1) * LANES)
        qd_ref[:, sl] = _rope(q_d[:, sl], tdq_ref, DIL_ROT // 2).astype(BF16)
        kd_ref[:, sl] = _rope(k_d[:, sl], tdk_ref, DIL_ROT // 2).astype(BF16)
    vd_ref[...] = v_d.astype(BF16)


def _in_proj(x2, g_mix, w_in_p, g_cq, w_uq_p, g_ckv, w_uk_p, w_uv, tabs, seq, tile):
    T = x2.shape[0]
    n_pos = seq // tile
    full = lambda a: pl.BlockSpec(a.shape, lambda i: (0,) * a.ndim)
    tab = pl.BlockSpec((3, tile, LANES), lambda i: (0, i % n_pos, 0))
    row = lambda w: pl.BlockSpec((tile, w), lambda i: (i, 0))
    outs = [(T, MLA_HEADS * LANES), (T, MLA_HEADS * LANES), (T, MLA_HEADS * MLA_V),
            (T, DIL_WIDTH), (T, DIL_WIDTH), (T, DIL_WIDTH)]
    return pl.pallas_call(
        _in_proj_kernel,
        grid=(T // tile,),
        in_specs=[row(D_MODEL), full(g_mix), full(w_in_p), full(g_cq), full(w_uq_p),
                  full(g_ckv), full(w_uk_p), full(w_uv), tab, tab, tab, tab],
        out_specs=[row(s[1]) for s in outs],
        out_shape=[jax.ShapeDtypeStruct(s, BF16) for s in outs],
        compiler_params=_params(("parallel",)),
        name="in_proj",
    )(x2, g_mix, w_in_p, g_cq, w_uq_p, g_ckv, w_uk_p, w_uv, *tabs)


def _mla_attn_kernel(q_ref, k_ref, v_ref, o_ref):
    v = v_ref[...]
    lane = lax.broadcasted_iota(jnp.int32, (q_ref.shape[0], LANES), 1)
    out = None
    for h in range(2):
        sl = slice(h * LANES, (h + 1) * LANES)
        s = _dot_nt(q_ref[:, sl], k_ref[:, sl])
        e = jnp.exp(s - jnp.max(s, axis=-1, keepdims=True))
        o = _dot(e.astype(BF16), v) / jnp.sum(e, axis=-1, keepdims=True)
        out = o if out is None else jnp.where(lane < MLA_V, out, o)
    o_ref[...] = out


def _mla_attn(qm, km, vm, batch, seq, tq):
    q3 = qm.reshape(batch, seq, MLA_HEADS * LANES)
    k3 = km.reshape(batch, seq, MLA_HEADS * LANES)
    v3 = vm.reshape(batch, seq, MLA_HEADS * MLA_V)
    return pl.pallas_call(
        _mla_attn_kernel,
        grid=(batch, MLA_HEADS // 2, seq // tq),
        in_specs=[pl.BlockSpec((None, tq, 2 * LANES), lambda b, p, i: (b, i, p)),
                  pl.BlockSpec((None, seq, 2 * LANES), lambda b, p, i: (b, 0, p)),
                  pl.BlockSpec((None, seq, LANES), lambda b, p, i: (b, 0, p))],
        out_specs=pl.BlockSpec((None, tq, LANES), lambda b, p, i: (b, i, p)),
        out_shape=jax.ShapeDtypeStruct((batch, seq, MLA_HEADS * MLA_V), F32),
        compiler_params=_params(("parallel", "parallel", "parallel")),
        name="mla_attn",
    )(q3, k3, v3)


def _dil_attn_kernel(q_ref, k_ref, v_ref, c_ref, o_ref):
    q = q_ref[...]
    k = k_ref[...]
    v = v_ref[...]
    c = c_ref[...]
    lane = lax.broadcasted_iota(jnp.int32, q.shape, 1)
    out = None
    for h in range(2):
        in_head = (lane >= h * DIL_HEAD_DIM) & (lane < (h + 1) * DIL_HEAD_DIM)
        s = _dot_nt(jnp.where(in_head, q, jnp.zeros_like(q)), k)
        s = jnp.where(c > 0.0, s, MASKED_SCORE)
        e = jnp.exp(s - jnp.max(s, axis=-1, keepdims=True)) * c
        o = _dot(e.astype(BF16), v) / jnp.sum(e, axis=-1, keepdims=True)
        out = o if out is None else jnp.where(lane < DIL_HEAD_DIM, out, o)
    o_ref[...] = out


def _dil_multiplicity(seq):
    d = jnp.abs(jnp.arange(seq)[:, None] - jnp.arange(seq)[None, :])
    c = jnp.zeros((seq, seq), F32)
    for window, dil in DIL_PATTERNS:
        c = c + ((d % dil == 0) & (d // dil <= window // (2 * dil))).astype(F32)
    return c


def _dil_attn(qd, kd, vd, batch, seq, tq):
    q3 = qd.reshape(batch, seq, DIL_WIDTH)
    k3 = kd.reshape(batch, seq, DIL_WIDTH)
    v3 = vd.reshape(batch, seq, DIL_WIDTH)
    return pl.pallas_call(
        _dil_attn_kernel,
        grid=(seq // tq, batch, DIL_HEADS // 2),
        in_specs=[pl.BlockSpec((None, tq, LANES), lambda i, b, p: (b, i, p)),
                  pl.BlockSpec((None, seq, LANES), lambda i, b, p: (b, 0, p)),
                  pl.BlockSpec((None, seq, LANES), lambda i, b, p: (b, 0, p)),
                  pl.BlockSpec((tq, seq), lambda i, b, p: (i, 0))],
        out_specs=pl.BlockSpec((None, tq, LANES), lambda i, b, p: (b, i, p)),
        out_shape=jax.ShapeDtypeStruct((batch, seq, DIL_WIDTH), F32),
        compiler_params=_params(("parallel", "parallel", "parallel")),
        name="dil_attn",
    )(q3, k3, v3, _dil_multiplicity(seq))


class _Ranked:
    def __init__(self, r8):
        self.r8 = r8
        self.rows = []
        self.lo = jnp.zeros(r8.shape, F32)
        self.hi = jnp.zeros(r8.shape, F32)

    def push(self, row):
        k = len(self.rows)
        self.rows.append(row)
        if k < 8:
            self.lo = jnp.where(self.r8 == k, row, self.lo)
        else:
            self.hi = jnp.where(self.r8 == k - 8, row, self.hi)


def _top16(vals, key_iota, r8):
    tv, ti = _Ranked(r8), _Ranked(r8)
    for _ in range(PEER_TOPK):
        m = jnp.max(vals, axis=0, keepdims=True)
        idx = jnp.min(jnp.where(vals == m, key_iota, float(PEER_KEYS)), axis=0, keepdims=True)
        tv.push(m)
        ti.push(idx)
        vals = jnp.where(key_iota == idx, -jnp.inf, vals)
    return tv, ti


_POS_SHIFT = float(PEER_EXPERTS)


def _pair_top16(v1, i1, v2, i2, r8):
    ninf = -jnp.inf

    def piece(cand, pos, expert, keep=None):
        if keep is not None:
            cand = jnp.where(keep, cand, ninf)
        return cand, pos * _POS_SHIFT + expert

    pieces = [
        piece(v1.rows[0] + v2.lo, r8, i1.rows[0] * PEER_KEYS + i2.lo),
        piece(v1.rows[0] + v2.hi, r8 + 8.0, i1.rows[0] * PEER_KEYS + i2.hi),
        piece(v1.rows[1] + v2.lo, r8 + 16.0, i1.rows[1] * PEER_KEYS + i2.lo),
        piece(v1.lo + v2.rows[0], r8 * 16.0, i1.lo * PEER_KEYS + i2.rows[0], r8 >= 2),
        piece(v1.hi + v2.rows[0], r8 * 16.0 + 128.0, i1.hi * PEER_KEYS + i2.rows[0]),
        piece(v1.lo + v2.rows[1], r8 * 16.0 + 1.0, i1.lo * PEER_KEYS + i2.rows[1], r8 >= 2),
        piece(v1.rows[2] + v2.lo, r8 + 32.0, i1.rows[2] * PEER_KEYS + i2.lo, (r8 >= 2) & (r8 <= 4)),
        piece(v1.rows[3] + v2.lo, r8 + 48.0, i1.rows[3] * PEER_KEYS + i2.lo, (r8 >= 2) & (r8 <= 3)),
        piece(v1.rows[4] + v2.lo, r8 + 64.0, i1.rows[4] * PEER_KEYS + i2.lo, r8 == 2),
    ]
    cand = jnp.concatenate([p[0] for p in pieces], axis=0)
    key = jnp.concatenate([p[1] for p in pieces], axis=0)
    top, sel = _Ranked(r8), _Ranked(r8)
    for _ in range(PEER_TOPK):
        m = jnp.max(cand, axis=0, keepdims=True)
        kk = jnp.min(jnp.where(cand == m, key, 3.0e38), axis=0, keepdims=True)
        top.push(m)
        sel.push(kk)
        cand = jnp.where(key == kk, ninf, cand)
    return top, sel


def _route_kernel(om_ref, od_ref, x_ref, gom_ref, god_ref, wom_ref, wod_ref, gffn_ref,
                  wq_ref, kdt_ref,
                  h_ref, xn_ref, ra_ref, rb_ref, rg_ref,
                  sc_ref, sa_ref, sb_ref, sg_ref):
    tile = x_ref.shape[0]
    nm = _rms(om_ref[...], gom_ref[...]).astype(BF16)
    nd = _rms(od_ref[...], god_ref[...]).astype(BF16)
    h = x_ref[...] + _dot(nm, wom_ref[...]) + _dot(nd, wod_ref[...])
    h_ref[...] = h
    xn = _rms(h, gffn_ref[...]).astype(BF16)
    xn_ref[...] = xn
    q = _dot(xn, wq_ref[...])
    kdt = kdt_ref[...]
    for hd in range(PEER_HEADS):
        qh = q[:, hd * LANES:(hd + 1) * LANES].astype(BF16)
        sc_ref[hd] = _dot_nt(kdt, qh)

    key_iota = lax.broadcasted_iota(jnp.int32, (PEER_KEYS, tile), 0).astype(F32)
    r8 = lax.broadcasted_iota(jnp.int32, (8, tile), 0).astype(F32)

    def head(hd, carry):
        v1, i1 = _top16(sc_ref[hd, :PEER_KEYS, :], key_iota, r8)
        v2, i2 = _top16(sc_ref[hd, PEER_KEYS:, :], key_iota, r8)
        top, sel = _pair_top16(v1, i1, v2, i2, r8)
        e_lo = jnp.exp(top.lo - top.rows[0])
        e_hi = jnp.exp(top.hi - top.rows[0])
        inv = 1.0 / (jnp.sum(e_lo, axis=0, keepdims=True) + jnp.sum(e_hi, axis=0, keepdims=True))
        for half, (s, e) in enumerate(((sel.lo, e_lo), (sel.hi, e_hi))):
            rows = slice(half * 8, half * 8 + 8)
            expert = s - jnp.floor(s * (1.0 / _POS_SHIFT)) * _POS_SHIFT
            a = jnp.floor(expert * (1.0 / PEER_KEYS))
            sa_ref[hd, rows, :] = a
            sb_ref[hd, rows, :] = expert - a * PEER_KEYS
            sg_ref[hd, rows, :] = e * inv
        return carry

    lax.fori_loop(0, PEER_HEADS, head, 0)
    a_all = sa_ref[...].reshape(PEER_SLOTS, tile)
    b_all = sb_ref[...].reshape(PEER_SLOTS, tile)
    g_all = sg_ref[...].reshape(PEER_SLOTS, tile)
    for c in range(tile // LANES):
        cols = slice(c * LANES, (c + 1) * LANES)
        ra_ref[cols, :] = a_all[:, cols].T
        rb_ref[cols, :] = b_all[:, cols].T
        rg_ref[cols, :] = g_all[:, cols].T


def _route(o_mla, o_dil, x2, g_om, g_od, w_om, w_od, g_ffn, w_q, kdt, tile):
    T = x2.shape[0]
    full = lambda a: pl.BlockSpec(a.shape, lambda i: (0,) * a.ndim)
    row = lambda w: pl.BlockSpec((tile, w), lambda i: (i, 0))
    return pl.pallas_call(
        _route_kernel,
        grid=(T // tile,),
        in_specs=[row(o_mla.shape[1]), row(o_dil.shape[1]), row(D_MODEL), full(g_om), full(g_od),
                  full(w_om), full(w_od), full(g_ffn), full(w_q), full(kdt)],
        out_specs=[row(D_MODEL), row(D_MODEL), row(PEER_SLOTS), row(PEER_SLOTS), row(PEER_SLOTS)],
        out_shape=[jax.ShapeDtypeStruct((T, D_MODEL), F32),
                   jax.ShapeDtypeStruct((T, D_MODEL), BF16),
                   jax.ShapeDtypeStruct((T, PEER_SLOTS), F32),
                   jax.ShapeDtypeStruct((T, PEER_SLOTS), F32),
                   jax.ShapeDtypeStruct((T, PEER_SLOTS), F32)],
        scratch_shapes=[pltpu.VMEM((PEER_HEADS, 2 * PEER_KEYS, tile), F32),
                        pltpu.VMEM((PEER_HEADS, PEER_TOPK, tile), F32),
                        pltpu.VMEM((PEER_HEADS, PEER_TOPK, tile), F32),
                        pltpu.VMEM((PEER_HEADS, PEER_TOPK, tile), F32)],
        compiler_params=_params(("parallel",)),
        name="route",
    )(o_mla, o_dil, x2, g_om, g_od, w_om, w_od, g_ffn, w_q, kdt)


GATE_PITCH = PEER_KEYS + 8


def _gate_build_kernel(ra_ref, rb_ref, rg_ref, g_ref, stage_ref):
    key_iota = lax.broadcasted_iota(jnp.int32, (PEER_KEYS, PEER_SLOTS), 0).astype(F32)
    tile = ra_ref.shape[0]

    def token(t, carry):
        a = ra_ref[pl.ds(t, 1), :]
        b = rb_ref[pl.ds(t, 1), :]
        g = rg_ref[pl.ds(t, 1), :]
        g_hi = g.astype(BF16).astype(F32)
        one_a = jnp.where(key_iota == a, 1.0, 0.0).astype(BF16)
        hit_b = key_iota == b
        gb_hi = jnp.where(hit_b, g_hi, 0.0).astype(BF16)
        gb_lo = jnp.where(hit_b, g - g_hi, 0.0).astype(BF16)
        lhs = jnp.concatenate([one_a, one_a], axis=1)
        rhs = jnp.concatenate([gb_hi, gb_lo], axis=1)
        row0 = pl.multiple_of(t * GATE_PITCH, 8)
        stage_ref[pl.ds(row0, PEER_KEYS), :] = _dot_nt(lhs, rhs)
        return carry

    lax.fori_loop(0, tile, token, 0)
    for a in range(PEER_KEYS):
        g_ref[a] = stage_ref[pl.ds(a, tile, stride=GATE_PITCH), :]


def _gate_build(ra, rb, rg, tile):
    T = ra.shape[0]
    row = pl.BlockSpec((tile, PEER_SLOTS), lambda i: (i, 0))
    return pl.pallas_call(
        _gate_build_kernel,
        grid=(T // tile,),
        in_specs=[row, row, row],
        out_specs=pl.BlockSpec((PEER_KEYS, tile, PEER_KEYS), lambda i: (0, i, 0)),
        out_shape=jax.ShapeDtypeStruct((PEER_KEYS, T, PEER_KEYS), F32),
        scratch_shapes=[pltpu.VMEM((tile * GATE_PITCH, PEER_KEYS), F32)],
        compiler_params=_params(("parallel",)),
        name="gate_build",
    )(ra, rb, rg)


def _peer_kernel(xn_ref, ut_ref, v_ref, g_ref, h_ref, o_ref, acc_ref, *, a_per_step):
    j = pl.program_id(1)

    @pl.when(j == 0)
    def _():
        acc_ref[...] = jnp.zeros_like(acc_ref)

    pre = _dot(xn_ref[...], ut_ref[...])
    act = 0.5 * pre * (1.0 + lax.erf(pre * float(1.0 / np.sqrt(2.0))))
    gate = jnp.concatenate([g_ref[a] for a in range(a_per_step)], axis=1)
    acc_ref[...] += _dot((act * gate).astype(BF16), v_ref[...])

    @pl.when(j == pl.num_programs(1) - 1)
    def _():
        o_ref[...] = h_ref[...] + acc_ref[...]


def _peer(xn, u_t, v_b, gates, h1, tile, a_per_step):
    T = xn.shape[0]
    et = a_per_step * PEER_KEYS
    return pl.pallas_call(
        functools.partial(_peer_kernel, a_per_step=a_per_step),
        grid=(T // tile, PEER_EXPERTS // et),
        in_specs=[pl.BlockSpec((tile, D_MODEL), lambda i, j: (i, 0)),
                  pl.BlockSpec((D_MODEL, et), lambda i, j: (0, j)),
                  pl.BlockSpec((et, D_MODEL), lambda i, j: (j, 0)),
                  pl.BlockSpec((a_per_step, tile, PEER_KEYS), lambda i, j: (j, i, 0)),
                  pl.BlockSpec((tile, D_MODEL), lambda i, j: (i, 0))],
        out_specs=pl.BlockSpec((tile, D_MODEL), lambda i, j: (i, 0)),
        out_shape=jax.ShapeDtypeStruct((T, D_MODEL), F32),
        scratch_shapes=[pltpu.VMEM((tile, D_MODEL), F32)],
        compiler_params=_params(("parallel", "arbitrary")),
        name="peer",
    )(xn, u_t, v_b, gates, h1)


def _ple_final_kernel(h_ref, p_ref, gple_ref, wg_ref, wp_ref, gfin_ref, o_ref, *, last_layer):
    h = h_ref[...]
    gate = jax.nn.sigmoid(_dot(_rms(h, gple_ref[...]).astype(BF16), wg_ref[...]))
    h = h + gate * _dot(p_ref[...].astype(BF16), wp_ref[...])
    o_ref[...] = _rms(h, gfin_ref[...]) if last_layer else h


def _ple_final(h2, p2, g_ple, w_gate, w_proj, g_final, tile, last_layer):
    T = h2.shape[0]
    full = lambda a: pl.BlockSpec(a.shape, lambda i: (0,) * a.ndim)
    row = lambda w: pl.BlockSpec((tile, w), lambda i: (i, 0))
    return pl.pallas_call(
        functools.partial(_ple_final_kernel, last_layer=last_layer),
        grid=(T // tile,),
        in_specs=[row(D_MODEL), row(PLE_DIM), full(g_ple), full(w_gate), full(w_proj), full(g_final)],
        out_specs=row(D_MODEL),
        out_shape=jax.ShapeDtypeStruct((T, D_MODEL), F32),
        compiler_params=_params(("parallel",)),
        name="ple_final",
    )(h2, p2, g_ple, w_gate, w_proj, g_final)


def _rope_lane_tables(seq, rot_dim, first_lane, period, scale):
    half = rot_dim // 2
    inv = ROPE_THETA ** (-jnp.arange(0, rot_dim, 2, dtype=F32) / rot_dim)
    ang = jnp.arange(seq, dtype=F32)[:, None] * inv[None, :]
    cos, sin = jnp.cos(ang), jnp.sin(ang)
    c = jnp.ones((seq, period), F32)
    c = c.at[:, first_lane:first_lane + half].set(cos).at[:, first_lane + half:first_lane + rot_dim].set(cos)
    s_lo = jnp.zeros((seq, period), F32).at[:, first_lane + half:first_lane + rot_dim].set(sin)
    s_hi = jnp.zeros((seq, period), F32).at[:, first_lane:first_lane + half].set(-sin)
    tabs = jnp.stack([c, s_lo, s_hi]) * scale
    return jnp.tile(tabs, (1, 1, LANES // period))


def _pick_tile(n, want):
    t = min(n, want)
    assert n % t == 0, (n, t)
    return t


def kernel(x, p, g_mix, w_in, g_cq, w_uq, g_ckv, w_ukv, g_out_mla, g_out_dil, w_out, g_ffn,
           w_peer_q, peer_keys1, peer_keys2, peer_u, peer_v, g_ple, w_ple_gate, w_ple_proj, g_final):
    batch, seq, _ = x.shape
    depth = p.shape[0]
    T = batch * seq
    mla_w = MLA_HEADS * MLA_V

    tabs = (_rope_lane_tables(seq, MLA_ROPE, MLA_NOPE, LANES, MLA_QK ** -0.5),
            _rope_lane_tables(seq, MLA_ROPE, MLA_NOPE, LANES, 1.0),
            _rope_lane_tables(seq, DIL_ROT, 0, DIL_HEAD_DIM, DIL_HEAD_DIM ** -0.5),
            _rope_lane_tables(seq, DIL_ROT, 0, DIL_HEAD_DIM, 1.0))

    h = x.reshape(T, D_MODEL)
    for i in range(depth):
        wi = w_in[i]
        o1 = MLA_Q_RANK + MLA_KV_RANK
        k_r_cols = jnp.pad(wi[:, o1:o1 + MLA_ROPE], ((0, 0), (MLA_NOPE, LANES - MLA_QK)))
        w_in_p = jnp.concatenate([wi[:, :o1], k_r_cols, wi[:, o1 + MLA_ROPE:]], axis=1).astype(BF16)
        w_uq_p = jnp.pad(w_uq[i].reshape(MLA_Q_RANK, MLA_HEADS, MLA_QK),
                         ((0, 0), (0, 0), (0, LANES - MLA_QK))).reshape(MLA_Q_RANK, -1).astype(BF16)
        w_ukv3 = w_ukv[i].reshape(MLA_KV_RANK, MLA_HEADS, MLA_NOPE + MLA_V)
        w_uk_p = jnp.pad(w_ukv3[:, :, :MLA_NOPE],
                         ((0, 0), (0, 0), (0, LANES - MLA_NOPE))).reshape(MLA_KV_RANK, -1).astype(BF16)
        w_uv = w_ukv3[:, :, MLA_NOPE:].reshape(MLA_KV_RANK, -1).astype(BF16)
        zeros = jnp.zeros((PEER_KEYS, PEER_HALF), F32)
        kdt = jnp.concatenate([jnp.concatenate([peer_keys1[i], zeros], axis=1),
                               jnp.concatenate([zeros, peer_keys2[i]], axis=1)], axis=0).astype(BF16)

        t_proj = _pick_tile(seq, 512)
        qm, km, vm, qd, kd, vd = _in_proj(
            h, g_mix[i][None], w_in_p, g_cq[i][None], w_uq_p, g_ckv[i][None], w_uk_p, w_uv,
            tabs, seq, t_proj)
        tq = _pick_tile(seq, 256)
        o_mla = _mla_attn(qm, km, vm, batch, seq, tq).reshape(T, mla_w)
        o_dil = _dil_attn(qd, kd, vd, batch, seq, tq).reshape(T, DIL_WIDTH)

        h1, xn, ra, rb, rg = _route(
            o_mla, o_dil, h, g_out_mla[i][None], g_out_dil[i][None],
            w_out[i][:mla_w].astype(BF16), w_out[i][mla_w:].astype(BF16), g_ffn[i][None],
            w_peer_q[i].astype(BF16), kdt, _pick_tile(T, 256))
        gates = _gate_build(ra, rb, rg, _pick_tile(T, 64))
        h2 = _peer(xn, peer_u[i].T.astype(BF16), peer_v[i].astype(BF16), gates, h1,
                   _pick_tile(T, 512), 8)
        h = _ple_final(h2, p[i].reshape(T, PLE_DIM), g_ple[i][None], w_ple_gate[i].astype(BF16),
                       w_ple_proj[i].astype(BF16), g_final[None], _pick_tile(T, 512),
                       last_layer=(i == depth - 1))
    return h.reshape(batch, seq, D_MODEL)
```

```python
import functools

import numpy as np
import jax
import jax.numpy as jnp
from jax import lax
from jax.experimental import pallas as pl
from jax.experimental.pallas import tpu as pltpu

F32 = jnp.float32
BF16 = jnp.bfloat16

EPS = 1e-6
MASKED_SCORE = -1e30
ROPE_THETA = 500000.0

LANES = 128
D_MODEL = 1024
PLE_DIM = 256
MLA_HEADS = 8
MLA_NOPE = 64
MLA_ROPE = 32
MLA_V = 64
MLA_QK = MLA_NOPE + MLA_ROPE
MLA_Q_RANK = 384
MLA_KV_RANK = 256
DIL_HEADS = 8
DIL_HEAD_DIM = 64
DIL_ROT = 16
DIL_WIDTH = DIL_HEADS * DIL_HEAD_DIM
DIL_PATTERNS = ((128, 1), (512, 4), (2048, 16))
PEER_KEYS = 128
PEER_HEADS = 8
PEER_HALF = 64
PEER_TOPK = 16
PEER_SLOTS = PEER_HEADS * PEER_TOPK
PEER_EXPERTS = PEER_KEYS * PEER_KEYS

VMEM_LIMIT = 48 * 1024 * 1024


def _params(semantics):
    return pltpu.CompilerParams(dimension_semantics=semantics, vmem_limit_bytes=VMEM_LIMIT)


def _rms(x, g):
    return x * lax.rsqrt(jnp.mean(x * x, axis=-1, keepdims=True) + EPS) * g


def _rope(x, tab_ref, half):
    return (x * tab_ref[0]
            + pltpu.roll(x, half, 1) * tab_ref[1]
            + pltpu.roll(x, LANES - half, 1) * tab_ref[2])


def _dot(a, b):
    return jnp.dot(a, b, preferred_element_type=F32)


def _dot_nt(a, b):
    return lax.dot_general(a, b, (((1,), (1,)), ((), ())), preferred_element_type=F32)


def _in_proj_kernel(x_ref, gmix_ref, win_ref, gcq_ref, wuq_ref, gckv_ref, wuk_ref, wuv_ref,
                    tmq_ref, tmk_ref, tdq_ref, tdk_ref,
                    qm_ref, km_ref, vm_ref, qd_ref, kd_ref, vd_ref):
    hn = _rms(x_ref[...], gmix_ref[...]).astype(BF16)
    y = _dot(hn, win_ref[...])
    o = 0
    c_q = y[:, o:o + MLA_Q_RANK]; o += MLA_Q_RANK
    c_kv = y[:, o:o + MLA_KV_RANK]; o += MLA_KV_RANK
    k_r = y[:, o:o + LANES]; o += LANES
    q_d = y[:, o:o + DIL_WIDTH]; o += DIL_WIDTH
    k_d = y[:, o:o + DIL_WIDTH]; o += DIL_WIDTH
    v_d = y[:, o:o + DIL_WIDTH]

    q = _dot(_rms(c_q, gcq_ref[...]).astype(BF16), wuq_ref[...])
    ckvn = _rms(c_kv, gckv_ref[...]).astype(BF16)
    k_n = _dot(ckvn, wuk_ref[...])
    vm_ref[...] = _dot(ckvn, wuv_ref[...]).astype(BF16)
    k_rope = _rope(k_r, tmk_ref, MLA_ROPE // 2)
    for h in range(MLA_HEADS):
        sl = slice(h * LANES, (h + 1) * LANES)
        qm_ref[:, sl] = _rope(q[:, sl], tmq_ref, MLA_ROPE // 2).astype(BF16)
        km_ref[:, sl] = (k_n[:, sl] + k_rope).astype(BF16)
    for c in range(DIL_WIDTH // LANES):
        sl = slice(c * LANES, (c + 1) * LANES)
        qd_ref[:, sl] = _rope(q_d[:, sl], tdq_ref, DIL_ROT // 2)
        kd_ref[:, sl] = _rope(k_d[:, sl], tdk_ref, DIL_ROT // 2)
    vd_ref[...] = v_d


def _in_proj(x2, g_mix, w_in_p, g_cq, w_uq_p, g_ckv, w_uk_p, w_uv, tabs, seq, tile):
    T = x2.shape[0]
    n_pos = seq // tile
    full = lambda a: pl.BlockSpec(a.shape, lambda i: (0,) * a.ndim)
    tab = pl.BlockSpec((3, tile, LANES), lambda i: (0, i % n_pos, 0))
    row = lambda w: pl.BlockSpec((tile, w), lambda i: (i, 0))
    outs = [(T, MLA_HEADS * LANES), (T, MLA_HEADS * LANES), (T, MLA_HEADS * MLA_V),
            (T, DIL_WIDTH), (T, DIL_WIDTH), (T, DIL_WIDTH)]
    return pl.pallas_call(
        _in_proj_kernel,
        grid=(T // tile,),
        in_specs=[row(D_MODEL), full(g_mix), full(w_in_p), full(g_cq), full(w_uq_p),
                  full(g_ckv), full(w_uk_p), full(w_uv), tab, tab, tab, tab],
        out_specs=[row(s[1]) for s in outs],
        out_shape=[jax.ShapeDtypeStruct(s, BF16) for s in outs[:3]]
                  + [jax.ShapeDtypeStruct(s, F32) for s in outs[3:]],
        compiler_params=_params(("parallel",)),
        name="in_proj",
    )(x2, g_mix, w_in_p, g_cq, w_uq_p, g_ckv, w_uk_p, w_uv, *tabs)


def _mla_attn_kernel(q_ref, k_ref, v_ref, o_ref):
    v = v_ref[...]
    lane = lax.broadcasted_iota(jnp.int32, (q_ref.shape[0], LANES), 1)
    out = None
    for h in range(2):
        sl = slice(h * LANES, (h + 1) * LANES)
        s = _dot_nt(q_ref[:, sl], k_ref[:, sl])
        e = jnp.exp(s - jnp.max(s, axis=-1, keepdims=True))
        o = _dot(e.astype(BF16), v) / jnp.sum(e, axis=-1, keepdims=True)
        out = o if out is None else jnp.where(lane < MLA_V, out, o)
    o_ref[...] = out


def _mla_attn(qm, km, vm, batch, seq, tq):
    q3 = qm.reshape(batch, seq, MLA_HEADS * LANES)
    k3 = km.reshape(batch, seq, MLA_HEADS * LANES)
    v3 = vm.reshape(batch, seq, MLA_HEADS * MLA_V)
    return pl.pallas_call(
        _mla_attn_kernel,
        grid=(batch, MLA_HEADS // 2, seq // tq),
        in_specs=[pl.BlockSpec((None, tq, 2 * LANES), lambda b, p, i: (b, i, p)),
                  pl.BlockSpec((None, seq, 2 * LANES), lambda b, p, i: (b, 0, p)),
                  pl.BlockSpec((None, seq, LANES), lambda b, p, i: (b, 0, p))],
        out_specs=pl.BlockSpec((None, tq, LANES), lambda b, p, i: (b, i, p)),
        out_shape=jax.ShapeDtypeStruct((batch, seq, MLA_HEADS * MLA_V), F32),
        compiler_params=_params(("parallel", "parallel", "parallel")),
        name="mla_attn",
    )(q3, k3, v3)


DIL_Q_BLOCK = 128
DIL_K_WINDOW = 256
DIL_UNROLL = 4


def _dil_attn_kernel(q_ref, k_ref, v_ref, o_ref, acc_ref, m_ref, l_ref):
    seq = q_ref.shape[0]
    lane = lax.broadcasted_iota(jnp.int32, (1, LANES), 1)
    head0 = lane < DIL_HEAD_DIM

    for p, (window, dil) in enumerate(DIL_PATTERNS):
        sub_len = seq // dil
        radius = window // (2 * dil)
        qb = min(DIL_Q_BLOCK, sub_len)
        kw = min(DIL_K_WINDOW, sub_len)
        assert kw == sub_len or kw >= qb + 2 * radius
        blocks = sub_len // qb
        rel = (lax.broadcasted_iota(jnp.int32, (qb, kw), 0)
               - lax.broadcasted_iota(jnp.int32, (qb, kw), 1))

        def rows(first, count, dil=dil):
            start = pl.multiple_of(first, 8) if dil == 1 else first
            return pl.ds(start, count) if dil == 1 else pl.ds(start, count, stride=dil)

        def block(n, carry, p=p, dil=dil, radius=radius, qb=qb, kw=kw, blocks=blocks,
                  sub_len=sub_len, rel=rel, rows=rows):
            res = n // blocks
            j0 = (n % blocks) * qb
            k0 = jnp.clip(j0 - (kw - qb) // 2, 0, sub_len - kw)
            q = q_ref[rows(res + dil * j0, qb), :]
            k = k_ref[rows(res + dil * k0, kw), :].astype(BF16)
            v = v_ref[rows(res + dil * k0, kw), :].astype(BF16)
            near = jnp.abs(rel + (j0 - k0)) <= radius
            acc, m_all, l_all = None, None, None
            for h in range(2):
                qh = jnp.where(head0 if h == 0 else ~head0, q, 0.0).astype(BF16)
                s = jnp.where(near, _dot_nt(qh, k), MASKED_SCORE)
                m = jnp.max(s, axis=-1, keepdims=True)
                e = jnp.exp(s - m)
                l = jnp.sum(e, axis=-1, keepdims=True)
                a = _dot(e.astype(BF16), v)
                if h == 0:
                    acc, m_all, l_all = a, m, l
                else:
                    acc = jnp.where(head0, acc, a)
                    m_all = jnp.where(head0, m_all, m)
                    l_all = jnp.where(head0, l_all, l)
            out_rows = rows(res + dil * j0, qb)
            acc_ref[p, out_rows, :] = acc
            m_ref[p, out_rows, :] = m_all
            l_ref[p, out_rows, :] = l_all
            return carry

        lax.fori_loop(0, dil * blocks, block, 0, unroll=min(DIL_UNROLL, dil * blocks))

    m = m_ref[...]
    w = jnp.exp(m - jnp.max(m, axis=0, keepdims=True))
    o_ref[...] = jnp.sum(w * acc_ref[...], axis=0) / jnp.sum(w * l_ref[...], axis=0)


def _dil_attn(qd, kd, vd, batch, seq):
    spec = pl.BlockSpec((None, seq, LANES), lambda b, p: (b, 0, p))
    stats = pltpu.VMEM((len(DIL_PATTERNS), seq, LANES), F32)
    return pl.pallas_call(
        _dil_attn_kernel,
        grid=(batch, DIL_HEADS // 2),
        in_specs=[spec, spec, spec],
        out_specs=spec,
        out_shape=jax.ShapeDtypeStruct((batch, seq, DIL_WIDTH), F32),
        scratch_shapes=[stats, stats, stats],
        compiler_params=_params(("parallel", "parallel")),
        name="dil_attn",
    )(qd.reshape(batch, seq, DIL_WIDTH), kd.reshape(batch, seq, DIL_WIDTH),
      vd.reshape(batch, seq, DIL_WIDTH))


class _Ranked:
    def __init__(self, r8):
        self.r8 = r8
        self.rows = []
        self.lo = jnp.zeros(r8.shape, F32)
        self.hi = jnp.zeros(r8.shape, F32)

    def push(self, row):
        k = len(self.rows)
        self.rows.append(row)
        if k < 8:
            self.lo = jnp.where(self.r8 == k, row, self.lo)
        else:
            self.hi = jnp.where(self.r8 == k - 8, row, self.hi)


def _top16(vals, key_iota, r8):
    tv, ti = _Ranked(r8), _Ranked(r8)
    for _ in range(PEER_TOPK):
        m = jnp.max(vals, axis=0, keepdims=True)
        idx = jnp.min(jnp.where(vals == m, key_iota, float(PEER_KEYS)), axis=0, keepdims=True)
        tv.push(m)
        ti.push(idx)
        vals = jnp.where(key_iota == idx, -jnp.inf, vals)
    return tv, ti


_POS_SHIFT = float(PEER_EXPERTS)


def _pair_top16(v1, i1, v2, i2, r8):
    ninf = -jnp.inf

    def piece(cand, pos, expert, keep=None):
        if keep is not None:
            cand = jnp.where(keep, cand, ninf)
        return cand, pos * _POS_SHIFT + expert

    pieces = [
        piece(v1.rows[0] + v2.lo, r8, i1.rows[0] * PEER_KEYS + i2.lo),
        piece(v1.rows[0] + v2.hi, r8 + 8.0, i1.rows[0] * PEER_KEYS + i2.hi),
        piece(v1.rows[1] + v2.lo, r8 + 16.0, i1.rows[1] * PEER_KEYS + i2.lo),
        piece(v1.lo + v2.rows[0], r8 * 16.0, i1.lo * PEER_KEYS + i2.rows[0], r8 >= 2),
        piece(v1.hi + v2.rows[0], r8 * 16.0 + 128.0, i1.hi * PEER_KEYS + i2.rows[0]),
        piece(v1.lo + v2.rows[1], r8 * 16.0 + 1.0, i1.lo * PEER_KEYS + i2.rows[1], r8 >= 2),
        piece(v1.rows[2] + v2.lo, r8 + 32.0, i1.rows[2] * PEER_KEYS + i2.lo, (r8 >= 2) & (r8 <= 4)),
        piece(v1.rows[3] + v2.lo, r8 + 48.0, i1.rows[3] * PEER_KEYS + i2.lo, (r8 >= 2) & (r8 <= 3)),
        piece(v1.rows[4] + v2.lo, r8 + 64.0, i1.rows[4] * PEER_KEYS + i2.lo, r8 == 2),
    ]
    cand = jnp.concatenate([p[0] for p in pieces], axis=0)
    key = jnp.concatenate([p[1] for p in pieces], axis=0)
    top, sel = _Ranked(r8), _Ranked(r8)
    for _ in range(PEER_TOPK):
        m = jnp.max(cand, axis=0, keepdims=True)
        kk = jnp.min(jnp.where(cand == m, key, 3.0e38), axis=0, keepdims=True)
        top.push(m)
        sel.push(kk)
        cand = jnp.where(key == kk, ninf, cand)
    return top, sel


def _route_kernel(om_ref, od_ref, x_ref, gom_ref, god_ref, wom_ref, wod_ref, gffn_ref,
                  wq_ref, kdt_ref,
                  h_ref, xn_ref, ra_ref, rb_ref, rg_ref,
                  sc_ref, sa_ref, sb_ref, sg_ref):
    tile = x_ref.shape[0]
    nm = _rms(om_ref[...], gom_ref[...]).astype(BF16)
    nd = _rms(od_ref[...], god_ref[...]).astype(BF16)
    h = x_ref[...] + _dot(nm, wom_ref[...]) + _dot(nd, wod_ref[...])
    h_ref[...] = h
    xn = _rms(h, gffn_ref[...]).astype(BF16)
    xn_ref[...] = xn
    q = _dot(xn, wq_ref[...])
    kdt = kdt_ref[...]
    for hd in range(PEER_HEADS):
        qh = q[:, hd * LANES:(hd + 1) * LANES].astype(BF16)
        sc_ref[hd] = _dot_nt(kdt, qh)

    key_iota = lax.broadcasted_iota(jnp.int32, (PEER_KEYS, tile), 0).astype(F32)
    r8 = lax.broadcasted_iota(jnp.int32, (8, tile), 0).astype(F32)

    def head(hd, carry):
        v1, i1 = _top16(sc_ref[hd, :PEER_KEYS, :], key_iota, r8)
        v2, i2 = _top16(sc_ref[hd, PEER_KEYS:, :], key_iota, r8)
        top, sel = _pair_top16(v1, i1, v2, i2, r8)
        e_lo = jnp.exp(top.lo - top.rows[0])
        e_hi = jnp.exp(top.hi - top.rows[0])
        inv = 1.0 / (jnp.sum(e_lo, axis=0, keepdims=True) + jnp.sum(e_hi, axis=0, keepdims=True))
        for half, (s, e) in enumerate(((sel.lo, e_lo), (sel.hi, e_hi))):
            rows = slice(half * 8, half * 8 + 8)
            expert = s - jnp.floor(s * (1.0 / _POS_SHIFT)) * _POS_SHIFT
            a = jnp.floor(expert * (1.0 / PEER_KEYS))
            sa_ref[hd, rows, :] = a
            sb_ref[hd, rows, :] = expert - a * PEER_KEYS
            sg_ref[hd, rows, :] = e * inv
        return carry

    lax.fori_loop(0, PEER_HEADS, head, 0)
    a_all = sa_ref[...].reshape(PEER_SLOTS, tile)
    b_all = sb_ref[...].reshape(PEER_SLOTS, tile)
    g_all = sg_ref[...].reshape(PEER_SLOTS, tile)
    for c in range(tile // LANES):
        cols = slice(c * LANES, (c + 1) * LANES)
        ra_ref[cols, :] = a_all[:, cols].T
        rb_ref[cols, :] = b_all[:, cols].T
        rg_ref[cols, :] = g_all[:, cols].T


def _route(o_mla, o_dil, x2, g_om, g_od, w_om, w_od, g_ffn, w_q, kdt, tile):
    T = x2.shape[0]
    full = lambda a: pl.BlockSpec(a.shape, lambda i: (0,) * a.ndim)
    row = lambda w: pl.BlockSpec((tile, w), lambda i: (i, 0))
    return pl.pallas_call(
        _route_kernel,
        grid=(T // tile,),
        in_specs=[row(o_mla.shape[1]), row(o_dil.shape[1]), row(D_MODEL), full(g_om), full(g_od),
                  full(w_om), full(w_od), full(g_ffn), full(w_q), full(kdt)],
        out_specs=[row(D_MODEL), row(D_MODEL), row(PEER_SLOTS), row(PEER_SLOTS), row(PEER_SLOTS)],
        out_shape=[jax.ShapeDtypeStruct((T, D_MODEL), F32),
                   jax.ShapeDtypeStruct((T, D_MODEL), BF16),
                   jax.ShapeDtypeStruct((T, PEER_SLOTS), F32),
                   jax.ShapeDtypeStruct((T, PEER_SLOTS), F32),
                   jax.ShapeDtypeStruct((T, PEER_SLOTS), F32)],
        scratch_shapes=[pltpu.VMEM((PEER_HEADS, 2 * PEER_KEYS, tile), F32),
                        pltpu.VMEM((PEER_HEADS, PEER_TOPK, tile), F32),
                        pltpu.VMEM((PEER_HEADS, PEER_TOPK, tile), F32),
                        pltpu.VMEM((PEER_HEADS, PEER_TOPK, tile), F32)],
        compiler_params=_params(("parallel",)),
        name="route",
    )(o_mla, o_dil, x2, g_om, g_od, w_om, w_od, g_ffn, w_q, kdt)


GATE_PITCH = PEER_KEYS + 8
GATE_UNROLL = 8


def _gate_build_kernel(ra_ref, rb_ref, rg_ref, g_ref, stage_ref):
    key_iota = lax.broadcasted_iota(jnp.int32, (PEER_KEYS, PEER_SLOTS), 0).astype(F32)
    tile = ra_ref.shape[0]

    def token(t, carry):
        a = ra_ref[pl.ds(t, 1), :]
        b = rb_ref[pl.ds(t, 1), :]
        g = rg_ref[pl.ds(t, 1), :]
        one_a = jnp.where(key_iota == a, 1.0, 0.0).astype(BF16)
        g_at_b = jnp.where(key_iota == b, g, 0.0).astype(BF16)
        row0 = pl.multiple_of(t * GATE_PITCH, 8)
        stage_ref[pl.ds(row0, PEER_KEYS), :] = _dot_nt(one_a, g_at_b)
        return carry

    lax.fori_loop(0, tile, token, 0, unroll=GATE_UNROLL)
    for a in range(PEER_KEYS):
        g_ref[a] = stage_ref[pl.ds(a, tile, stride=GATE_PITCH), :].astype(BF16)


def _gate_build(ra, rb, rg, tile):
    T = ra.shape[0]
    row = pl.BlockSpec((tile, PEER_SLOTS), lambda i: (i, 0))
    return pl.pallas_call(
        _gate_build_kernel,
        grid=(T // tile,),
        in_specs=[row, row, row],
        out_specs=pl.BlockSpec((PEER_KEYS, tile, PEER_KEYS), lambda i: (0, i, 0)),
        out_shape=jax.ShapeDtypeStruct((PEER_KEYS, T, PEER_KEYS), BF16),
        scratch_shapes=[pltpu.VMEM((tile * GATE_PITCH, PEER_KEYS), F32)],
        compiler_params=_params(("parallel",)),
        name="gate_build",
    )(ra, rb, rg)


def _peer_kernel(xn_ref, ut_ref, v_ref, g_ref, o_ref, *, a_per_step):
    pre = _dot(xn_ref[...], ut_ref[...])
    act = 0.5 * pre * (1.0 + lax.erf(pre * float(1.0 / np.sqrt(2.0))))
    gate = jnp.concatenate([g_ref[a] for a in range(a_per_step)], axis=1)
    y = _dot((act * gate.astype(F32)).astype(BF16), v_ref[...])

    @pl.when(pl.program_id(1) == 0)
    def _():
        o_ref[...] = y

    @pl.when(pl.program_id(1) != 0)
    def _():
        o_ref[...] += y


def _peer(xn, u_t, v_b, gates, tile, a_per_step):
    T = xn.shape[0]
    et = a_per_step * PEER_KEYS
    return pl.pallas_call(
        functools.partial(_peer_kernel, a_per_step=a_per_step),
        grid=(T // tile, PEER_EXPERTS // et),
        in_specs=[pl.BlockSpec((tile, D_MODEL), lambda i, j: (i, 0)),
                  pl.BlockSpec((D_MODEL, et), lambda i, j: (0, j)),
                  pl.BlockSpec((et, D_MODEL), lambda i, j: (j, 0)),
                  pl.BlockSpec((a_per_step, tile, PEER_KEYS), lambda i, j: (j, i, 0))],
        out_specs=pl.BlockSpec((tile, D_MODEL), lambda i, j: (i, 0)),
        out_shape=jax.ShapeDtypeStruct((T, D_MODEL), F32),
        compiler_params=_params(("parallel", "arbitrary")),
        name="peer",
    )(xn, u_t, v_b, gates)


def _ple_final_kernel(h_ref, y_ref, p_ref, gple_ref, wg_ref, wp_ref, gfin_ref, o_ref, *, last_layer):
    h = h_ref[...] + y_ref[...]
    gate = jax.nn.sigmoid(_dot(_rms(h, gple_ref[...]).astype(BF16), wg_ref[...]))
    h = h + gate * _dot(p_ref[...].astype(BF16), wp_ref[...])
    o_ref[...] = _rms(h, gfin_ref[...]) if last_layer else h


def _ple_final(h1, y, p2, g_ple, w_gate, w_proj, g_final, tile, last_layer):
    T = h1.shape[0]
    full = lambda a: pl.BlockSpec(a.shape, lambda i: (0,) * a.ndim)
    row = lambda w: pl.BlockSpec((tile, w), lambda i: (i, 0))
    return pl.pallas_call(
        functools.partial(_ple_final_kernel, last_layer=last_layer),
        grid=(T // tile,),
        in_specs=[row(D_MODEL), row(D_MODEL), row(PLE_DIM), full(g_ple), full(w_gate),
                  full(w_proj), full(g_final)],
        out_specs=row(D_MODEL),
        out_shape=jax.ShapeDtypeStruct((T, D_MODEL), F32),
        compiler_params=_params(("parallel",)),
        name="ple_final",
    )(h1, y, p2, g_ple, w_gate, w_proj, g_final)


def _rope_lane_tables(seq, rot_dim, first_lane, period, scale):
    half = rot_dim // 2
    inv = ROPE_THETA ** (-jnp.arange(0, rot_dim, 2, dtype=F32) / rot_dim)
    ang = jnp.arange(seq, dtype=F32)[:, None] * inv[None, :]
    cos, sin = jnp.cos(ang), jnp.sin(ang)
    c = jnp.ones((seq, period), F32)
    c = c.at[:, first_lane:first_lane + half].set(cos).at[:, first_lane + half:first_lane + rot_dim].set(cos)
    s_lo = jnp.zeros((seq, period), F32).at[:, first_lane + half:first_lane + rot_dim].set(sin)
    s_hi = jnp.zeros((seq, period), F32).at[:, first_lane:first_lane + half].set(-sin)
    tabs = jnp.stack([c, s_lo, s_hi]) * scale
    return jnp.tile(tabs, (1, 1, LANES // period))


def _pick_tile(n, want):
    t = min(n, want)
    assert n % t == 0, (n, t)
    return t


def kernel(x, p, g_mix, w_in, g_cq, w_uq, g_ckv, w_ukv, g_out_mla, g_out_dil, w_out, g_ffn,
           w_peer_q, peer_keys1, peer_keys2, peer_u, peer_v, g_ple, w_ple_gate, w_ple_proj, g_final):
    batch, seq, _ = x.shape
    depth = p.shape[0]
    T = batch * seq
    mla_w = MLA_HEADS * MLA_V

    tabs = (_rope_lane_tables(seq, MLA_ROPE, MLA_NOPE, LANES, MLA_QK ** -0.5),
            _rope_lane_tables(seq, MLA_ROPE, MLA_NOPE, LANES, 1.0),
            _rope_lane_tables(seq, DIL_ROT, 0, DIL_HEAD_DIM, DIL_HEAD_DIM ** -0.5),
            _rope_lane_tables(seq, DIL_ROT, 0, DIL_HEAD_DIM, 1.0))

    h = x.reshape(T, D_MODEL)
    for i in range(depth):
        wi = w_in[i]
        o1 = MLA_Q_RANK + MLA_KV_RANK
        k_r_cols = jnp.pad(wi[:, o1:o1 + MLA_ROPE], ((0, 0), (MLA_NOPE, LANES - MLA_QK)))
        w_in_p = jnp.concatenate([wi[:, :o1], k_r_cols, wi[:, o1 + MLA_ROPE:]], axis=1).astype(BF16)
        w_uq_p = jnp.pad(w_uq[i].reshape(MLA_Q_RANK, MLA_HEADS, MLA_QK),
                         ((0, 0), (0, 0), (0, LANES - MLA_QK))).reshape(MLA_Q_RANK, -1).astype(BF16)
        w_ukv3 = w_ukv[i].reshape(MLA_KV_RANK, MLA_HEADS, MLA_NOPE + MLA_V)
        w_uk_p = jnp.pad(w_ukv3[:, :, :MLA_NOPE],
                         ((0, 0), (0, 0), (0, LANES - MLA_NOPE))).reshape(MLA_KV_RANK, -1).astype(BF16)
        w_uv = w_ukv3[:, :, MLA_NOPE:].reshape(MLA_KV_RANK, -1).astype(BF16)
        zeros = jnp.zeros((PEER_KEYS, PEER_HALF), F32)
        kdt = jnp.concatenate([jnp.concatenate([peer_keys1[i], zeros], axis=1),
                               jnp.concatenate([zeros, peer_keys2[i]], axis=1)], axis=0).astype(BF16)

        t_proj = _pick_tile(seq, 512)
        qm, km, vm, qd, kd, vd = _in_proj(
            h, g_mix[i][None], w_in_p, g_cq[i][None], w_uq_p, g_ckv[i][None], w_uk_p, w_uv,
            tabs, seq, t_proj)
        tq = _pick_tile(seq, 512)
        o_mla = _mla_attn(qm, km, vm, batch, seq, tq).reshape(T, mla_w)
        o_dil = _dil_attn(qd, kd, vd, batch, seq).reshape(T, DIL_WIDTH)

        h1, xn, ra, rb, rg = _route(
            o_mla, o_dil, h, g_out_mla[i][None], g_out_dil[i][None],
            w_out[i][:mla_w].astype(BF16), w_out[i][mla_w:].astype(BF16), g_ffn[i][None],
            w_peer_q[i].astype(BF16), kdt, _pick_tile(T, 256))
        gates = _gate_build(ra, rb, rg, _pick_tile(T, 128))
        y = _peer(xn, peer_u[i].T.astype(BF16), peer_v[i].astype(BF16), gates,
                  _pick_tile(T, 1024), 4)
        h = _ple_final(h1, y, p[i].reshape(T, PLE_DIM), g_ple[i][None], w_ple_gate[i].astype(BF16),
                       w_ple_proj[i].astype(BF16), g_final[None], _pick_tile(T, 512),
                       last_layer=(i == depth - 1))
    return h.reshape(batch, seq, D_MODEL)
```

```python
import functools

import numpy as np
import jax
import jax.numpy as jnp
from jax import lax
from jax.experimental import pallas as pl
from jax.experimental.pallas import tpu as pltpu

F32 = jnp.float32
BF16 = jnp.bfloat16

EPS = 1e-6
MASKED_SCORE = -1e30
ROPE_THETA = 500000.0

LANES = 128
D_MODEL = 1024
PLE_DIM = 256
MLA_HEADS = 8
MLA_NOPE = 64
MLA_ROPE = 32
MLA_V = 64
MLA_QK = MLA_NOPE + MLA_ROPE
MLA_Q_RANK = 384
MLA_KV_RANK = 256
DIL_HEADS = 8
DIL_HEAD_DIM = 64
DIL_ROT = 16
DIL_WIDTH = DIL_HEADS * DIL_HEAD_DIM
DIL_PATTERNS = ((128, 1), (512, 4), (2048, 16))
PEER_KEYS = 128
PEER_HEADS = 8
PEER_HALF = 64
PEER_TOPK = 16
PEER_SLOTS = PEER_HEADS * PEER_TOPK
PEER_EXPERTS = PEER_KEYS * PEER_KEYS

VMEM_LIMIT = 48 * 1024 * 1024


def _params(semantics):
    return pltpu.CompilerParams(dimension_semantics=semantics, vmem_limit_bytes=VMEM_LIMIT)


def _rms(x, g):
    return x * lax.rsqrt(jnp.mean(x * x, axis=-1, keepdims=True) + EPS) * g


def _rope(x, tab_ref, half):
    return (x * tab_ref[0]
            + pltpu.roll(x, half, 1) * tab_ref[1]
            + pltpu.roll(x, LANES - half, 1) * tab_ref[2])


def _dot(a, b):
    return jnp.dot(a, b, preferred_element_type=F32)


def _dot_nt(a, b):
    return lax.dot_general(a, b, (((1,), (1,)), ((), ())), preferred_element_type=F32)


def _in_proj_kernel(x_ref, gmix_ref, win_ref, gcq_ref, wuq_ref, gckv_ref, wuk_ref, wuv_ref,
                    tmq_ref, tmk_ref, tdq_ref, tdk_ref,
                    qm_ref, km_ref, vm_ref, qd_ref, kd_ref, vd_ref):
    hn = _rms(x_ref[...], gmix_ref[...]).astype(BF16)
    y = _dot(hn, win_ref[...])
    o = 0
    c_q = y[:, o:o + MLA_Q_RANK]; o += MLA_Q_RANK
    c_kv = y[:, o:o + MLA_KV_RANK]; o += MLA_KV_RANK
    k_r = y[:, o:o + LANES]; o += LANES
    q_d = y[:, o:o + DIL_WIDTH]; o += DIL_WIDTH
    k_d = y[:, o:o + DIL_WIDTH]; o += DIL_WIDTH
    v_d = y[:, o:o + DIL_WIDTH]

    q = _dot(_rms(c_q, gcq_ref[...]).astype(BF16), wuq_ref[...])
    ckvn = _rms(c_kv, gckv_ref[...]).astype(BF16)
    k_n = _dot(ckvn, wuk_ref[...])
    vm_ref[...] = _dot(ckvn, wuv_ref[...]).astype(BF16)
    k_rope = _rope(k_r, tmk_ref, MLA_ROPE // 2)
    for h in range(MLA_HEADS):
        sl = slice(h * LANES, (h + 1) * LANES)
        qm_ref[:, sl] = _rope(q[:, sl], tmq_ref, MLA_ROPE // 2).astype(BF16)
        km_ref[:, sl] = (k_n[:, sl] + k_rope).astype(BF16)
    for c in range(DIL_WIDTH // LANES):
        sl = slice(c * LANES, (c + 1) * LANES)
        qd_ref[:, sl] = _rope(q_d[:, sl], tdq_ref, DIL_ROT // 2)
        kd_ref[:, sl] = _rope(k_d[:, sl], tdk_ref, DIL_ROT // 2)
    vd_ref[...] = v_d


def _in_proj(x2, g_mix, w_in_p, g_cq, w_uq_p, g_ckv, w_uk_p, w_uv, tabs, seq, tile):
    T = x2.shape[0]
    n_pos = seq // tile
    full = lambda a: pl.BlockSpec(a.shape, lambda i: (0,) * a.ndim)
    tab = pl.BlockSpec((3, tile, LANES), lambda i: (0, i % n_pos, 0))
    row = lambda w: pl.BlockSpec((tile, w), lambda i: (i, 0))
    outs = [(T, MLA_HEADS * LANES), (T, MLA_HEADS * LANES), (T, MLA_HEADS * MLA_V),
            (T, DIL_WIDTH), (T, DIL_WIDTH), (T, DIL_WIDTH)]
    return pl.pallas_call(
        _in_proj_kernel,
        grid=(T // tile,),
        in_specs=[row(D_MODEL), full(g_mix), full(w_in_p), full(g_cq), full(w_uq_p),
                  full(g_ckv), full(w_uk_p), full(w_uv), tab, tab, tab, tab],
        out_specs=[row(s[1]) for s in outs],
        out_shape=[jax.ShapeDtypeStruct(s, BF16) for s in outs[:3]]
                  + [jax.ShapeDtypeStruct(s, F32) for s in outs[3:]],
        compiler_params=_params(("parallel",)),
        name="in_proj",
    )(x2, g_mix, w_in_p, g_cq, w_uq_p, g_ckv, w_uk_p, w_uv, *tabs)


def _mla_attn_kernel(q_ref, k_ref, v_ref, o_ref):
    v = v_ref[...]
    lane = lax.broadcasted_iota(jnp.int32, (q_ref.shape[0], LANES), 1)
    out = None
    for h in range(2):
        sl = slice(h * LANES, (h + 1) * LANES)
        s = _dot_nt(q_ref[:, sl], k_ref[:, sl])
        e = jnp.exp(s - jnp.max(s, axis=-1, keepdims=True))
        o = _dot(e.astype(BF16), v) / jnp.sum(e, axis=-1, keepdims=True)
        out = o if out is None else jnp.where(lane < MLA_V, out, o)
    o_ref[...] = out


def _mla_attn(qm, km, vm, batch, seq, tq):
    q3 = qm.reshape(batch, seq, MLA_HEADS * LANES)
    k3 = km.reshape(batch, seq, MLA_HEADS * LANES)
    v3 = vm.reshape(batch, seq, MLA_HEADS * MLA_V)
    return pl.pallas_call(
        _mla_attn_kernel,
        grid=(batch, MLA_HEADS // 2, seq // tq),
        in_specs=[pl.BlockSpec((None, tq, 2 * LANES), lambda b, p, i: (b, i, p)),
                  pl.BlockSpec((None, seq, 2 * LANES), lambda b, p, i: (b, 0, p)),
                  pl.BlockSpec((None, seq, LANES), lambda b, p, i: (b, 0, p))],
        out_specs=pl.BlockSpec((None, tq, LANES), lambda b, p, i: (b, i, p)),
        out_shape=jax.ShapeDtypeStruct((batch, seq, MLA_HEADS * MLA_V), F32),
        compiler_params=_params(("parallel", "parallel", "parallel")),
        name="mla_attn",
    )(q3, k3, v3)


DIL_Q_BLOCK = 128
DIL_K_WINDOW = 256

def _dil_attn_kernel(q_ref, k_ref, v_ref, o_ref, acc_ref, m_ref, l_ref):
    seq = q_ref.shape[0]
    lane = lax.broadcasted_iota(jnp.int32, (1, 1, LANES), 2)
    head0 = lane < DIL_HEAD_DIM

    for p, (window, dil) in enumerate(DIL_PATTERNS):
        sub_len = seq // dil
        radius = window // (2 * dil)
        qb = min(DIL_Q_BLOCK, sub_len)
        kw = min(DIL_K_WINDOW, sub_len)
        assert kw == sub_len or kw >= qb + 2 * radius
        blocks = sub_len // qb
        rel = (lax.broadcasted_iota(jnp.int32, (qb, kw), 0)
               - lax.broadcasted_iota(jnp.int32, (qb, kw), 1))

        def rows(first, count, dil=dil):
            return pl.ds(first, count) if dil == 1 else pl.ds(first, count, stride=dil)

        q_rows, k_rows, shifts = [], [], []
        for res in range(dil):
            for jb in range(blocks):
                j0 = jb * qb
                k0 = min(max(j0 - (kw - qb) // 2, 0), sub_len - kw)
                q_rows.append(rows(res + dil * j0, qb))
                k_rows.append(rows(res + dil * k0, kw))
                shifts.append(j0 - k0)
        near_by_shift = {s: jnp.abs(rel + s) <= radius for s in sorted(set(shifts))}
        near = jnp.stack([near_by_shift[s] for s in shifts])
        q = jnp.stack([q_ref[r, :] for r in q_rows])
        k = jnp.stack([k_ref[r, :] for r in k_rows]).astype(BF16)
        v = jnp.stack([v_ref[r, :] for r in k_rows]).astype(BF16)

        acc, m_all, l_all = None, None, None
        for h in range(2):
            qh = jnp.where(head0 if h == 0 else ~head0, q, 0.0).astype(BF16)
            s = lax.dot_general(qh, k, (((2,), (2,)), ((0,), (0,))), preferred_element_type=F32)
            s = jnp.where(near, s, MASKED_SCORE)
            m = jnp.max(s, axis=-1, keepdims=True)
            e = jnp.exp(s - m)
            l = jnp.sum(e, axis=-1, keepdims=True)
            a = lax.dot_general(e.astype(BF16), v, (((2,), (1,)), ((0,), (0,))),
                                preferred_element_type=F32)
            if h == 0:
                acc, m_all, l_all = a, m, l
            else:
                acc = jnp.where(head0, acc, a)
                m_all = jnp.where(head0, m_all, m)
                l_all = jnp.where(head0, l_all, l)
        for n, r in enumerate(q_rows):
            acc_ref[p, r, :] = acc[n]
            m_ref[p, r, :] = m_all[n]
            l_ref[p, r, :] = l_all[n]

    m = m_ref[...]
    w = jnp.exp(m - jnp.max(m, axis=0, keepdims=True))
    o_ref[...] = jnp.sum(w * acc_ref[...], axis=0) / jnp.sum(w * l_ref[...], axis=0)


def _dil_attn(qd, kd, vd, batch, seq):
    spec = pl.BlockSpec((None, seq, LANES), lambda b, p: (b, 0, p))
    stats = pltpu.VMEM((len(DIL_PATTERNS), seq, LANES), F32)
    return pl.pallas_call(
        _dil_attn_kernel,
        grid=(batch, DIL_HEADS // 2),
        in_specs=[spec, spec, spec],
        out_specs=spec,
        out_shape=jax.ShapeDtypeStruct((batch, seq, DIL_WIDTH), F32),
        scratch_shapes=[stats, stats, stats],
        compiler_params=_params(("parallel", "parallel")),
        name="dil_attn",
    )(qd.reshape(batch, seq, DIL_WIDTH), kd.reshape(batch, seq, DIL_WIDTH),
      vd.reshape(batch, seq, DIL_WIDTH))


class _Ranked:
    def __init__(self, r8):
        self.r8 = r8
        self.rows = []
        self.lo = jnp.zeros(r8.shape, F32)
        self.hi = jnp.zeros(r8.shape, F32)

    def push(self, row):
        k = len(self.rows)
        self.rows.append(row)
        if k < 8:
            self.lo = jnp.where(self.r8 == k, row, self.lo)
        else:
            self.hi = jnp.where(self.r8 == k - 8, row, self.hi)


def _top16(vals, key_iota, r8):
    tv, ti = _Ranked(r8), _Ranked(r8)
    for _ in range(PEER_TOPK):
        m = jnp.max(vals, axis=0, keepdims=True)
        idx = jnp.min(jnp.where(vals == m, key_iota, float(PEER_KEYS)), axis=0, keepdims=True)
        tv.push(m)
        ti.push(idx)
        vals = jnp.where(key_iota == idx, -jnp.inf, vals)
    return tv, ti


_POS_SHIFT = float(PEER_EXPERTS)


def _pair_top16(v1, i1, v2, i2, r8):
    ninf = -jnp.inf

    def piece(cand, pos, expert, keep=None):
        if keep is not None:
            cand = jnp.where(keep, cand, ninf)
        return cand, pos * _POS_SHIFT + expert

    pieces = [
        piece(v1.rows[0] + v2.lo, r8, i1.rows[0] * PEER_KEYS + i2.lo),
        piece(v1.rows[0] + v2.hi, r8 + 8.0, i1.rows[0] * PEER_KEYS + i2.hi),
        piece(v1.rows[1] + v2.lo, r8 + 16.0, i1.rows[1] * PEER_KEYS + i2.lo),
        piece(v1.lo + v2.rows[0], r8 * 16.0, i1.lo * PEER_KEYS + i2.rows[0], r8 >= 2),
        piece(v1.hi + v2.rows[0], r8 * 16.0 + 128.0, i1.hi * PEER_KEYS + i2.rows[0]),
        piece(v1.lo + v2.rows[1], r8 * 16.0 + 1.0, i1.lo * PEER_KEYS + i2.rows[1], r8 >= 2),
        piece(v1.rows[2] + v2.lo, r8 + 32.0, i1.rows[2] * PEER_KEYS + i2.lo, (r8 >= 2) & (r8 <= 4)),
        piece(v1.rows[3] + v2.lo, r8 + 48.0, i1.rows[3] * PEER_KEYS + i2.lo, (r8 >= 2) & (r8 <= 3)),
        piece(v1.rows[4] + v2.lo, r8 + 64.0, i1.rows[4] * PEER_KEYS + i2.lo, r8 == 2),
    ]
    cand = jnp.concatenate([p[0] for p in pieces], axis=0)
    key = jnp.concatenate([p[1] for p in pieces], axis=0)
    top, sel = _Ranked(r8), _Ranked(r8)
    for _ in range(PEER_TOPK):
        m = jnp.max(cand, axis=0, keepdims=True)
        kk = jnp.min(jnp.where(cand == m, key, 3.0e38), axis=0, keepdims=True)
        top.push(m)
        sel.push(kk)
        cand = jnp.where(key == kk, ninf, cand)
    return top, sel


def _route_kernel(om_ref, od_ref, x_ref, gom_ref, god_ref, wom_ref, wod_ref, gffn_ref,
                  wq_ref, kdt_ref,
                  h_ref, xn_ref, ra_ref, rb_ref, rg_ref,
                  sc_ref, sa_ref, sb_ref, sg_ref):
    tile = x_ref.shape[0]
    nm = _rms(om_ref[...], gom_ref[...]).astype(BF16)
    nd = _rms(od_ref[...], god_ref[...]).astype(BF16)
    h = x_ref[...] + _dot(nm, wom_ref[...]) + _dot(nd, wod_ref[...])
    h_ref[...] = h
    xn = _rms(h, gffn_ref[...]).astype(BF16)
    xn_ref[...] = xn
    q = _dot(xn, wq_ref[...])
    kdt = kdt_ref[...]
    for hd in range(PEER_HEADS):
        qh = q[:, hd * LANES:(hd + 1) * LANES].astype(BF16)
        sc_ref[hd] = _dot_nt(kdt, qh)

    key_iota = lax.broadcasted_iota(jnp.int32, (PEER_KEYS, tile), 0).astype(F32)
    r8 = lax.broadcasted_iota(jnp.int32, (8, tile), 0).astype(F32)

    def head(hd, carry):
        v1, i1 = _top16(sc_ref[hd, :PEER_KEYS, :], key_iota, r8)
        v2, i2 = _top16(sc_ref[hd, PEER_KEYS:, :], key_iota, r8)
        top, sel = _pair_top16(v1, i1, v2, i2, r8)
        e_lo = jnp.exp(top.lo - top.rows[0])
        e_hi = jnp.exp(top.hi - top.rows[0])
        inv = 1.0 / (jnp.sum(e_lo, axis=0, keepdims=True) + jnp.sum(e_hi, axis=0, keepdims=True))
        for half, (s, e) in enumerate(((sel.lo, e_lo), (sel.hi, e_hi))):
            rows = slice(half * 8, half * 8 + 8)
            expert = s - jnp.floor(s * (1.0 / _POS_SHIFT)) * _POS_SHIFT
            a = jnp.floor(expert * (1.0 / PEER_KEYS))
            sa_ref[hd, rows, :] = a
            sb_ref[hd, rows, :] = expert - a * PEER_KEYS
            sg_ref[hd, rows, :] = e * inv
        return carry

    lax.fori_loop(0, PEER_HEADS, head, 0)
    a_all = sa_ref[...].reshape(PEER_SLOTS, tile)
    b_all = sb_ref[...].reshape(PEER_SLOTS, tile)
    g_all = sg_ref[...].reshape(PEER_SLOTS, tile)
    for c in range(tile // LANES):
        cols = slice(c * LANES, (c + 1) * LANES)
        ra_ref[cols, :] = a_all[:, cols].T
        rb_ref[cols, :] = b_all[:, cols].T
        rg_ref[cols, :] = g_all[:, cols].T


def _route(o_mla, o_dil, x2, g_om, g_od, w_om, w_od, g_ffn, w_q, kdt, tile):
    T = x2.shape[0]
    full = lambda a: pl.BlockSpec(a.shape, lambda i: (0,) * a.ndim)
    row = lambda w: pl.BlockSpec((tile, w), lambda i: (i, 0))
    return pl.pallas_call(
        _route_kernel,
        grid=(T // tile,),
        in_specs=[row(o_mla.shape[1]), row(o_dil.shape[1]), row(D_MODEL), full(g_om), full(g_od),
                  full(w_om), full(w_od), full(g_ffn), full(w_q), full(kdt)],
        out_specs=[row(D_MODEL), row(D_MODEL), row(PEER_SLOTS), row(PEER_SLOTS), row(PEER_SLOTS)],
        out_shape=[jax.ShapeDtypeStruct((T, D_MODEL), F32),
                   jax.ShapeDtypeStruct((T, D_MODEL), BF16),
                   jax.ShapeDtypeStruct((T, PEER_SLOTS), F32),
                   jax.ShapeDtypeStruct((T, PEER_SLOTS), F32),
                   jax.ShapeDtypeStruct((T, PEER_SLOTS), F32)],
        scratch_shapes=[pltpu.VMEM((PEER_HEADS, 2 * PEER_KEYS, tile), F32),
                        pltpu.VMEM((PEER_HEADS, PEER_TOPK, tile), F32),
                        pltpu.VMEM((PEER_HEADS, PEER_TOPK, tile), F32),
                        pltpu.VMEM((PEER_HEADS, PEER_TOPK, tile), F32)],
        compiler_params=_params(("parallel",)),
        name="route",
    )(o_mla, o_dil, x2, g_om, g_od, w_om, w_od, g_ffn, w_q, kdt)


GATE_PITCH = PEER_KEYS + 8
GATE_UNROLL = 8


def _gate_build_kernel(ra_ref, rb_ref, rg_ref, g_ref, stage_ref):
    key_iota = lax.broadcasted_iota(jnp.int32, (PEER_KEYS, PEER_SLOTS), 0).astype(F32)
    tile = ra_ref.shape[0]

    def token(t, carry):
        a = ra_ref[pl.ds(t, 1), :]
        b = rb_ref[pl.ds(t, 1), :]
        g = rg_ref[pl.ds(t, 1), :]
        one_a = jnp.where(key_iota == a, 1.0, 0.0).astype(BF16)
        g_at_b = jnp.where(key_iota == b, g, 0.0).astype(BF16)
        row0 = pl.multiple_of(t * GATE_PITCH, 8)
        stage_ref[pl.ds(row0, PEER_KEYS), :] = _dot_nt(one_a, g_at_b)
        return carry

    lax.fori_loop(0, tile, token, 0, unroll=GATE_UNROLL)
    for a in range(PEER_KEYS):
        g_ref[a] = stage_ref[pl.ds(a, tile, stride=GATE_PITCH), :].astype(BF16)


def _gate_build(ra, rb, rg, tile):
    T = ra.shape[0]
    row = pl.BlockSpec((tile, PEER_SLOTS), lambda i: (i, 0))
    return pl.pallas_call(
        _gate_build_kernel,
        grid=(T // tile,),
        in_specs=[row, row, row],
        out_specs=pl.BlockSpec((PEER_KEYS, tile, PEER_KEYS), lambda i: (0, i, 0)),
        out_shape=jax.ShapeDtypeStruct((PEER_KEYS, T, PEER_KEYS), BF16),
        scratch_shapes=[pltpu.VMEM((tile * GATE_PITCH, PEER_KEYS), F32)],
        compiler_params=_params(("parallel",)),
        name="gate_build",
    )(ra, rb, rg)


def _peer_kernel(xn_ref, ut_ref, v_ref, g_ref, o_ref, *, a_per_step):
    pre = _dot(xn_ref[...], ut_ref[...])
    act = 0.5 * pre * (1.0 + lax.erf(pre * float(1.0 / np.sqrt(2.0))))
    gate = jnp.concatenate([g_ref[a] for a in range(a_per_step)], axis=1)
    y = _dot((act * gate.astype(F32)).astype(BF16), v_ref[...])

    @pl.when(pl.program_id(1) == 0)
    def _():
        o_ref[...] = y

    @pl.when(pl.program_id(1) != 0)
    def _():
        o_ref[...] += y


def _peer(xn, u_t, v_b, gates, tile, a_per_step):
    T = xn.shape[0]
    et = a_per_step * PEER_KEYS
    return pl.pallas_call(
        functools.partial(_peer_kernel, a_per_step=a_per_step),
        grid=(T // tile, PEER_EXPERTS // et),
        in_specs=[pl.BlockSpec((tile, D_MODEL), lambda i, j: (i, 0)),
                  pl.BlockSpec((D_MODEL, et), lambda i, j: (0, j)),
                  pl.BlockSpec((et, D_MODEL), lambda i, j: (j, 0)),
                  pl.BlockSpec((a_per_step, tile, PEER_KEYS), lambda i, j: (j, i, 0))],
        out_specs=pl.BlockSpec((tile, D_MODEL), lambda i, j: (i, 0)),
        out_shape=jax.ShapeDtypeStruct((T, D_MODEL), F32),
        compiler_params=_params(("parallel", "arbitrary")),
        name="peer",
    )(xn, u_t, v_b, gates)


def _ple_final_kernel(h_ref, y_ref, p_ref, gple_ref, wg_ref, wp_ref, gfin_ref, o_ref, *, last_layer):
    h = h_ref[...] + y_ref[...]
    gate = jax.nn.sigmoid(_dot(_rms(h, gple_ref[...]).astype(BF16), wg_ref[...]))
    h = h + gate * _dot(p_ref[...].astype(BF16), wp_ref[...])
    o_ref[...] = _rms(h, gfin_ref[...]) if last_layer else h


def _ple_final(h1, y, p2, g_ple, w_gate, w_proj, g_final, tile, last_layer):
    T = h1.shape[0]
    full = lambda a: pl.BlockSpec(a.shape, lambda i: (0,) * a.ndim)
    row = lambda w: pl.BlockSpec((tile, w), lambda i: (i, 0))
    return pl.pallas_call(
        functools.partial(_ple_final_kernel, last_layer=last_layer),
        grid=(T // tile,),
        in_specs=[row(D_MODEL), row(D_MODEL), row(PLE_DIM), full(g_ple), full(w_gate),
                  full(w_proj), full(g_final)],
        out_specs=row(D_MODEL),
        out_shape=jax.ShapeDtypeStruct((T, D_MODEL), F32),
        compiler_params=_params(("parallel",)),
        name="ple_final",
    )(h1, y, p2, g_ple, w_gate, w_proj, g_final)


def _rope_lane_tables(seq, rot_dim, first_lane, period, scale):
    half = rot_dim // 2
    inv = ROPE_THETA ** (-jnp.arange(0, rot_dim, 2, dtype=F32) / rot_dim)
    ang = jnp.arange(seq, dtype=F32)[:, None] * inv[None, :]
    cos, sin = jnp.cos(ang), jnp.sin(ang)
    c = jnp.ones((seq, period), F32)
    c = c.at[:, first_lane:first_lane + half].set(cos).at[:, first_lane + half:first_lane + rot_dim].set(cos)
    s_lo = jnp.zeros((seq, period), F32).at[:, first_lane + half:first_lane + rot_dim].set(sin)
    s_hi = jnp.zeros((seq, period), F32).at[:, first_lane:first_lane + half].set(-sin)
    tabs = jnp.stack([c, s_lo, s_hi]) * scale
    return jnp.tile(tabs, (1, 1, LANES // period))


def _pick_tile(n, want):
    t = min(n, want)
    assert n % t == 0, (n, t)
    return t


def kernel(x, p, g_mix, w_in, g_cq, w_uq, g_ckv, w_ukv, g_out_mla, g_out_dil, w_out, g_ffn,
           w_peer_q, peer_keys1, peer_keys2, peer_u, peer_v, g_ple, w_ple_gate, w_ple_proj, g_final):
    batch, seq, _ = x.shape
    depth = p.shape[0]
    T = batch * seq
    mla_w = MLA_HEADS * MLA_V

    tabs = (_rope_lane_tables(seq, MLA_ROPE, MLA_NOPE, LANES, MLA_QK ** -0.5),
            _rope_lane_tables(seq, MLA_ROPE, MLA_NOPE, LANES, 1.0),
            _rope_lane_tables(seq, DIL_ROT, 0, DIL_HEAD_DIM, DIL_HEAD_DIM ** -0.5),
            _rope_lane_tables(seq, DIL_ROT, 0, DIL_HEAD_DIM, 1.0))

    h = x.reshape(T, D_MODEL)
    for i in range(depth):
        wi = w_in[i]
        o1 = MLA_Q_RANK + MLA_KV_RANK
        k_r_cols = jnp.pad(wi[:, o1:o1 + MLA_ROPE], ((0, 0), (MLA_NOPE, LANES - MLA_QK)))
        w_in_p = jnp.concatenate([wi[:, :o1], k_r_cols, wi[:, o1 + MLA_ROPE:]], axis=1).astype(BF16)
        w_uq_p = jnp.pad(w_uq[i].reshape(MLA_Q_RANK, MLA_HEADS, MLA_QK),
                         ((0, 0), (0, 0), (0, LANES - MLA_QK))).reshape(MLA_Q_RANK, -1).astype(BF16)
        w_ukv3 = w_ukv[i].reshape(MLA_KV_RANK, MLA_HEADS, MLA_NOPE + MLA_V)
        w_uk_p = jnp.pad(w_ukv3[:, :, :MLA_NOPE],
                         ((0, 0), (0, 0), (0, LANES - MLA_NOPE))).reshape(MLA_KV_RANK, -1).astype(BF16)
        w_uv = w_ukv3[:, :, MLA_NOPE:].reshape(MLA_KV_RANK, -1).astype(BF16)
        zeros = jnp.zeros((PEER_KEYS, PEER_HALF), F32)
        kdt = jnp.concatenate([jnp.concatenate([peer_keys1[i], zeros], axis=1),
                               jnp.concatenate([zeros, peer_keys2[i]], axis=1)], axis=0).astype(BF16)

        t_proj = _pick_tile(seq, 512)
        qm, km, vm, qd, kd, vd = _in_proj(
            h, g_mix[i][None], w_in_p, g_cq[i][None], w_uq_p, g_ckv[i][None], w_uk_p, w_uv,
            tabs, seq, t_proj)
        tq = _pick_tile(seq, 512)
        o_mla = _mla_attn(qm, km, vm, batch, seq, tq).reshape(T, mla_w)
        o_dil = _dil_attn(qd, kd, vd, batch, seq).reshape(T, DIL_WIDTH)

        h1, xn, ra, rb, rg = _route(
            o_mla, o_dil, h, g_out_mla[i][None], g_out_dil[i][None],
            w_out[i][:mla_w].astype(BF16), w_out[i][mla_w:].astype(BF16), g_ffn[i][None],
            w_peer_q[i].astype(BF16), kdt, _pick_tile(T, 256))
        gates = _gate_build(ra, rb, rg, _pick_tile(T, 128))
        y = _peer(xn, peer_u[i].T.astype(BF16), peer_v[i].astype(BF16), gates,
                  _pick_tile(T, 1024), 8)
        h = _ple_final(h1, y, p[i].reshape(T, PLE_DIM), g_ple[i][None], w_ple_gate[i].astype(BF16),
                       w_ple_proj[i].astype(BF16), g_final[None], _pick_tile(T, 512),
                       last_layer=(i == depth - 1))
    return h.reshape(batch, seq, D_MODEL)
```

```python
import functools

import numpy as np
import jax
import jax.numpy as jnp
from jax import lax
from jax.experimental import pallas as pl
from jax.experimental.pallas import tpu as pltpu

F32 = jnp.float32
BF16 = jnp.bfloat16

EPS = 1e-6
MASKED_SCORE = -1e30
ROPE_THETA = 500000.0

LANES = 128
D_MODEL = 1024
PLE_DIM = 256
MLA_HEADS = 8
MLA_NOPE = 64
MLA_ROPE = 32
MLA_V = 64
MLA_QK = MLA_NOPE + MLA_ROPE
MLA_Q_RANK = 384
MLA_KV_RANK = 256
DIL_HEADS = 8
DIL_HEAD_DIM = 64
DIL_ROT = 16
DIL_WIDTH = DIL_HEADS * DIL_HEAD_DIM
DIL_PATTERNS = ((128, 1), (512, 4), (2048, 16))
PEER_KEYS = 128
PEER_HEADS = 8
PEER_HALF = 64
PEER_TOPK = 16
PEER_SLOTS = PEER_HEADS * PEER_TOPK
PEER_EXPERTS = PEER_KEYS * PEER_KEYS

VMEM_LIMIT = 48 * 1024 * 1024


def _params(semantics):
    return pltpu.CompilerParams(dimension_semantics=semantics, vmem_limit_bytes=VMEM_LIMIT)


def _rms(x, g):
    return x * lax.rsqrt(jnp.mean(x * x, axis=-1, keepdims=True) + EPS) * g


def _rope(x, tab_ref, half):
    return (x * tab_ref[0]
            + pltpu.roll(x, half, 1) * tab_ref[1]
            + pltpu.roll(x, LANES - half, 1) * tab_ref[2])


def _dot(a, b):
    return jnp.dot(a, b, preferred_element_type=F32)


def _dot_nt(a, b):
    return lax.dot_general(a, b, (((1,), (1,)), ((), ())), preferred_element_type=F32)


def _in_proj_kernel(x_ref, gmix_ref, win_ref, gcq_ref, wuq_ref, gckv_ref, wuk_ref, wuv_ref,
                    tmq_ref, tmk_ref, tdq_ref, tdk_ref,
                    qm_ref, km_ref, vm_ref, qd_ref, kd_ref, vd_ref):
    hn = _rms(x_ref[...], gmix_ref[...]).astype(BF16)
    y = _dot(hn, win_ref[...])
    o = 0
    c_q = y[:, o:o + MLA_Q_RANK]; o += MLA_Q_RANK
    c_kv = y[:, o:o + MLA_KV_RANK]; o += MLA_KV_RANK
    k_r = y[:, o:o + LANES]; o += LANES
    q_d = y[:, o:o + DIL_WIDTH]; o += DIL_WIDTH
    k_d = y[:, o:o + DIL_WIDTH]; o += DIL_WIDTH
    v_d = y[:, o:o + DIL_WIDTH]

    q = _dot(_rms(c_q, gcq_ref[...]).astype(BF16), wuq_ref[...])
    ckvn = _rms(c_kv, gckv_ref[...]).astype(BF16)
    k_n = _dot(ckvn, wuk_ref[...])
    vm_ref[...] = _dot(ckvn, wuv_ref[...]).astype(BF16)
    k_rope = _rope(k_r, tmk_ref, MLA_ROPE // 2)
    for h in range(MLA_HEADS):
        sl = slice(h * LANES, (h + 1) * LANES)
        qm_ref[:, sl] = _rope(q[:, sl], tmq_ref, MLA_ROPE // 2).astype(BF16)
        km_ref[:, sl] = (k_n[:, sl] + k_rope).astype(BF16)
    for c in range(DIL_WIDTH // LANES):
        sl = slice(c * LANES, (c + 1) * LANES)
        qd_ref[:, sl] = _rope(q_d[:, sl], tdq_ref, DIL_ROT // 2)
        kd_ref[:, sl] = _rope(k_d[:, sl], tdk_ref, DIL_ROT // 2)
    vd_ref[...] = v_d


def _in_proj(x2, g_mix, w_in_p, g_cq, w_uq_p, g_ckv, w_uk_p, w_uv, tabs, seq, tile):
    T = x2.shape[0]
    n_pos = seq // tile
    full = lambda a: pl.BlockSpec(a.shape, lambda i: (0,) * a.ndim)
    tab = pl.BlockSpec((3, tile, LANES), lambda i: (0, i % n_pos, 0))
    row = lambda w: pl.BlockSpec((tile, w), lambda i: (i, 0))
    outs = [(T, MLA_HEADS * LANES), (T, MLA_HEADS * LANES), (T, MLA_HEADS * MLA_V),
            (T, DIL_WIDTH), (T, DIL_WIDTH), (T, DIL_WIDTH)]
    return pl.pallas_call(
        _in_proj_kernel,
        grid=(T // tile,),
        in_specs=[row(D_MODEL), full(g_mix), full(w_in_p), full(g_cq), full(w_uq_p),
                  full(g_ckv), full(w_uk_p), full(w_uv), tab, tab, tab, tab],
        out_specs=[row(s[1]) for s in outs],
        out_shape=[jax.ShapeDtypeStruct(s, BF16) for s in outs[:3]]
                  + [jax.ShapeDtypeStruct(s, F32) for s in outs[3:]],
        compiler_params=_params(("parallel",)),
        name="in_proj",
    )(x2, g_mix, w_in_p, g_cq, w_uq_p, g_ckv, w_uk_p, w_uv, *tabs)


def _mla_attn_kernel(q_ref, k_ref, v_ref, o_ref):
    v = v_ref[...]
    lane = lax.broadcasted_iota(jnp.int32, (q_ref.shape[0], LANES), 1)
    out = None
    for h in range(2):
        sl = slice(h * LANES, (h + 1) * LANES)
        s = _dot_nt(q_ref[:, sl], k_ref[:, sl])
        e = jnp.exp(s - jnp.max(s, axis=-1, keepdims=True))
        o = _dot(e.astype(BF16), v) / jnp.sum(e, axis=-1, keepdims=True)
        out = o if out is None else jnp.where(lane < MLA_V, out, o)
    o_ref[...] = out


def _mla_attn(qm, km, vm, batch, seq, tq):
    q3 = qm.reshape(batch, seq, MLA_HEADS * LANES)
    k3 = km.reshape(batch, seq, MLA_HEADS * LANES)
    v3 = vm.reshape(batch, seq, MLA_HEADS * MLA_V)
    return pl.pallas_call(
        _mla_attn_kernel,
        grid=(batch, MLA_HEADS // 2, seq // tq),
        in_specs=[pl.BlockSpec((None, tq, 2 * LANES), lambda b, p, i: (b, i, p)),
                  pl.BlockSpec((None, seq, 2 * LANES), lambda b, p, i: (b, 0, p)),
                  pl.BlockSpec((None, seq, LANES), lambda b, p, i: (b, 0, p))],
        out_specs=pl.BlockSpec((None, tq, LANES), lambda b, p, i: (b, i, p)),
        out_shape=jax.ShapeDtypeStruct((batch, seq, MLA_HEADS * MLA_V), F32),
        compiler_params=_params(("parallel", "parallel", "parallel")),
        name="mla_attn",
    )(q3, k3, v3)


DIL_Q_BLOCK = 128
DIL_K_WINDOW = 256

def _dil_attn_kernel(q_ref, k_ref, v_ref, o_ref, acc_ref, m_ref, l_ref):
    seq = q_ref.shape[0]
    lane = lax.broadcasted_iota(jnp.int32, (1, 1, LANES), 2)
    head0 = lane < DIL_HEAD_DIM

    for p, (window, dil) in enumerate(DIL_PATTERNS):
        sub_len = seq // dil
        radius = window // (2 * dil)
        qb = min(DIL_Q_BLOCK, sub_len)
        kw = min(DIL_K_WINDOW, sub_len)
        assert kw == sub_len or kw >= qb + 2 * radius
        blocks = sub_len // qb
        rel = (lax.broadcasted_iota(jnp.int32, (qb, kw), 0)
               - lax.broadcasted_iota(jnp.int32, (qb, kw), 1))

        def rows(first, count, dil=dil):
            return pl.ds(first, count) if dil == 1 else pl.ds(first, count, stride=dil)

        q_rows, k_rows, shifts = [], [], []
        for res in range(dil):
            for jb in range(blocks):
                j0 = jb * qb
                k0 = min(max(j0 - (kw - qb) // 2, 0), sub_len - kw)
                q_rows.append(rows(res + dil * j0, qb))
                k_rows.append(rows(res + dil * k0, kw))
                shifts.append(j0 - k0)
        near_by_shift = {s: jnp.abs(rel + s) <= radius for s in sorted(set(shifts))}
        near = jnp.stack([near_by_shift[s] for s in shifts])
        q = jnp.stack([q_ref[r, :] for r in q_rows])
        k = jnp.stack([k_ref[r, :] for r in k_rows]).astype(BF16)
        v = jnp.stack([v_ref[r, :] for r in k_rows]).astype(BF16)

        acc, m_all, l_all = None, None, None
        for h in range(2):
            qh = jnp.where(head0 if h == 0 else ~head0, q, 0.0).astype(BF16)
            s = lax.dot_general(qh, k, (((2,), (2,)), ((0,), (0,))), preferred_element_type=F32)
            s = jnp.where(near, s, MASKED_SCORE)
            m = jnp.max(s, axis=-1, keepdims=True)
            e = jnp.exp(s - m)
            l = jnp.sum(e, axis=-1, keepdims=True)
            a = lax.dot_general(e.astype(BF16), v, (((2,), (1,)), ((0,), (0,))),
                                preferred_element_type=F32)
            if h == 0:
                acc, m_all, l_all = a, m, l
            else:
                acc = jnp.where(head0, acc, a)
                m_all = jnp.where(head0, m_all, m)
                l_all = jnp.where(head0, l_all, l)
        for n, r in enumerate(q_rows):
            acc_ref[p, r, :] = acc[n]
            m_ref[p, r, :] = m_all[n]
            l_ref[p, r, :] = l_all[n]

    m = m_ref[...]
    w = jnp.exp(m - jnp.max(m, axis=0, keepdims=True))
    o_ref[...] = jnp.sum(w * acc_ref[...], axis=0) / jnp.sum(w * l_ref[...], axis=0)


def _dil_attn(qd, kd, vd, batch, seq):
    spec = pl.BlockSpec((None, seq, LANES), lambda b, p: (b, 0, p))
    stats = pltpu.VMEM((len(DIL_PATTERNS), seq, LANES), F32)
    return pl.pallas_call(
        _dil_attn_kernel,
        grid=(batch, DIL_HEADS // 2),
        in_specs=[spec, spec, spec],
        out_specs=spec,
        out_shape=jax.ShapeDtypeStruct((batch, seq, DIL_WIDTH), F32),
        scratch_shapes=[stats, stats, stats],
        compiler_params=_params(("parallel", "parallel")),
        name="dil_attn",
    )(qd.reshape(batch, seq, DIL_WIDTH), kd.reshape(batch, seq, DIL_WIDTH),
      vd.reshape(batch, seq, DIL_WIDTH))


class _Ranked:
    def __init__(self, r8):
        self.r8 = r8
        self.rows = []
        self.lo = jnp.zeros(r8.shape, F32)
        self.hi = jnp.zeros(r8.shape, F32)

    def push(self, row):
        k = len(self.rows)
        self.rows.append(row)
        if k < 8:
            self.lo = jnp.where(self.r8 == k, row, self.lo)
        else:
            self.hi = jnp.where(self.r8 == k - 8, row, self.hi)


KEY_STACK = 4


def _top16(blocks, r8):
    groups = []
    for g in range(len(blocks) // KEY_STACK):
        vs = [blocks[g * KEY_STACK + d] for d in range(KEY_STACK)]
        ids = [r8 + float(8 * (g * KEY_STACK + d)) for d in range(KEY_STACK)]
        for a in (0, 2, 1, 0, 2, 1):
            swap = vs[a + 1] > vs[a]
            vs[a], vs[a + 1] = jnp.where(swap, vs[a + 1], vs[a]), jnp.where(swap, vs[a], vs[a + 1])
            ids[a], ids[a + 1] = jnp.where(swap, ids[a + 1], ids[a]), jnp.where(swap, ids[a], ids[a + 1])
        groups.append((vs, ids))

    tv, ti = _Ranked(r8), _Ranked(r8)
    for _ in range(PEER_TOPK):
        m = jnp.max(functools.reduce(jnp.maximum, [vs[0] for vs, _ in groups]),
                    axis=0, keepdims=True)
        low = functools.reduce(jnp.minimum, [jnp.where(vs[0] == m, ids[0], float(PEER_KEYS))
                                            for vs, ids in groups])
        idx = jnp.min(low, axis=0, keepdims=True)
        tv.push(m)
        ti.push(idx)
        for vs, ids in groups:
            hit = ids[0] == idx
            for d in range(KEY_STACK - 1):
                vs[d] = jnp.where(hit, vs[d + 1], vs[d])
                ids[d] = jnp.where(hit, ids[d + 1], ids[d])
            vs[KEY_STACK - 1] = jnp.where(hit, -jnp.inf, vs[KEY_STACK - 1])
    return tv, ti


_POS_SHIFT = float(PEER_EXPERTS)


def _pair_top16(v1, i1, v2, i2, r8):
    ninf = -jnp.inf

    def piece(cand, pos, expert, keep=None):
        if keep is not None:
            cand = jnp.where(keep, cand, ninf)
        return cand, pos * _POS_SHIFT + expert

    pieces = [
        piece(v1.rows[0] + v2.lo, r8, i1.rows[0] * PEER_KEYS + i2.lo),
        piece(v1.rows[0] + v2.hi, r8 + 8.0, i1.rows[0] * PEER_KEYS + i2.hi),
        piece(v1.rows[1] + v2.lo, r8 + 16.0, i1.rows[1] * PEER_KEYS + i2.lo),
        piece(v1.lo + v2.rows[0], r8 * 16.0, i1.lo * PEER_KEYS + i2.rows[0], r8 >= 2),
        piece(v1.hi + v2.rows[0], r8 * 16.0 + 128.0, i1.hi * PEER_KEYS + i2.rows[0]),
        piece(v1.lo + v2.rows[1], r8 * 16.0 + 1.0, i1.lo * PEER_KEYS + i2.rows[1], r8 >= 2),
        piece(v1.rows[2] + v2.lo, r8 + 32.0, i1.rows[2] * PEER_KEYS + i2.lo, (r8 >= 2) & (r8 <= 4)),
        piece(v1.rows[3] + v2.lo, r8 + 48.0, i1.rows[3] * PEER_KEYS + i2.lo, (r8 >= 2) & (r8 <= 3)),
        piece(v1.rows[4] + v2.lo, r8 + 64.0, i1.rows[4] * PEER_KEYS + i2.lo, r8 == 2),
    ]
    cand = jnp.concatenate([p[0] for p in pieces], axis=0)
    key = jnp.concatenate([p[1] for p in pieces], axis=0)
    top, sel = _Ranked(r8), _Ranked(r8)
    for _ in range(PEER_TOPK):
        m = jnp.max(cand, axis=0, keepdims=True)
        kk = jnp.min(jnp.where(cand == m, key, 3.0e38), axis=0, keepdims=True)
        top.push(m)
        sel.push(kk)
        cand = jnp.where(key == kk, ninf, cand)
    return top, sel


def _route_kernel(om_ref, od_ref, x_ref, gom_ref, god_ref, wom_ref, wod_ref, gffn_ref,
                  wq_ref, kdt_ref,
                  h_ref, xn_ref, ra_ref, rb_ref, rg_ref,
                  sc_ref, sa_ref, sb_ref, sg_ref):
    tile = x_ref.shape[0]
    nm = _rms(om_ref[...], gom_ref[...]).astype(BF16)
    nd = _rms(od_ref[...], god_ref[...]).astype(BF16)
    h = x_ref[...] + _dot(nm, wom_ref[...]) + _dot(nd, wod_ref[...])
    h_ref[...] = h
    xn = _rms(h, gffn_ref[...]).astype(BF16)
    xn_ref[...] = xn
    q = _dot(xn, wq_ref[...])
    kdt = kdt_ref[...]
    for hd in range(PEER_HEADS):
        qh = q[:, hd * LANES:(hd + 1) * LANES].astype(BF16)
        sc_ref[hd] = _dot_nt(kdt, qh)

    r8 =lax.broadcasted_iota(jnp.int32, (8, tile), 0).astype(F32)

    def head(hd, carry):
        def key_blocks(first):
            return [sc_ref[hd, first + 8 * d:first + 8 * d + 8, :] for d in range(PEER_KEYS // 8)]

        v1, i1 = _top16(key_blocks(0), r8)
        v2, i2 = _top16(key_blocks(PEER_KEYS), r8)
        top, sel = _pair_top16(v1, i1, v2, i2, r8)
        e_lo = jnp.exp(top.lo - top.rows[0])
        e_hi = jnp.exp(top.hi - top.rows[0])
        inv = 1.0 / (jnp.sum(e_lo, axis=0, keepdims=True) + jnp.sum(e_hi, axis=0, keepdims=True))
        for half, (s, e) in enumerate(((sel.lo, e_lo), (sel.hi, e_hi))):
            rows = slice(half * 8, half * 8 + 8)
            expert = s - jnp.floor(s * (1.0 / _POS_SHIFT)) * _POS_SHIFT
            a = jnp.floor(expert * (1.0 / PEER_KEYS))
            sa_ref[hd, rows, :] = a
            sb_ref[hd, rows, :] = expert - a * PEER_KEYS
            sg_ref[hd, rows, :] = e * inv
        return carry

    lax.fori_loop(0, PEER_HEADS, head, 0)
    a_all = sa_ref[...].reshape(PEER_SLOTS, tile)
    b_all = sb_ref[...].reshape(PEER_SLOTS, tile)
    g_all = sg_ref[...].reshape(PEER_SLOTS, tile)
    for c in range(tile // LANES):
        cols = slice(c * LANES, (c + 1) * LANES)
        ra_ref[cols, :] = a_all[:, cols].T
        rb_ref[cols, :] = b_all[:, cols].T
        rg_ref[cols, :] = g_all[:, cols].T


def _route(o_mla, o_dil, x2, g_om, g_od, w_om, w_od, g_ffn, w_q, kdt, tile):
    T = x2.shape[0]
    full = lambda a: pl.BlockSpec(a.shape, lambda i: (0,) * a.ndim)
    row = lambda w: pl.BlockSpec((tile, w), lambda i: (i, 0))
    return pl.pallas_call(
        _route_kernel,
        grid=(T // tile,),
        in_specs=[row(o_mla.shape[1]), row(o_dil.shape[1]), row(D_MODEL), full(g_om), full(g_od),
                  full(w_om), full(w_od), full(g_ffn), full(w_q), full(kdt)],
        out_specs=[row(D_MODEL), row(D_MODEL), row(PEER_SLOTS), row(PEER_SLOTS), row(PEER_SLOTS)],
        out_shape=[jax.ShapeDtypeStruct((T, D_MODEL), F32),
                   jax.ShapeDtypeStruct((T, D_MODEL), BF16),
                   jax.ShapeDtypeStruct((T, PEER_SLOTS), F32),
                   jax.ShapeDtypeStruct((T, PEER_SLOTS), F32),
                   jax.ShapeDtypeStruct((T, PEER_SLOTS), F32)],
        scratch_shapes=[pltpu.VMEM((PEER_HEADS, 2 * PEER_KEYS, tile), F32),
                        pltpu.VMEM((PEER_HEADS, PEER_TOPK, tile), F32),
                        pltpu.VMEM((PEER_HEADS, PEER_TOPK, tile), F32),
                        pltpu.VMEM((PEER_HEADS, PEER_TOPK, tile), F32)],
        compiler_params=_params(("parallel",)),
        name="route",
    )(o_mla, o_dil, x2, g_om, g_od, w_om, w_od, g_ffn, w_q, kdt)


GATE_PITCH = PEER_KEYS + 8
GATE_UNROLL = 128


def _gate_build_kernel(ra_ref, rb_ref, rg_ref, g_ref, stage_ref):
    key_iota = lax.broadcasted_iota(jnp.int32, (PEER_KEYS, PEER_SLOTS), 0).astype(F32)
    tile = ra_ref.shape[0]

    def token(t, carry):
        a = ra_ref[pl.ds(t, 1), :]
        b = rb_ref[pl.ds(t, 1), :]
        g = rg_ref[pl.ds(t, 1), :]
        one_a = jnp.where(key_iota == a, 1.0, 0.0).astype(BF16)
        g_at_b = jnp.where(key_iota == b, g, 0.0).astype(BF16)
        row0 = pl.multiple_of(t * GATE_PITCH, 8)
        stage_ref[pl.ds(row0, PEER_KEYS), :] = _dot_nt(one_a, g_at_b)
        return carry

    lax.fori_loop(0, tile, token, 0, unroll=GATE_UNROLL)
    for a in range(PEER_KEYS):
        g_ref[a] = stage_ref[pl.ds(a, tile, stride=GATE_PITCH), :].astype(BF16)


def _gate_build(ra, rb, rg, tile):
    T = ra.shape[0]
    row = pl.BlockSpec((tile, PEER_SLOTS), lambda i: (i, 0))
    return pl.pallas_call(
        _gate_build_kernel,
        grid=(T // tile,),
        in_specs=[row, row, row],
        out_specs=pl.BlockSpec((PEER_KEYS, tile, PEER_KEYS), lambda i: (0, i, 0)),
        out_shape=jax.ShapeDtypeStruct((PEER_KEYS, T, PEER_KEYS), BF16),
        scratch_shapes=[pltpu.VMEM((tile * GATE_PITCH, PEER_KEYS), F32)],
        compiler_params=_params(("parallel",)),
        name="gate_build",
    )(ra, rb, rg)


def _peer_kernel(xn_ref, u_ref, v_ref, g_ref, o_ref, *, a_per_step):
    pre = _dot_nt(xn_ref[...], u_ref[...])
    act = 0.5 * pre * (1.0 + lax.erf(pre * float(1.0 / np.sqrt(2.0))))
    gate = jnp.concatenate([g_ref[a] for a in range(a_per_step)], axis=1)
    y = _dot((act * gate.astype(F32)).astype(BF16), v_ref[...])

    @pl.when(pl.program_id(1) == 0)
    def _():
        o_ref[...] = y

    @pl.when(pl.program_id(1) != 0)
    def _():
        o_ref[...] += y


def _peer(xn, u_b, v_b, gates, tile, a_per_step):
    T = xn.shape[0]
    et = a_per_step * PEER_KEYS
    return pl.pallas_call(
        functools.partial(_peer_kernel, a_per_step=a_per_step),
        grid=(T // tile, PEER_EXPERTS // et),
        in_specs=[pl.BlockSpec((tile, D_MODEL), lambda i, j: (i, 0)),
                  pl.BlockSpec((et, D_MODEL), lambda i, j: (j, 0)),
                  pl.BlockSpec((et, D_MODEL), lambda i, j: (j, 0)),
                  pl.BlockSpec((a_per_step, tile, PEER_KEYS), lambda i, j: (j, i, 0))],
        out_specs=pl.BlockSpec((tile, D_MODEL), lambda i, j: (i, 0)),
        out_shape=jax.ShapeDtypeStruct((T, D_MODEL), F32),
        compiler_params=_params(("parallel", "arbitrary")),
        name="peer",
    )(xn, u_b, v_b, gates)


def _ple_final_kernel(h_ref, y_ref, p_ref, gple_ref, wg_ref, wp_ref, gfin_ref, o_ref, *, last_layer):
    h = h_ref[...] + y_ref[...]
    gate = jax.nn.sigmoid(_dot(_rms(h, gple_ref[...]).astype(BF16), wg_ref[...]))
    h = h + gate * _dot(p_ref[...].astype(BF16), wp_ref[...])
    o_ref[...] = _rms(h, gfin_ref[...]) if last_layer else h


def _ple_final(h1, y, p2, g_ple, w_gate, w_proj, g_final, tile, last_layer):
    T = h1.shape[0]
    full = lambda a: pl.BlockSpec(a.shape, lambda i: (0,) * a.ndim)
    row = lambda w: pl.BlockSpec((tile, w), lambda i: (i, 0))
    return pl.pallas_call(
        functools.partial(_ple_final_kernel, last_layer=last_layer),
        grid=(T // tile,),
        in_specs=[row(D_MODEL), row(D_MODEL), row(PLE_DIM), full(g_ple), full(w_gate),
                  full(w_proj), full(g_final)],
        out_specs=row(D_MODEL),
        out_shape=jax.ShapeDtypeStruct((T, D_MODEL), F32),
        compiler_params=_params(("parallel",)),
        name="ple_final",
    )(h1, y, p2, g_ple, w_gate, w_proj, g_final)


def _rope_lane_tables(seq, rot_dim, first_lane, period, scale):
    half = rot_dim // 2
    inv = ROPE_THETA ** (-jnp.arange(0, rot_dim, 2, dtype=F32) / rot_dim)
    ang = jnp.arange(seq, dtype=F32)[:, None] * inv[None, :]
    cos, sin = jnp.cos(ang), jnp.sin(ang)
    c = jnp.ones((seq, period), F32)
    c = c.at[:, first_lane:first_lane + half].set(cos).at[:, first_lane + half:first_lane + rot_dim].set(cos)
    s_lo = jnp.zeros((seq, period), F32).at[:, first_lane + half:first_lane + rot_dim].set(sin)
    s_hi = jnp.zeros((seq, period), F32).at[:, first_lane:first_lane + half].set(-sin)
    tabs = jnp.stack([c, s_lo, s_hi]) * scale
    return jnp.tile(tabs, (1, 1, LANES // period))


def _pick_tile(n, want):
    t = min(n, want)
    assert n % t == 0, (n, t)
    return t


def kernel(x, p, g_mix, w_in, g_cq, w_uq, g_ckv, w_ukv, g_out_mla, g_out_dil, w_out, g_ffn,
           w_peer_q, peer_keys1, peer_keys2, peer_u, peer_v, g_ple, w_ple_gate, w_ple_proj, g_final):
    batch, seq, _ = x.shape
    depth = p.shape[0]
    T = batch * seq
    mla_w = MLA_HEADS * MLA_V

    tabs = (_rope_lane_tables(seq, MLA_ROPE, MLA_NOPE, LANES, MLA_QK ** -0.5),
            _rope_lane_tables(seq, MLA_ROPE, MLA_NOPE, LANES, 1.0),
            _rope_lane_tables(seq, DIL_ROT, 0, DIL_HEAD_DIM, DIL_HEAD_DIM ** -0.5),
            _rope_lane_tables(seq, DIL_ROT, 0, DIL_HEAD_DIM, 1.0))

    h = x.reshape(T, D_MODEL)
    for i in range(depth):
        wi = w_in[i]
        o1 = MLA_Q_RANK + MLA_KV_RANK
        k_r_cols = jnp.pad(wi[:, o1:o1 + MLA_ROPE], ((0, 0), (MLA_NOPE, LANES - MLA_QK)))
        w_in_p = jnp.concatenate([wi[:, :o1], k_r_cols, wi[:, o1 + MLA_ROPE:]], axis=1).astype(BF16)
        w_uq_p = jnp.pad(w_uq[i].reshape(MLA_Q_RANK, MLA_HEADS, MLA_QK),
                         ((0, 0), (0, 0), (0, LANES - MLA_QK))).reshape(MLA_Q_RANK, -1).astype(BF16)
        w_ukv3 = w_ukv[i].reshape(MLA_KV_RANK, MLA_HEADS, MLA_NOPE + MLA_V)
        w_uk_p = jnp.pad(w_ukv3[:, :, :MLA_NOPE],
                         ((0, 0), (0, 0), (0, LANES - MLA_NOPE))).reshape(MLA_KV_RANK, -1).astype(BF16)
        w_uv = w_ukv3[:, :, MLA_NOPE:].reshape(MLA_KV_RANK, -1).astype(BF16)
        zeros = jnp.zeros((PEER_KEYS, PEER_HALF), F32)
        kdt = jnp.concatenate([jnp.concatenate([peer_keys1[i], zeros], axis=1),
                               jnp.concatenate([zeros, peer_keys2[i]], axis=1)], axis=0).astype(BF16)

        t_proj = _pick_tile(seq, 512)
        qm, km, vm, qd, kd, vd = _in_proj(
            h, g_mix[i][None], w_in_p, g_cq[i][None], w_uq_p, g_ckv[i][None], w_uk_p, w_uv,
            tabs, seq, t_proj)
        tq = _pick_tile(seq, 512)
        o_mla = _mla_attn(qm, km, vm, batch, seq, tq).reshape(T, mla_w)
        o_dil = _dil_attn(qd, kd, vd, batch, seq).reshape(T, DIL_WIDTH)

        h1, xn, ra, rb, rg = _route(
            o_mla, o_dil, h, g_out_mla[i][None], g_out_dil[i][None],
            w_out[i][:mla_w].astype(BF16), w_out[i][mla_w:].astype(BF16), g_ffn[i][None],
            w_peer_q[i].astype(BF16), kdt, _pick_tile(T, 256))
        gates = _gate_build(ra, rb, rg, _pick_tile(T, 128))
        y = _peer(xn, peer_u[i].astype(BF16), peer_v[i].astype(BF16), gates,
                  _pick_tile(T, 1024), 8)
        h = _ple_final(h1, y, p[i].reshape(T, PLE_DIM), g_ple[i][None], w_ple_gate[i].astype(BF16),
                       w_ple_proj[i].astype(BF16), g_final[None], _pick_tile(T, 512),
                       last_layer=(i == depth - 1))
    return h.reshape(batch, seq, D_MODEL)
```

```python
import functools

import numpy as np
import jax
import jax.numpy as jnp
from jax import lax
from jax.experimental import pallas as pl
from jax.experimental.pallas import tpu as pltpu

F32 = jnp.float32
BF16 = jnp.bfloat16

EPS = 1e-6
MASKED_SCORE = -1e30
ROPE_THETA = 500000.0

LANES = 128
D_MODEL = 1024
PLE_DIM = 256
MLA_HEADS = 8
MLA_NOPE = 64
MLA_ROPE = 32
MLA_V = 64
MLA_QK = MLA_NOPE + MLA_ROPE
MLA_Q_RANK = 384
MLA_KV_RANK = 256
DIL_HEADS = 8
DIL_HEAD_DIM = 64
DIL_ROT = 16
DIL_WIDTH = DIL_HEADS * DIL_HEAD_DIM
DIL_PATTERNS = ((128, 1), (512, 4), (2048, 16))
PEER_KEYS = 128
PEER_HEADS = 8
PEER_HALF = 64
PEER_TOPK = 16
PEER_SLOTS = PEER_HEADS * PEER_TOPK
PEER_EXPERTS = PEER_KEYS * PEER_KEYS

VMEM_LIMIT = 48 * 1024 * 1024


def _params(semantics):
    return pltpu.CompilerParams(dimension_semantics=semantics, vmem_limit_bytes=VMEM_LIMIT)


def _rms(x, g):
    return x * lax.rsqrt(jnp.mean(x * x, axis=-1, keepdims=True) + EPS) * g


def _rope(x, tab_ref, half):
    return (x * tab_ref[0]
            + pltpu.roll(x, half, 1) * tab_ref[1]
            + pltpu.roll(x, LANES - half, 1) * tab_ref[2])


def _dot(a, b):
    return jnp.dot(a, b, preferred_element_type=F32)


def _dot_nt(a, b):
    return lax.dot_general(a, b, (((1,), (1,)), ((), ())), preferred_element_type=F32)


def _in_proj_kernel(x_ref, gmix_ref, win_ref, gcq_ref, wuq_ref, gckv_ref, wuk_ref, wuv_ref,
                    tmq_ref, tmk_ref, tdq_ref, tdk_ref,
                    qm_ref, km_ref, vm_ref, qd_ref, kd_ref, vd_ref):
    hn = _rms(x_ref[...], gmix_ref[...]).astype(BF16)
    y = _dot(hn, win_ref[...])
    o = 0
    c_q = y[:, o:o + MLA_Q_RANK]; o += MLA_Q_RANK
    c_kv = y[:, o:o + MLA_KV_RANK]; o += MLA_KV_RANK
    k_r = y[:, o:o + LANES]; o += LANES
    q_d = y[:, o:o + DIL_WIDTH]; o += DIL_WIDTH
    k_d = y[:, o:o + DIL_WIDTH]; o += DIL_WIDTH
    v_d = y[:, o:o + DIL_WIDTH]

    q = _dot(_rms(c_q, gcq_ref[...]).astype(BF16), wuq_ref[...])
    ckvn = _rms(c_kv, gckv_ref[...]).astype(BF16)
    k_n = _dot(ckvn, wuk_ref[...])
    vm_ref[...] = _dot(ckvn, wuv_ref[...]).astype(BF16)
    k_rope = _rope(k_r, tmk_ref, MLA_ROPE // 2)
    for h in range(MLA_HEADS):
        sl = slice(h * LANES, (h + 1) * LANES)
        qm_ref[:, sl] = _rope(q[:, sl], tmq_ref, MLA_ROPE // 2).astype(BF16)
        km_ref[:, sl] = (k_n[:, sl] + k_rope).astype(BF16)
    for c in range(DIL_WIDTH // LANES):
        sl = slice(c * LANES, (c + 1) * LANES)
        qd_ref[:, sl] = _rope(q_d[:, sl], tdq_ref, DIL_ROT // 2)
        kd_ref[:, sl] = _rope(k_d[:, sl], tdk_ref, DIL_ROT // 2)
    vd_ref[...] = v_d


def _in_proj(x2, g_mix, w_in_p, g_cq, w_uq_p, g_ckv, w_uk_p, w_uv, tabs, seq, tile):
    T = x2.shape[0]
    n_pos = seq // tile
    full = lambda a: pl.BlockSpec(a.shape, lambda i: (0,) * a.ndim)
    tab = pl.BlockSpec((3, tile, LANES), lambda i: (0, i % n_pos, 0))
    row = lambda w: pl.BlockSpec((tile, w), lambda i: (i, 0))
    outs = [(T, MLA_HEADS * LANES), (T, MLA_HEADS * LANES), (T, MLA_HEADS * MLA_V),
            (T, DIL_WIDTH), (T, DIL_WIDTH), (T, DIL_WIDTH)]
    return pl.pallas_call(
        _in_proj_kernel,
        grid=(T // tile,),
        in_specs=[row(D_MODEL), full(g_mix), full(w_in_p), full(g_cq), full(w_uq_p),
                  full(g_ckv), full(w_uk_p), full(w_uv), tab, tab, tab, tab],
        out_specs=[row(s[1]) for s in outs],
        out_shape=[jax.ShapeDtypeStruct(s, BF16) for s in outs[:3]]
                  + [jax.ShapeDtypeStruct(s, F32) for s in outs[3:]],
        compiler_params=_params(("parallel",)),
        name="in_proj",
    )(x2, g_mix, w_in_p, g_cq, w_uq_p, g_ckv, w_uk_p, w_uv, *tabs)


def _mla_attn_kernel(q_ref, k_ref, v_ref, o_ref):
    v = v_ref[...]
    lane = lax.broadcasted_iota(jnp.int32, (q_ref.shape[0], LANES), 1)
    out = None
    for h in range(2):
        sl = slice(h * LANES, (h + 1) * LANES)
        s = _dot_nt(q_ref[:, sl], k_ref[:, sl])
        e = jnp.exp(s - jnp.max(s, axis=-1, keepdims=True))
        p = e.astype(BF16)
        half = p.shape[0] // 2
        o = jnp.concatenate([_dot(p[:half], v), _dot(p[half:], v)], axis=0) / jnp.sum(
            e, axis=-1, keepdims=True)
        out = o if out is None else jnp.where(lane < MLA_V, out, o)
    o_ref[...] = out


def _mla_attn(qm, km, vm, batch, seq, tq):
    q3 = qm.reshape(batch, seq, MLA_HEADS * LANES)
    k3 = km.reshape(batch, seq, MLA_HEADS * LANES)
    v3 = vm.reshape(batch, seq, MLA_HEADS * MLA_V)
    return pl.pallas_call(
        _mla_attn_kernel,
        grid=(batch, MLA_HEADS // 2, seq // tq),
        in_specs=[pl.BlockSpec((None, tq, 2 * LANES), lambda b, p, i: (b, i, p)),
                  pl.BlockSpec((None, seq, 2 * LANES), lambda b, p, i: (b, 0, p)),
                  pl.BlockSpec((None, seq, LANES), lambda b, p, i: (b, 0, p))],
        out_specs=pl.BlockSpec((None, tq, LANES), lambda b, p, i: (b, i, p)),
        out_shape=jax.ShapeDtypeStruct((batch, seq, MLA_HEADS * MLA_V), F32),
        compiler_params=_params(("parallel", "parallel", "parallel")),
        name="mla_attn",
    )(q3, k3, v3)


DIL_Q_BLOCK = 128
DIL_K_WINDOW = 256

def _dil_attn_kernel(q_ref, k_ref, v_ref, o_ref, acc_ref, m_ref, l_ref):
    seq = q_ref.shape[0]
    lane = lax.broadcasted_iota(jnp.int32, (1, 1, LANES), 2)
    head0 = lane < DIL_HEAD_DIM

    for p, (window, dil) in enumerate(DIL_PATTERNS):
        sub_len = seq // dil
        radius = window // (2 * dil)
        qb = min(DIL_Q_BLOCK, sub_len)
        kw = min(DIL_K_WINDOW, sub_len)
        assert kw == sub_len or kw >= qb + 2 * radius
        blocks = sub_len // qb
        rel = (lax.broadcasted_iota(jnp.int32, (qb, kw), 0)
               - lax.broadcasted_iota(jnp.int32, (qb, kw), 1))

        def rows(first, count, dil=dil):
            return pl.ds(first, count) if dil == 1 else pl.ds(first, count, stride=dil)

        q_rows, k_rows, shifts = [], [], []
        for res in range(dil):
            for jb in range(blocks):
                j0 = jb * qb
                k0 = min(max(j0 - (kw - qb) // 2, 0), sub_len - kw)
                q_rows.append(rows(res + dil * j0, qb))
                k_rows.append(rows(res + dil * k0, kw))
                shifts.append(j0 - k0)
        near_by_shift = {s: jnp.abs(rel + s) <= radius for s in sorted(set(shifts))}
        near = jnp.stack([near_by_shift[s] for s in shifts])
        q = jnp.stack([q_ref[r, :] for r in q_rows])
        k = jnp.stack([k_ref[r, :] for r in k_rows]).astype(BF16)
        v = jnp.stack([v_ref[r, :] for r in k_rows]).astype(BF16)

        acc, m_all, l_all = None, None, None
        for h in range(2):
            qh = jnp.where(head0 if h == 0 else ~head0, q, 0.0).astype(BF16)
            s = lax.dot_general(qh, k, (((2,), (2,)), ((0,), (0,))), preferred_element_type=F32)
            s = jnp.where(near, s, MASKED_SCORE)
            m = jnp.max(s, axis=-1, keepdims=True)
            e = jnp.exp(s - m)
            l = jnp.sum(e, axis=-1, keepdims=True)
            a = lax.dot_general(e.astype(BF16), v, (((2,), (1,)), ((0,), (0,))),
                                preferred_element_type=F32)
            if h == 0:
                acc, m_all, l_all = a, m, l
            else:
                acc = jnp.where(head0, acc, a)
                m_all = jnp.where(head0, m_all, m)
                l_all = jnp.where(head0, l_all, l)
        for n, r in enumerate(q_rows):
            acc_ref[p, r, :] = acc[n]
            m_ref[p, r, :] = m_all[n]
            l_ref[p, r, :] = l_all[n]

    m = m_ref[...]
    w = jnp.exp(m - jnp.max(m, axis=0, keepdims=True))
    o_ref[...] = jnp.sum(w * acc_ref[...], axis=0) / jnp.sum(w * l_ref[...], axis=0)


def _dil_attn(qd, kd, vd, batch, seq):
    spec = pl.BlockSpec((None, seq, LANES), lambda b, p: (b, 0, p))
    stats = pltpu.VMEM((len(DIL_PATTERNS), seq, LANES), F32)
    return pl.pallas_call(
        _dil_attn_kernel,
        grid=(batch, DIL_HEADS // 2),
        in_specs=[spec, spec, spec],
        out_specs=spec,
        out_shape=jax.ShapeDtypeStruct((batch, seq, DIL_WIDTH), F32),
        scratch_shapes=[stats, stats, stats],
        compiler_params=_params(("parallel", "parallel")),
        name="dil_attn",
    )(qd.reshape(batch, seq, DIL_WIDTH), kd.reshape(batch, seq, DIL_WIDTH),
      vd.reshape(batch, seq, DIL_WIDTH))


class _Ranked:
    def __init__(self, r8):
        self.r8 = r8
        self.rows = []
        self.lo = jnp.zeros(r8.shape, F32)
        self.hi = jnp.zeros(r8.shape, F32)

    def push(self, row):
        k = len(self.rows)
        self.rows.append(row)
        if k < 8:
            self.lo = jnp.where(self.r8 == k, row, self.lo)
        else:
            self.hi = jnp.where(self.r8 == k - 8, row, self.hi)


KEY_STACK = 4


def _top16(blocks, r8):
    groups = []
    for g in range(len(blocks) // KEY_STACK):
        vs = [blocks[g * KEY_STACK + d] for d in range(KEY_STACK)]
        ids = [r8 + float(8 * (g * KEY_STACK + d)) for d in range(KEY_STACK)]
        for a in (0, 2, 1, 0, 2, 1):
            swap = vs[a + 1] > vs[a]
            vs[a], vs[a + 1] = jnp.where(swap, vs[a + 1], vs[a]), jnp.where(swap, vs[a], vs[a + 1])
            ids[a], ids[a + 1] = jnp.where(swap, ids[a + 1], ids[a]), jnp.where(swap, ids[a], ids[a + 1])
        groups.append((vs, ids))

    tv, ti = _Ranked(r8), _Ranked(r8)
    for _ in range(PEER_TOPK):
        m = jnp.max(functools.reduce(jnp.maximum, [vs[0] for vs, _ in groups]),
                    axis=0, keepdims=True)
        low = functools.reduce(jnp.minimum, [jnp.where(vs[0] == m, ids[0], float(PEER_KEYS))
                                            for vs, ids in groups])
        idx = jnp.min(low, axis=0, keepdims=True)
        tv.push(m)
        ti.push(idx)
        for vs, ids in groups:
            hit = ids[0] == idx
            for d in range(KEY_STACK - 1):
                vs[d] = jnp.where(hit, vs[d + 1], vs[d])
                ids[d] = jnp.where(hit, ids[d + 1], ids[d])
            vs[KEY_STACK - 1] = jnp.where(hit, -jnp.inf, vs[KEY_STACK - 1])
    return tv, ti


_POS_SHIFT = float(PEER_EXPERTS)


def _pair_top16(v1, i1, v2, i2, r8):
    ninf = -jnp.inf

    def piece(cand, pos, expert, keep=None):
        if keep is not None:
            cand = jnp.where(keep, cand, ninf)
        return cand, pos * _POS_SHIFT + expert

    pieces = [
        piece(v1.rows[0] + v2.lo, r8, i1.rows[0] * PEER_KEYS + i2.lo),
        piece(v1.rows[0] + v2.hi, r8 + 8.0, i1.rows[0] * PEER_KEYS + i2.hi),
        piece(v1.rows[1] + v2.lo, r8 + 16.0, i1.rows[1] * PEER_KEYS + i2.lo),
        piece(v1.lo + v2.rows[0], r8 * 16.0, i1.lo * PEER_KEYS + i2.rows[0], r8 >= 2),
        piece(v1.hi + v2.rows[0], r8 * 16.0 + 128.0, i1.hi * PEER_KEYS + i2.rows[0]),
        piece(v1.lo + v2.rows[1], r8 * 16.0 + 1.0, i1.lo * PEER_KEYS + i2.rows[1], r8 >= 2),
        piece(v1.rows[2] + v2.lo, r8 + 32.0, i1.rows[2] * PEER_KEYS + i2.lo, (r8 >= 2) & (r8 <= 4)),
        piece(v1.rows[3] + v2.lo, r8 + 48.0, i1.rows[3] * PEER_KEYS + i2.lo, (r8 >= 2) & (r8 <= 3)),
        piece(v1.rows[4] + v2.lo, r8 + 64.0, i1.rows[4] * PEER_KEYS + i2.lo, r8 == 2),
    ]
    cand = jnp.concatenate([p[0] for p in pieces], axis=0)
    key = jnp.concatenate([p[1] for p in pieces], axis=0)
    top, sel = _Ranked(r8), _Ranked(r8)
    for _ in range(PEER_TOPK):
        m = jnp.max(cand, axis=0, keepdims=True)
        kk = jnp.min(jnp.where(cand == m, key, 3.0e38), axis=0, keepdims=True)
        top.push(m)
        sel.push(kk)
        cand = jnp.where(key == kk, ninf, cand)
    return top, sel


def _route_kernel(om_ref, od_ref, x_ref, gom_ref, god_ref, wom_ref, wod_ref, gffn_ref,
                  wq_ref, kdt_ref,
                  h_ref, xn_ref, ra_ref, rb_ref, rg_ref,
                  sc_ref, sa_ref, sb_ref, sg_ref):
    tile = x_ref.shape[0]
    nm = _rms(om_ref[...], gom_ref[...]).astype(BF16)
    nd = _rms(od_ref[...], god_ref[...]).astype(BF16)
    h = x_ref[...] + _dot(nm, wom_ref[...]) + _dot(nd, wod_ref[...])
    h_ref[...] = h
    xn = _rms(h, gffn_ref[...]).astype(BF16)
    xn_ref[...] = xn
    q = _dot(xn, wq_ref[...])
    kdt = kdt_ref[...]
    for hd in range(PEER_HEADS):
        qh = q[:, hd * LANES:(hd + 1) * LANES].astype(BF16)
        sc_ref[hd] = _dot_nt(kdt, qh)

    r8 =lax.broadcasted_iota(jnp.int32, (8, tile), 0).astype(F32)

    def head(hd, carry):
        def key_blocks(first):
            return [sc_ref[hd, first + 8 * d:first + 8 * d + 8, :] for d in range(PEER_KEYS // 8)]

        v1, i1 = _top16(key_blocks(0), r8)
        v2, i2 = _top16(key_blocks(PEER_KEYS), r8)
        top, sel = _pair_top16(v1, i1, v2, i2, r8)
        e_lo = jnp.exp(top.lo - top.rows[0])
        e_hi = jnp.exp(top.hi - top.rows[0])
        inv = 1.0 / (jnp.sum(e_lo, axis=0, keepdims=True) + jnp.sum(e_hi, axis=0, keepdims=True))
        for half, (s, e) in enumerate(((sel.lo, e_lo), (sel.hi, e_hi))):
            rows = slice(half * 8, half * 8 + 8)
            expert = s - jnp.floor(s * (1.0 / _POS_SHIFT)) * _POS_SHIFT
            a = jnp.floor(expert * (1.0 / PEER_KEYS))
            sa_ref[hd, rows, :] = a
            sb_ref[hd, rows, :] = expert - a * PEER_KEYS
            sg_ref[hd, rows, :] = e * inv
        return carry

    lax.fori_loop(0, PEER_HEADS, head, 0, unroll=4)
    a_all = sa_ref[...].reshape(PEER_SLOTS, tile)
    b_all = sb_ref[...].reshape(PEER_SLOTS, tile)
    g_all = sg_ref[...].reshape(PEER_SLOTS, tile)
    for c in range(tile // LANES):
        cols = slice(c * LANES, (c + 1) * LANES)
        ra_ref[cols, :] = a_all[:, cols].T
        rb_ref[cols, :] = b_all[:, cols].T
        rg_ref[cols, :] = g_all[:, cols].T


def _route(o_mla, o_dil, x2, g_om, g_od, w_om, w_od, g_ffn, w_q, kdt, tile):
    T = x2.shape[0]
    full = lambda a: pl.BlockSpec(a.shape, lambda i: (0,) * a.ndim)
    row = lambda w: pl.BlockSpec((tile, w), lambda i: (i, 0))
    return pl.pallas_call(
        _route_kernel,
        grid=(T // tile,),
        in_specs=[row(o_mla.shape[1]), row(o_dil.shape[1]), row(D_MODEL), full(g_om), full(g_od),
                  full(w_om), full(w_od), full(g_ffn), full(w_q), full(kdt)],
        out_specs=[row(D_MODEL), row(D_MODEL), row(PEER_SLOTS), row(PEER_SLOTS), row(PEER_SLOTS)],
        out_shape=[jax.ShapeDtypeStruct((T, D_MODEL), F32),
                   jax.ShapeDtypeStruct((T, D_MODEL), BF16),
                   jax.ShapeDtypeStruct((T, PEER_SLOTS), F32),
                   jax.ShapeDtypeStruct((T, PEER_SLOTS), F32),
                   jax.ShapeDtypeStruct((T, PEER_SLOTS), F32)],
        scratch_shapes=[pltpu.VMEM((PEER_HEADS, 2 * PEER_KEYS, tile), F32),
                        pltpu.VMEM((PEER_HEADS, PEER_TOPK, tile), F32),
                        pltpu.VMEM((PEER_HEADS, PEER_TOPK, tile), F32),
                        pltpu.VMEM((PEER_HEADS, PEER_TOPK, tile), F32)],
        compiler_params=_params(("parallel",)),
        name="route",
    )(o_mla, o_dil, x2, g_om, g_od, w_om, w_od, g_ffn, w_q, kdt)


GATE_PITCH = PEER_KEYS + 8
GATE_UNROLL = 128


def _gate_build_kernel(ra_ref, rb_ref, rg_ref, g_ref, stage_ref):
    key_iota = lax.broadcasted_iota(jnp.int32, (PEER_KEYS, PEER_SLOTS), 0).astype(F32)
    tile = ra_ref.shape[0]

    def token(t, carry):
        a = ra_ref[pl.ds(t, 1), :]
        b = rb_ref[pl.ds(t, 1), :]
        g = rg_ref[pl.ds(t, 1), :]
        one_a = jnp.where(key_iota == a, 1.0, 0.0).astype(BF16)
        g_at_b = jnp.where(key_iota == b, g, 0.0).astype(BF16)
        row0 = pl.multiple_of(t * GATE_PITCH, 8)
        stage_ref[pl.ds(row0, PEER_KEYS), :] = _dot_nt(one_a, g_at_b)
        return carry

    lax.fori_loop(0, tile, token, 0, unroll=GATE_UNROLL)
    for a in range(PEER_KEYS):
        g_ref[a] = stage_ref[pl.ds(a, tile, stride=GATE_PITCH), :].astype(BF16)


def _gate_build(ra, rb, rg, tile):
    T = ra.shape[0]
    row = pl.BlockSpec((tile, PEER_SLOTS), lambda i: (i, 0))
    return pl.pallas_call(
        _gate_build_kernel,
        grid=(T // tile,),
        in_specs=[row, row, row],
        out_specs=pl.BlockSpec((PEER_KEYS, tile, PEER_KEYS), lambda i: (0, i, 0)),
        out_shape=jax.ShapeDtypeStruct((PEER_KEYS, T, PEER_KEYS), BF16),
        scratch_shapes=[pltpu.VMEM((tile * GATE_PITCH, PEER_KEYS), F32)],
        compiler_params=_params(("parallel",)),
        name="gate_build",
    )(ra, rb, rg)


def _peer_kernel(xn_ref, u_ref, v_ref, g_ref, o_ref, *, a_per_step):
    pre = _dot_nt(xn_ref[...], u_ref[...])
    act = 0.5 * pre * (1.0 + lax.erf(pre * float(1.0 / np.sqrt(2.0))))
    gate = jnp.concatenate([g_ref[a] for a in range(a_per_step)], axis=1)
    y = _dot((act * gate.astype(F32)).astype(BF16), v_ref[...])

    @pl.when(pl.program_id(1) == 0)
    def _():
        o_ref[...] = y

    @pl.when(pl.program_id(1) != 0)
    def _():
        o_ref[...] += y


def _peer(xn, u_b, v_b, gates, tile, a_per_step):
    T = xn.shape[0]
    et = a_per_step * PEER_KEYS
    return pl.pallas_call(
        functools.partial(_peer_kernel, a_per_step=a_per_step),
        grid=(T // tile, PEER_EXPERTS // et),
        in_specs=[pl.BlockSpec((tile, D_MODEL), lambda i, j: (i, 0)),
                  pl.BlockSpec((et, D_MODEL), lambda i, j: (j, 0)),
                  pl.BlockSpec((et, D_MODEL), lambda i, j: (j, 0)),
                  pl.BlockSpec((a_per_step, tile, PEER_KEYS), lambda i, j: (j, i, 0))],
        out_specs=pl.BlockSpec((tile, D_MODEL), lambda i, j: (i, 0)),
        out_shape=jax.ShapeDtypeStruct((T, D_MODEL), F32),
        compiler_params=_params(("parallel", "arbitrary")),
        name="peer",
    )(xn, u_b, v_b, gates)


def _ple_final_kernel(h_ref, y_ref, p_ref, gple_ref, wg_ref, wp_ref, gfin_ref, o_ref, *, last_layer):
    h = h_ref[...] + y_ref[...]
    gate = jax.nn.sigmoid(_dot(_rms(h, gple_ref[...]).astype(BF16), wg_ref[...]))
    h = h + gate * _dot(p_ref[...].astype(BF16), wp_ref[...])
    o_ref[...] = _rms(h, gfin_ref[...]) if last_layer else h


def _ple_final(h1, y, p2, g_ple, w_gate, w_proj, g_final, tile, last_layer):
    T = h1.shape[0]
    full = lambda a: pl.BlockSpec(a.shape, lambda i: (0,) * a.ndim)
    row = lambda w: pl.BlockSpec((tile, w), lambda i: (i, 0))
    return pl.pallas_call(
        functools.partial(_ple_final_kernel, last_layer=last_layer),
        grid=(T // tile,),
        in_specs=[row(D_MODEL), row(D_MODEL), row(PLE_DIM), full(g_ple), full(w_gate),
                  full(w_proj), full(g_final)],
        out_specs=row(D_MODEL),
        out_shape=jax.ShapeDtypeStruct((T, D_MODEL), F32),
        compiler_params=_params(("parallel",)),
        name="ple_final",
    )(h1, y, p2, g_ple, w_gate, w_proj, g_final)


def _rope_lane_tables(seq, rot_dim, first_lane, period, scale):
    half = rot_dim // 2
    inv = ROPE_THETA ** (-jnp.arange(0, rot_dim, 2, dtype=F32) / rot_dim)
    ang = jnp.arange(seq, dtype=F32)[:, None] * inv[None, :]
    cos, sin = jnp.cos(ang), jnp.sin(ang)
    c = jnp.ones((seq, period), F32)
    c = c.at[:, first_lane:first_lane + half].set(cos).at[:, first_lane + half:first_lane + rot_dim].set(cos)
    s_lo = jnp.zeros((seq, period), F32).at[:, first_lane + half:first_lane + rot_dim].set(sin)
    s_hi = jnp.zeros((seq, period), F32).at[:, first_lane:first_lane + half].set(-sin)
    tabs = jnp.stack([c, s_lo, s_hi]) * scale
    return jnp.tile(tabs, (1, 1, LANES // period))


def _pick_tile(n, want):
    t = min(n, want)
    assert n % t == 0, (n, t)
    return t


def kernel(x, p, g_mix, w_in, g_cq, w_uq, g_ckv, w_ukv, g_out_mla, g_out_dil, w_out, g_ffn,
           w_peer_q, peer_keys1, peer_keys2, peer_u, peer_v, g_ple, w_ple_gate, w_ple_proj, g_final):
    batch, seq, _ = x.shape
    depth = p.shape[0]
    T = batch * seq
    mla_w = MLA_HEADS * MLA_V

    tabs = (_rope_lane_tables(seq, MLA_ROPE, MLA_NOPE, LANES, MLA_QK ** -0.5),
            _rope_lane_tables(seq, MLA_ROPE, MLA_NOPE, LANES, 1.0),
            _rope_lane_tables(seq, DIL_ROT, 0, DIL_HEAD_DIM, DIL_HEAD_DIM ** -0.5),
            _rope_lane_tables(seq, DIL_ROT, 0, DIL_HEAD_DIM, 1.0))

    h = x.reshape(T, D_MODEL)
    for i in range(depth):
        wi = w_in[i]
        o1 = MLA_Q_RANK + MLA_KV_RANK
        k_r_cols = jnp.pad(wi[:, o1:o1 + MLA_ROPE], ((0, 0), (MLA_NOPE, LANES - MLA_QK)))
        w_in_p = jnp.concatenate([wi[:, :o1], k_r_cols, wi[:, o1 + MLA_ROPE:]], axis=1).astype(BF16)
        w_uq_p = jnp.pad(w_uq[i].reshape(MLA_Q_RANK, MLA_HEADS, MLA_QK),
                         ((0, 0), (0, 0), (0, LANES - MLA_QK))).reshape(MLA_Q_RANK, -1).astype(BF16)
        w_ukv3 = w_ukv[i].reshape(MLA_KV_RANK, MLA_HEADS, MLA_NOPE + MLA_V)
        w_uk_p = jnp.pad(w_ukv3[:, :, :MLA_NOPE],
                         ((0, 0), (0, 0), (0, LANES - MLA_NOPE))).reshape(MLA_KV_RANK, -1).astype(BF16)
        w_uv = w_ukv3[:, :, MLA_NOPE:].reshape(MLA_KV_RANK, -1).astype(BF16)
        zeros = jnp.zeros((PEER_KEYS, PEER_HALF), F32)
        kdt = jnp.concatenate([jnp.concatenate([peer_keys1[i], zeros], axis=1),
                               jnp.concatenate([zeros, peer_keys2[i]], axis=1)], axis=0).astype(BF16)

        t_proj = _pick_tile(seq, 512)
        qm, km, vm, qd, kd, vd = _in_proj(
            h, g_mix[i][None], w_in_p, g_cq[i][None], w_uq_p, g_ckv[i][None], w_uk_p, w_uv,
            tabs, seq, t_proj)
        tq = _pick_tile(seq, 512)
        o_mla = _mla_attn(qm, km, vm, batch, seq, tq).reshape(T, mla_w)
        o_dil = _dil_attn(qd, kd, vd, batch, seq).reshape(T, DIL_WIDTH)

        h1, xn, ra, rb, rg = _route(
            o_mla, o_dil, h, g_out_mla[i][None], g_out_dil[i][None],
            w_out[i][:mla_w].astype(BF16), w_out[i][mla_w:].astype(BF16), g_ffn[i][None],
            w_peer_q[i].astype(BF16), kdt, _pick_tile(T, 256))
        gates = _gate_build(ra, rb, rg, _pick_tile(T, 128))
        y = _peer(xn, peer_u[i].astype(BF16), peer_v[i].astype(BF16), gates,
                  _pick_tile(T, 1024), 8)
        h = _ple_final(h1, y, p[i].reshape(T, PLE_DIM), g_ple[i][None], w_ple_gate[i].astype(BF16),
                       w_ple_proj[i].astype(BF16), g_final[None], _pick_tile(T, 512),
                       last_layer=(i == depth - 1))
    return h.reshape(batch, seq, D_MODEL)
```

```python
import functools

import numpy as np
import jax
import jax.numpy as jnp
from jax import lax
from jax.experimental import pallas as pl
from jax.experimental.pallas import tpu as pltpu

F32 = jnp.float32
BF16 = jnp.bfloat16

EPS = 1e-6
MASKED_SCORE = -1e30
ROPE_THETA = 500000.0

LANES = 128
D_MODEL = 1024
PLE_DIM = 256
MLA_HEADS = 8
MLA_NOPE = 64
MLA_ROPE = 32
MLA_V = 64
MLA_QK = MLA_NOPE + MLA_ROPE
MLA_Q_RANK = 384
MLA_KV_RANK = 256
DIL_HEADS = 8
DIL_HEAD_DIM = 64
DIL_ROT = 16
DIL_WIDTH = DIL_HEADS * DIL_HEAD_DIM
DIL_PATTERNS = ((128, 1), (512, 4), (2048, 16))
PEER_KEYS = 128
PEER_HEADS = 8
PEER_HALF = 64
PEER_TOPK = 16
PEER_SLOTS = PEER_HEADS * PEER_TOPK
PEER_EXPERTS = PEER_KEYS * PEER_KEYS

VMEM_LIMIT = 48 * 1024 * 1024


def _params(semantics):
    return pltpu.CompilerParams(dimension_semantics=semantics, vmem_limit_bytes=VMEM_LIMIT)


def _rms(x, g):
    return x * lax.rsqrt(jnp.mean(x * x, axis=-1, keepdims=True) + EPS) * g


def _rope(x, tab_ref, half):
    return (x * tab_ref[0]
            + pltpu.roll(x, half, 1) * tab_ref[1]
            + pltpu.roll(x, LANES - half, 1) * tab_ref[2])


def _dot(a, b):
    return jnp.dot(a, b, preferred_element_type=F32)


def _dot_nt(a, b):
    return lax.dot_general(a, b, (((1,), (1,)), ((), ())), preferred_element_type=F32)


def _in_proj_kernel(x_ref, gmix_ref, win_ref, gcq_ref, wuq_ref, gckv_ref, wuk_ref, wuv_ref,
                    tmq_ref, tmk_ref, tdq_ref, tdk_ref,
                    qm_ref, km_ref, vm_ref, qd_ref, kd_ref, vd_ref):
    hn = _rms(x_ref[...], gmix_ref[...]).astype(BF16)
    y = _dot(hn, win_ref[...])
    o = 0
    c_q = y[:, o:o + MLA_Q_RANK]; o += MLA_Q_RANK
    c_kv = y[:, o:o + MLA_KV_RANK]; o += MLA_KV_RANK
    k_r = y[:, o:o + LANES]; o += LANES
    q_d = y[:, o:o + DIL_WIDTH]; o += DIL_WIDTH
    k_d = y[:, o:o + DIL_WIDTH]; o += DIL_WIDTH
    v_d = y[:, o:o + DIL_WIDTH]

    q = _dot(_rms(c_q, gcq_ref[...]).astype(BF16), wuq_ref[...])
    ckvn = _rms(c_kv, gckv_ref[...]).astype(BF16)
    k_n = _dot(ckvn, wuk_ref[...])
    vm_ref[...] = _dot(ckvn, wuv_ref[...]).astype(BF16)
    k_rope = _rope(k_r, tmk_ref, MLA_ROPE // 2)
    for h in range(MLA_HEADS):
        sl = slice(h * LANES, (h + 1) * LANES)
        qm_ref[:, sl] = _rope(q[:, sl], tmq_ref, MLA_ROPE // 2).astype(BF16)
        km_ref[:, sl] = (k_n[:, sl] + k_rope).astype(BF16)
    for c in range(DIL_WIDTH // LANES):
        sl = slice(c * LANES, (c + 1) * LANES)
        qd_ref[:, sl] = _rope(q_d[:, sl], tdq_ref, DIL_ROT // 2)
        kd_ref[:, sl] = _rope(k_d[:, sl], tdk_ref, DIL_ROT // 2)
    vd_ref[...] = v_d


def _in_proj(x2, g_mix, w_in_p, g_cq, w_uq_p, g_ckv, w_uk_p, w_uv, tabs, seq, tile):
    T = x2.shape[0]
    n_pos = seq // tile
    full = lambda a: pl.BlockSpec(a.shape, lambda i: (0,) * a.ndim)
    tab = pl.BlockSpec((3, tile, LANES), lambda i: (0, i % n_pos, 0))
    row = lambda w: pl.BlockSpec((tile, w), lambda i: (i, 0))
    outs = [(T, MLA_HEADS * LANES), (T, MLA_HEADS * LANES), (T, MLA_HEADS * MLA_V),
            (T, DIL_WIDTH), (T, DIL_WIDTH), (T, DIL_WIDTH)]
    return pl.pallas_call(
        _in_proj_kernel,
        grid=(T // tile,),
        in_specs=[row(D_MODEL), full(g_mix), full(w_in_p), full(g_cq), full(w_uq_p),
                  full(g_ckv), full(w_uk_p), full(w_uv), tab, tab, tab, tab],
        out_specs=[row(s[1]) for s in outs],
        out_shape=[jax.ShapeDtypeStruct(s, BF16) for s in outs[:3]]
                  + [jax.ShapeDtypeStruct(s, F32) for s in outs[3:]],
        compiler_params=_params(("parallel",)),
        name="in_proj",
    )(x2, g_mix, w_in_p, g_cq, w_uq_p, g_ckv, w_uk_p, w_uv, *tabs)


def _mla_attn_kernel(q_ref, k_ref, v_ref, o_ref):
    v = v_ref[...]
    lane = lax.broadcasted_iota(jnp.int32, (q_ref.shape[0], LANES), 1)
    out = None
    for h in range(2):
        sl = slice(h * LANES, (h + 1) * LANES)
        s = _dot_nt(q_ref[:, sl], k_ref[:, sl])
        e = jnp.exp(s - jnp.max(s, axis=-1, keepdims=True))
        p = e.astype(BF16)
        half = p.shape[0] // 2
        o = jnp.concatenate([_dot(p[:half], v), _dot(p[half:], v)], axis=0) / jnp.sum(
            e, axis=-1, keepdims=True)
        out = o if out is None else jnp.where(lane < MLA_V, out, o)
    o_ref[...] = out


def _mla_attn(qm, km, vm, batch, seq, tq):
    q3 = qm.reshape(batch, seq, MLA_HEADS * LANES)
    k3 = km.reshape(batch, seq, MLA_HEADS * LANES)
    v3 = vm.reshape(batch, seq, MLA_HEADS * MLA_V)
    return pl.pallas_call(
        _mla_attn_kernel,
        grid=(batch, MLA_HEADS // 2, seq // tq),
        in_specs=[pl.BlockSpec((None, tq, 2 * LANES), lambda b, p, i: (b, i, p)),
                  pl.BlockSpec((None, seq, 2 * LANES), lambda b, p, i: (b, 0, p)),
                  pl.BlockSpec((None, seq, LANES), lambda b, p, i: (b, 0, p))],
        out_specs=pl.BlockSpec((None, tq, LANES), lambda b, p, i: (b, i, p)),
        out_shape=jax.ShapeDtypeStruct((batch, seq, MLA_HEADS * MLA_V), F32),
        compiler_params=_params(("parallel", "parallel", "parallel")),
        name="mla_attn",
    )(q3, k3, v3)


DIL_Q_BLOCK = 128
DIL_K_WINDOW = 256

def _dil_attn_kernel(q_ref, k_ref, v_ref, o_ref, acc_ref, m_ref, l_ref):
    seq = q_ref.shape[0]
    lane = lax.broadcasted_iota(jnp.int32, (1, 1, LANES), 2)
    head0 = lane < DIL_HEAD_DIM

    for p, (window, dil) in enumerate(DIL_PATTERNS):
        sub_len = seq // dil
        radius = window // (2 * dil)
        qb = min(DIL_Q_BLOCK, sub_len)
        kw = min(DIL_K_WINDOW, sub_len)
        assert kw == sub_len or kw >= qb + 2 * radius
        blocks = sub_len // qb
        rel = (lax.broadcasted_iota(jnp.int32, (qb, kw), 0)
               - lax.broadcasted_iota(jnp.int32, (qb, kw), 1))

        def rows(first, count, dil=dil):
            return pl.ds(first, count) if dil == 1 else pl.ds(first, count, stride=dil)

        q_rows, k_rows, shifts = [], [], []
        for res in range(dil):
            for jb in range(blocks):
                j0 = jb * qb
                k0 = min(max(j0 - (kw - qb) // 2, 0), sub_len - kw)
                q_rows.append(rows(res + dil * j0, qb))
                k_rows.append(rows(res + dil * k0, kw))
                shifts.append(j0 - k0)
        near_by_shift = {s: jnp.abs(rel + s) <= radius for s in sorted(set(shifts))}
        near = jnp.stack([near_by_shift[s] for s in shifts])
        q = jnp.stack([q_ref[r, :] for r in q_rows])
        k = jnp.stack([k_ref[r, :] for r in k_rows]).astype(BF16)
        v = jnp.stack([v_ref[r, :] for r in k_rows]).astype(BF16)

        acc, m_all, l_all = None, None, None
        for h in range(2):
            qh = jnp.where(head0 if h == 0 else ~head0, q, 0.0).astype(BF16)
            s = lax.dot_general(qh, k, (((2,), (2,)), ((0,), (0,))), preferred_element_type=F32)
            s = jnp.where(near, s, MASKED_SCORE)
            m = jnp.max(s, axis=-1, keepdims=True)
            e = jnp.exp(s - m)
            l = jnp.sum(e, axis=-1, keepdims=True)
            a = lax.dot_general(e.astype(BF16), v, (((2,), (1,)), ((0,), (0,))),
                                preferred_element_type=F32)
            if h == 0:
                acc, m_all, l_all = a, m, l
            else:
                acc = jnp.where(head0, acc, a)
                m_all = jnp.where(head0, m_all, m)
                l_all = jnp.where(head0, l_all, l)
        for n, r in enumerate(q_rows):
            acc_ref[p, r, :] = acc[n]
            m_ref[p, r, :] = m_all[n]
            l_ref[p, r, :] = l_all[n]

    m = m_ref[...]
    w = jnp.exp(m - jnp.max(m, axis=0, keepdims=True))
    o_ref[...] = jnp.sum(w * acc_ref[...], axis=0) / jnp.sum(w * l_ref[...], axis=0)


def _dil_attn(qd, kd, vd, batch, seq):
    spec = pl.BlockSpec((None, seq, LANES), lambda b, p: (b, 0, p))
    stats = pltpu.VMEM((len(DIL_PATTERNS), seq, LANES), F32)
    return pl.pallas_call(
        _dil_attn_kernel,
        grid=(batch, DIL_HEADS // 2),
        in_specs=[spec, spec, spec],
        out_specs=spec,
        out_shape=jax.ShapeDtypeStruct((batch, seq, DIL_WIDTH), F32),
        scratch_shapes=[stats, stats, stats],
        compiler_params=_params(("parallel", "parallel")),
        name="dil_attn",
    )(qd.reshape(batch, seq, DIL_WIDTH), kd.reshape(batch, seq, DIL_WIDTH),
      vd.reshape(batch, seq, DIL_WIDTH))


class _Ranked:
    def __init__(self, r8):
        self.r8 = r8
        self.rows = []
        self.lo = jnp.zeros(r8.shape, F32)
        self.hi = jnp.zeros(r8.shape, F32)

    def push(self, row):
        k = len(self.rows)
        self.rows.append(row)
        if k < 8:
            self.lo = jnp.where(self.r8 == k, row, self.lo)
        else:
            self.hi = jnp.where(self.r8 == k - 8, row, self.hi)


KEY_STACK = 4


def _top16(blocks, r8):
    groups = []
    for g in range(len(blocks) // KEY_STACK):
        vs = [blocks[g * KEY_STACK + d] for d in range(KEY_STACK)]
        ids = [r8 + float(8 * (g * KEY_STACK + d)) for d in range(KEY_STACK)]
        for a in (0, 2, 1, 0, 2, 1):
            swap = vs[a + 1] > vs[a]
            vs[a], vs[a + 1] = jnp.where(swap, vs[a + 1], vs[a]), jnp.where(swap, vs[a], vs[a + 1])
            ids[a], ids[a + 1] = jnp.where(swap, ids[a + 1], ids[a]), jnp.where(swap, ids[a], ids[a + 1])
        groups.append((vs, ids))

    tv, ti = _Ranked(r8), _Ranked(r8)
    for _ in range(PEER_TOPK):
        m = jnp.max(functools.reduce(jnp.maximum, [vs[0] for vs, _ in groups]),
                    axis=0, keepdims=True)
        low = functools.reduce(jnp.minimum, [jnp.where(vs[0] == m, ids[0], float(PEER_KEYS))
                                            for vs, ids in groups])
        idx = jnp.min(low, axis=0, keepdims=True)
        tv.push(m)
        ti.push(idx)
        for vs, ids in groups:
            hit = ids[0] == idx
            for d in range(KEY_STACK - 1):
                vs[d] = jnp.where(hit, vs[d + 1], vs[d])
                ids[d] = jnp.where(hit, ids[d + 1], ids[d])
            vs[KEY_STACK - 1] = jnp.where(hit, -jnp.inf, vs[KEY_STACK - 1])
    return tv, ti


_POS_SHIFT = float(PEER_EXPERTS)


def _pair_top16(v1, i1, v2, i2, r8):
    ninf = -jnp.inf

    def piece(cand, pos, expert, keep=None):
        if keep is not None:
            cand = jnp.where(keep, cand, ninf)
        return cand, pos * _POS_SHIFT + expert

    pieces = [
        piece(v1.rows[0] + v2.lo, r8, i1.rows[0] * PEER_KEYS + i2.lo),
        piece(v1.rows[0] + v2.hi, r8 + 8.0, i1.rows[0] * PEER_KEYS + i2.hi),
        piece(v1.rows[1] + v2.lo, r8 + 16.0, i1.rows[1] * PEER_KEYS + i2.lo),
        piece(v1.lo + v2.rows[0], r8 * 16.0, i1.lo * PEER_KEYS + i2.rows[0], r8 >= 2),
        piece(v1.hi + v2.rows[0], r8 * 16.0 + 128.0, i1.hi * PEER_KEYS + i2.rows[0]),
        piece(v1.lo + v2.rows[1], r8 * 16.0 + 1.0, i1.lo * PEER_KEYS + i2.rows[1], r8 >= 2),
        piece(v1.rows[2] + v2.lo, r8 + 32.0, i1.rows[2] * PEER_KEYS + i2.lo, (r8 >= 2) & (r8 <= 4)),
        piece(v1.rows[3] + v2.lo, r8 + 48.0, i1.rows[3] * PEER_KEYS + i2.lo, (r8 >= 2) & (r8 <= 3)),
        piece(v1.rows[4] + v2.lo, r8 + 64.0, i1.rows[4] * PEER_KEYS + i2.lo, r8 == 2),
    ]
    cand = jnp.concatenate([p[0] for p in pieces], axis=0)
    key = jnp.concatenate([p[1] for p in pieces], axis=0)
    top, sel = _Ranked(r8), _Ranked(r8)
    for _ in range(PEER_TOPK):
        m = jnp.max(cand, axis=0, keepdims=True)
        kk = jnp.min(jnp.where(cand == m, key, 3.0e38), axis=0, keepdims=True)
        top.push(m)
        sel.push(kk)
        cand = jnp.where(key == kk, ninf, cand)
    return top, sel


GATE_PITCH = PEER_KEYS + 8
ROUTE_HEADS_PER_ITER = 2
GATE_BATCH = 8


def _build_gates(slot_ref, first, count, stage_ref, g_ref):
    key_iota = lax.broadcasted_iota(jnp.int32, (1, PEER_KEYS, PEER_SLOTS), 1).astype(F32)
    for t0 in range(0, count, GATE_BATCH):
        rows = pl.ds(first + t0, GATE_BATCH)
        a = slot_ref[0, rows, :][:, None, :]
        b = slot_ref[1, rows, :][:, None, :]
        g = slot_ref[2, rows, :][:, None, :]
        one_a = jnp.where(key_iota == a, 1.0, 0.0).astype(BF16)
        g_at_b = jnp.where(key_iota == b, g, 0.0).astype(BF16)
        gates = lax.dot_general(one_a, g_at_b, (((2,), (2,)), ((0,), (0,))),
                                preferred_element_type=F32)
        for t in range(GATE_BATCH):
            r0 = (t0 + t) * GATE_PITCH
            stage_ref[r0:r0 + PEER_KEYS, :] = gates[t]
    for a in range(PEER_KEYS):
        g_ref[a, pl.ds(first, count), :] = stage_ref[
            pl.ds(a, count, stride=GATE_PITCH), :].astype(BF16)


def _route_kernel(om_ref, od_ref, x_ref, gom_ref, god_ref, wom_ref, wod_ref, gffn_ref,
                  wq_ref, kdt_ref,
                  h_ref, xn_ref, g_ref,
                  sc_ref, sa_ref, sb_ref, sg_ref, slots_ref, stage_ref):
    step = pl.program_id(0)
    tile = x_ref.shape[0]

    @pl.when(step == 0)
    def _():
        slots_ref[...] = jnp.zeros_like(slots_ref)

    prev_slots = slots_ref.at[(step + 1) % 2]
    nm = _rms(om_ref[...], gom_ref[...]).astype(BF16)
    nd = _rms(od_ref[...], god_ref[...]).astype(BF16)
    h = x_ref[...] + _dot(nm, wom_ref[...]) + _dot(nd, wod_ref[...])
    h_ref[...] = h
    xn = _rms(h, gffn_ref[...]).astype(BF16)
    xn_ref[...] = xn
    q = _dot(xn, wq_ref[...])
    kdt = kdt_ref[...]
    for hd in range(PEER_HEADS):
        qh = q[:, hd * LANES:(hd + 1) * LANES].astype(BF16)
        sc_ref[hd] = _dot_nt(kdt, qh)

    r8 = lax.broadcasted_iota(jnp.int32, (8, LANES), 0).astype(F32)

    def rank(hd, cols):
        def key_blocks(first):
            return [sc_ref[hd, first + 8 * d:first + 8 * d + 8, cols] for d in range(PEER_KEYS // 8)]

        v1, i1 = _top16(key_blocks(0), r8)
        v2, i2 = _top16(key_blocks(PEER_KEYS), r8)
        top, sel = _pair_top16(v1, i1, v2, i2, r8)
        e_lo = jnp.exp(top.lo - top.rows[0])
        e_hi = jnp.exp(top.hi - top.rows[0])
        inv = 1.0 / (jnp.sum(e_lo, axis=0, keepdims=True) + jnp.sum(e_hi, axis=0, keepdims=True))
        for half, (s, e) in enumerate(((sel.lo, e_lo), (sel.hi, e_hi))):
            rows = slice(half * 8, half * 8 + 8)
            expert = s - jnp.floor(s * (1.0 / _POS_SHIFT)) * _POS_SHIFT
            a = jnp.floor(expert * (1.0 / PEER_KEYS))
            sa_ref[hd, rows, cols] = a
            sb_ref[hd, rows, cols] = expert - a * PEER_KEYS
            sg_ref[hd, rows, cols] = e * inv

    def head(hd):
        for c in range(tile // LANES):
            rank(hd, slice(c * LANES, (c + 1) * LANES))

    iters = PEER_HEADS // ROUTE_HEADS_PER_ITER
    tokens_per_iter = tile // iters

    def rank_and_build(it, carry):
        for k in range(ROUTE_HEADS_PER_ITER):
            head(it * ROUTE_HEADS_PER_ITER + k)
        first = pl.multiple_of(it * tokens_per_iter, tokens_per_iter)
        _build_gates(prev_slots, first, tokens_per_iter, stage_ref, g_ref)
        return carry

    lax.fori_loop(0, iters, rank_and_build, 0)
    slots = slots_ref.at[step % 2]
    for k, ranked_ref in enumerate((sa_ref, sb_ref, sg_ref)):
        all_slots = ranked_ref[...].reshape(PEER_SLOTS, tile)
        for c in range(tile // LANES):
            cols = slice(c * LANES, (c + 1) * LANES)
            slots[k, cols, :] = all_slots[:, cols].T


def _route(o_mla, o_dil, x2, g_om, g_od, w_om, w_od, g_ffn, w_q, kdt, tile):
    T = x2.shape[0]
    steps = T // tile
    full = lambda a: pl.BlockSpec(a.shape, lambda i: (0,) * a.ndim)
    row = lambda w: pl.BlockSpec((tile, w), lambda i: (jnp.minimum(i, steps - 1), 0))
    tokens_per_iter = tile // (PEER_HEADS // ROUTE_HEADS_PER_ITER)
    return pl.pallas_call(
        _route_kernel,
        grid=(steps + 1,),
        in_specs=[row(o_mla.shape[1]), row(o_dil.shape[1]), row(D_MODEL), full(g_om), full(g_od),
                  full(w_om), full(w_od), full(g_ffn), full(w_q), full(kdt)],
        out_specs=[row(D_MODEL), row(D_MODEL),
                   pl.BlockSpec((PEER_KEYS, tile, PEER_KEYS),
                                lambda i: (0, jnp.maximum(i - 1, 0), 0))],
        out_shape=[jax.ShapeDtypeStruct((T, D_MODEL), F32),
                   jax.ShapeDtypeStruct((T, D_MODEL), BF16),
                   jax.ShapeDtypeStruct((PEER_KEYS, T, PEER_KEYS), BF16)],
        scratch_shapes=[pltpu.VMEM((PEER_HEADS, 2 * PEER_KEYS, tile), F32),
                        pltpu.VMEM((PEER_HEADS, PEER_TOPK, tile), F32),
                        pltpu.VMEM((PEER_HEADS, PEER_TOPK, tile), F32),
                        pltpu.VMEM((PEER_HEADS, PEER_TOPK, tile), F32),
                        pltpu.VMEM((2, 3, tile, PEER_SLOTS), F32),
                        pltpu.VMEM((tokens_per_iter * GATE_PITCH, PEER_KEYS), F32)],
        compiler_params=_params(("arbitrary",)),
        name="route",
    )(o_mla, o_dil, x2, g_om, g_od, w_om, w_od, g_ffn, w_q, kdt)


def _peer_kernel(xn_ref, u_ref, v_ref, g_ref, o_ref, *, a_per_step):
    pre = _dot_nt(xn_ref[...], u_ref[...])
    act = 0.5 * pre * (1.0 + lax.erf(pre * float(1.0 / np.sqrt(2.0))))
    gate = jnp.concatenate([g_ref[a] for a in range(a_per_step)], axis=1)
    y = _dot((act * gate.astype(F32)).astype(BF16), v_ref[...])

    @pl.when(pl.program_id(1) == 0)
    def _():
        o_ref[...] = y

    @pl.when(pl.program_id(1) != 0)
    def _():
        o_ref[...] += y


def _peer(xn, u_b, v_b, gates, tile, a_per_step):
    T = xn.shape[0]
    et = a_per_step * PEER_KEYS
    return pl.pallas_call(
        functools.partial(_peer_kernel, a_per_step=a_per_step),
        grid=(T // tile, PEER_EXPERTS // et),
        in_specs=[pl.BlockSpec((tile, D_MODEL), lambda i, j: (i, 0)),
                  pl.BlockSpec((et, D_MODEL), lambda i, j: (j, 0)),
                  pl.BlockSpec((et, D_MODEL), lambda i, j: (j, 0)),
                  pl.BlockSpec((a_per_step, tile, PEER_KEYS), lambda i, j: (j, i, 0))],
        out_specs=pl.BlockSpec((tile, D_MODEL), lambda i, j: (i, 0)),
        out_shape=jax.ShapeDtypeStruct((T, D_MODEL), F32),
        compiler_params=_params(("parallel", "arbitrary")),
        name="peer",
    )(xn, u_b, v_b, gates)


def _ple_final_kernel(h_ref, y_ref, p_ref, gple_ref, wg_ref, wp_ref, gfin_ref, o_ref, *, last_layer):
    h = h_ref[...] + y_ref[...]
    gate = jax.nn.sigmoid(_dot(_rms(h, gple_ref[...]).astype(BF16), wg_ref[...]))
    h = h + gate * _dot(p_ref[...].astype(BF16), wp_ref[...])
    o_ref[...] = _rms(h, gfin_ref[...]) if last_layer else h


def _ple_final(h1, y, p2, g_ple, w_gate, w_proj, g_final, tile, last_layer):
    T = h1.shape[0]
    full = lambda a: pl.BlockSpec(a.shape, lambda i: (0,) * a.ndim)
    row = lambda w: pl.BlockSpec((tile, w), lambda i: (i, 0))
    return pl.pallas_call(
        functools.partial(_ple_final_kernel, last_layer=last_layer),
        grid=(T // tile,),
        in_specs=[row(D_MODEL), row(D_MODEL), row(PLE_DIM), full(g_ple), full(w_gate),
                  full(w_proj), full(g_final)],
        out_specs=row(D_MODEL),
        out_shape=jax.ShapeDtypeStruct((T, D_MODEL), F32),
        compiler_params=_params(("parallel",)),
        name="ple_final",
    )(h1, y, p2, g_ple, w_gate, w_proj, g_final)


def _rope_lane_tables(seq, rot_dim, first_lane, period, scale):
    half = rot_dim // 2
    inv = ROPE_THETA ** (-jnp.arange(0, rot_dim, 2, dtype=F32) / rot_dim)
    ang = jnp.arange(seq, dtype=F32)[:, None] * inv[None, :]
    cos, sin = jnp.cos(ang), jnp.sin(ang)
    c = jnp.ones((seq, period), F32)
    c = c.at[:, first_lane:first_lane + half].set(cos).at[:, first_lane + half:first_lane + rot_dim].set(cos)
    s_lo = jnp.zeros((seq, period), F32).at[:, first_lane + half:first_lane + rot_dim].set(sin)
    s_hi = jnp.zeros((seq, period), F32).at[:, first_lane:first_lane + half].set(-sin)
    tabs = jnp.stack([c, s_lo, s_hi]) * scale
    return jnp.tile(tabs, (1, 1, LANES // period))


def _pick_tile(n, want):
    t = min(n, want)
    assert n % t == 0, (n, t)
    return t


def kernel(x, p, g_mix, w_in, g_cq, w_uq, g_ckv, w_ukv, g_out_mla, g_out_dil, w_out, g_ffn,
           w_peer_q, peer_keys1, peer_keys2, peer_u, peer_v, g_ple, w_ple_gate, w_ple_proj, g_final):
    batch, seq, _ = x.shape
    depth = p.shape[0]
    T = batch * seq
    mla_w = MLA_HEADS * MLA_V

    tabs = (_rope_lane_tables(seq, MLA_ROPE, MLA_NOPE, LANES, MLA_QK ** -0.5),
            _rope_lane_tables(seq, MLA_ROPE, MLA_NOPE, LANES, 1.0),
            _rope_lane_tables(seq, DIL_ROT, 0, DIL_HEAD_DIM, DIL_HEAD_DIM ** -0.5),
            _rope_lane_tables(seq, DIL_ROT, 0, DIL_HEAD_DIM, 1.0))

    h = x.reshape(T, D_MODEL)
    for i in range(depth):
        wi = w_in[i]
        o1 = MLA_Q_RANK + MLA_KV_RANK
        k_r_cols = jnp.pad(wi[:, o1:o1 + MLA_ROPE], ((0, 0), (MLA_NOPE, LANES - MLA_QK)))
        w_in_p = jnp.concatenate([wi[:, :o1], k_r_cols, wi[:, o1 + MLA_ROPE:]], axis=1).astype(BF16)
        w_uq_p = jnp.pad(w_uq[i].reshape(MLA_Q_RANK, MLA_HEADS, MLA_QK),
                         ((0, 0), (0, 0), (0, LANES - MLA_QK))).reshape(MLA_Q_RANK, -1).astype(BF16)
        w_ukv3 = w_ukv[i].reshape(MLA_KV_RANK, MLA_HEADS, MLA_NOPE + MLA_V)
        w_uk_p = jnp.pad(w_ukv3[:, :, :MLA_NOPE],
                         ((0, 0), (0, 0), (0, LANES - MLA_NOPE))).reshape(MLA_KV_RANK, -1).astype(BF16)
        w_uv = w_ukv3[:, :, MLA_NOPE:].reshape(MLA_KV_RANK, -1).astype(BF16)
        zeros = jnp.zeros((PEER_KEYS, PEER_HALF), F32)
        kdt = jnp.concatenate([jnp.concatenate([peer_keys1[i], zeros], axis=1),
                               jnp.concatenate([zeros, peer_keys2[i]], axis=1)], axis=0).astype(BF16)

        t_proj = _pick_tile(seq, 512)
        qm, km, vm, qd, kd, vd = _in_proj(
            h, g_mix[i][None], w_in_p, g_cq[i][None], w_uq_p, g_ckv[i][None], w_uk_p, w_uv,
            tabs, seq, t_proj)
        tq = _pick_tile(seq, 512)
        o_mla = _mla_attn(qm, km, vm, batch, seq, tq).reshape(T, mla_w)
        o_dil = _dil_attn(qd, kd, vd, batch, seq).reshape(T, DIL_WIDTH)

        h1, xn, gates = _route(
            o_mla, o_dil, h, g_out_mla[i][None], g_out_dil[i][None],
            w_out[i][:mla_w].astype(BF16), w_out[i][mla_w:].astype(BF16), g_ffn[i][None],
            w_peer_q[i].astype(BF16), kdt, _pick_tile(T, 256))
        y = _peer(xn, peer_u[i].astype(BF16), peer_v[i].astype(BF16), gates,
                  _pick_tile(T, 1024), 8)
        h = _ple_final(h1, y, p[i].reshape(T, PLE_DIM), g_ple[i][None], w_ple_gate[i].astype(BF16),
                       w_ple_proj[i].astype(BF16), g_final[None], _pick_tile(T, 512),
                       last_layer=(i == depth - 1))
    return h.reshape(batch, seq, D_MODEL)
```

```python
import functools

import numpy as np
import jax
import jax.numpy as jnp
from jax import lax
from jax.experimental import pallas as pl
from jax.experimental.pallas import tpu as pltpu

F32 = jnp.float32
BF16 = jnp.bfloat16

EPS = 1e-6
MASKED_SCORE = -1e30
ROPE_THETA = 500000.0

LANES = 128
D_MODEL = 1024
PLE_DIM = 256
MLA_HEADS = 8
MLA_NOPE = 64
MLA_ROPE = 32
MLA_V = 64
MLA_QK = MLA_NOPE + MLA_ROPE
MLA_Q_RANK = 384
MLA_KV_RANK = 256
DIL_HEADS = 8
DIL_HEAD_DIM = 64
DIL_ROT = 16
DIL_WIDTH = DIL_HEADS * DIL_HEAD_DIM
DIL_PATTERNS = ((128, 1), (512, 4), (2048, 16))
PEER_KEYS = 128
PEER_HEADS = 8
PEER_HALF = 64
PEER_TOPK = 16
PEER_SLOTS = PEER_HEADS * PEER_TOPK
PEER_EXPERTS = PEER_KEYS * PEER_KEYS

VMEM_LIMIT = 48 * 1024 * 1024


def _params(semantics):
    return pltpu.CompilerParams(dimension_semantics=semantics, vmem_limit_bytes=VMEM_LIMIT)


def _rms(x, g):
    return x * lax.rsqrt(jnp.mean(x * x, axis=-1, keepdims=True) + EPS) * g


def _rope(x, tab_ref, half):
    return (x * tab_ref[0]
            + pltpu.roll(x, half, 1) * tab_ref[1]
            + pltpu.roll(x, LANES - half, 1) * tab_ref[2])


def _dot(a, b):
    return jnp.dot(a, b, preferred_element_type=F32)


def _dot_nt(a, b):
    return lax.dot_general(a, b, (((1,), (1,)), ((), ())), preferred_element_type=F32)


def _in_proj_kernel(x_ref, gmix_ref, win_ref, gcq_ref, wuq_ref, gckv_ref, wuk_ref, wuv_ref,
                    tmq_ref, tmk_ref, tdq_ref, tdk_ref,
                    qm_ref, km_ref, vm_ref, qd_ref, kd_ref, vd_ref):
    hn = _rms(x_ref[...], gmix_ref[...]).astype(BF16)
    y = _dot(hn, win_ref[...])
    o = 0
    c_q = y[:, o:o + MLA_Q_RANK]; o += MLA_Q_RANK
    c_kv = y[:, o:o + MLA_KV_RANK]; o += MLA_KV_RANK
    k_r = y[:, o:o + LANES]; o += LANES
    q_d = y[:, o:o + DIL_WIDTH]; o += DIL_WIDTH
    k_d = y[:, o:o + DIL_WIDTH]; o += DIL_WIDTH
    v_d = y[:, o:o + DIL_WIDTH]

    q = _dot(_rms(c_q, gcq_ref[...]).astype(BF16), wuq_ref[...])
    ckvn = _rms(c_kv, gckv_ref[...]).astype(BF16)
    k_n = _dot(ckvn, wuk_ref[...])
    vm_ref[...] = _dot(ckvn, wuv_ref[...]).astype(BF16)
    k_rope = _rope(k_r, tmk_ref, MLA_ROPE // 2)
    for h in range(MLA_HEADS):
        sl = slice(h * LANES, (h + 1) * LANES)
        qm_ref[:, sl] = _rope(q[:, sl], tmq_ref, MLA_ROPE // 2).astype(BF16)
        km_ref[:, sl] = (k_n[:, sl] + k_rope).astype(BF16)
    for c in range(DIL_WIDTH // LANES):
        sl = slice(c * LANES, (c + 1) * LANES)
        qd_ref[:, sl] = _rope(q_d[:, sl], tdq_ref, DIL_ROT // 2)
        kd_ref[:, sl] = _rope(k_d[:, sl], tdk_ref, DIL_ROT // 2)
    vd_ref[...] = v_d


def _in_proj(x2, g_mix, w_in_p, g_cq, w_uq_p, g_ckv, w_uk_p, w_uv, tabs, seq, tile):
    T = x2.shape[0]
    n_pos = seq // tile
    full = lambda a: pl.BlockSpec(a.shape, lambda i: (0,) * a.ndim)
    tab = pl.BlockSpec((3, tile, LANES), lambda i: (0, i % n_pos, 0))
    row = lambda w: pl.BlockSpec((tile, w), lambda i: (i, 0))
    outs = [(T, MLA_HEADS * LANES), (T, MLA_HEADS * LANES), (T, MLA_HEADS * MLA_V),
            (T, DIL_WIDTH), (T, DIL_WIDTH), (T, DIL_WIDTH)]
    return pl.pallas_call(
        _in_proj_kernel,
        grid=(T // tile,),
        in_specs=[row(D_MODEL), full(g_mix), full(w_in_p), full(g_cq), full(w_uq_p),
                  full(g_ckv), full(w_uk_p), full(w_uv), tab, tab, tab, tab],
        out_specs=[row(s[1]) for s in outs],
        out_shape=[jax.ShapeDtypeStruct(s, BF16) for s in outs[:3]]
                  + [jax.ShapeDtypeStruct(s, F32) for s in outs[3:]],
        compiler_params=_params(("parallel",)),
        name="in_proj",
    )(x2, g_mix, w_in_p, g_cq, w_uq_p, g_ckv, w_uk_p, w_uv, *tabs)


def _mla_attn_kernel(q_ref, k_ref, v_ref, o_ref):
    v = v_ref[...]
    lane = lax.broadcasted_iota(jnp.int32, (q_ref.shape[0], LANES), 1)
    out = None
    for h in range(2):
        sl = slice(h * LANES, (h + 1) * LANES)
        s = _dot_nt(q_ref[:, sl], k_ref[:, sl])
        e = jnp.exp(s - jnp.max(s, axis=-1, keepdims=True))
        p = e.astype(BF16)
        half = p.shape[0] // 2
        o = jnp.concatenate([_dot(p[:half], v), _dot(p[half:], v)], axis=0) / jnp.sum(
            e, axis=-1, keepdims=True)
        out = o if out is None else jnp.where(lane < MLA_V, out, o)
    o_ref[...] = out


def _mla_attn(qm, km, vm, batch, seq, tq):
    q3 = qm.reshape(batch, seq, MLA_HEADS * LANES)
    k3 = km.reshape(batch, seq, MLA_HEADS * LANES)
    v3 = vm.reshape(batch, seq, MLA_HEADS * MLA_V)
    return pl.pallas_call(
        _mla_attn_kernel,
        grid=(batch, MLA_HEADS // 2, seq // tq),
        in_specs=[pl.BlockSpec((None, tq, 2 * LANES), lambda b, p, i: (b, i, p)),
                  pl.BlockSpec((None, seq, 2 * LANES), lambda b, p, i: (b, 0, p)),
                  pl.BlockSpec((None, seq, LANES), lambda b, p, i: (b, 0, p))],
        out_specs=pl.BlockSpec((None, tq, LANES), lambda b, p, i: (b, i, p)),
        out_shape=jax.ShapeDtypeStruct((batch, seq, MLA_HEADS * MLA_V), F32),
        compiler_params=_params(("parallel", "parallel", "parallel")),
        name="mla_attn",
    )(q3, k3, v3)


DIL_Q_BLOCK = 128
DIL_K_WINDOW = 256

def _dil_attn_kernel(q_ref, k_ref, v_ref, o_ref, acc_ref, m_ref, l_ref):
    seq = q_ref.shape[0]
    lane = lax.broadcasted_iota(jnp.int32, (1, 1, LANES), 2)
    head0 = lane < DIL_HEAD_DIM

    for p, (window, dil) in enumerate(DIL_PATTERNS):
        sub_len = seq // dil
        radius = window // (2 * dil)
        qb = min(DIL_Q_BLOCK, sub_len)
        kw = min(DIL_K_WINDOW, sub_len)
        assert kw == sub_len or kw >= qb + 2 * radius
        blocks = sub_len // qb
        rel = (lax.broadcasted_iota(jnp.int32, (qb, kw), 0)
               - lax.broadcasted_iota(jnp.int32, (qb, kw), 1))

        def rows(first, count, dil=dil):
            return pl.ds(first, count) if dil == 1 else pl.ds(first, count, stride=dil)

        q_rows, k_rows, shifts = [], [], []
        for res in range(dil):
            for jb in range(blocks):
                j0 = jb * qb
                k0 = min(max(j0 - (kw - qb) // 2, 0), sub_len - kw)
                q_rows.append(rows(res + dil * j0, qb))
                k_rows.append(rows(res + dil * k0, kw))
                shifts.append(j0 - k0)
        near_by_shift = {s: jnp.abs(rel + s) <= radius for s in sorted(set(shifts))}
        near = jnp.stack([near_by_shift[s] for s in shifts])
        q = jnp.stack([q_ref[r, :] for r in q_rows])
        k = jnp.stack([k_ref[r, :] for r in k_rows]).astype(BF16)
        v = jnp.stack([v_ref[r, :] for r in k_rows]).astype(BF16)

        acc, m_all, l_all = None, None, None
        for h in range(2):
            qh = jnp.where(head0 if h == 0 else ~head0, q, 0.0).astype(BF16)
            s = lax.dot_general(qh, k, (((2,), (2,)), ((0,), (0,))), preferred_element_type=F32)
            s = jnp.where(near, s, MASKED_SCORE)
            m = jnp.max(s, axis=-1, keepdims=True)
            e = jnp.exp(s - m)
            l = jnp.sum(e, axis=-1, keepdims=True)
            a = lax.dot_general(e.astype(BF16), v, (((2,), (1,)), ((0,), (0,))),
                                preferred_element_type=F32)
            if h == 0:
                acc, m_all, l_all = a, m, l
            else:
                acc = jnp.where(head0, acc, a)
                m_all = jnp.where(head0, m_all, m)
                l_all = jnp.where(head0, l_all, l)
        for n, r in enumerate(q_rows):
            acc_ref[p, r, :] = acc[n]
            m_ref[p, r, :] = m_all[n]
            l_ref[p, r, :] = l_all[n]

    m = m_ref[...]
    w = jnp.exp(m - jnp.max(m, axis=0, keepdims=True))
    o_ref[...] = jnp.sum(w * acc_ref[...], axis=0) / jnp.sum(w * l_ref[...], axis=0)


def _dil_attn(qd, kd, vd, batch, seq):
    spec = pl.BlockSpec((None, seq, LANES), lambda b, p: (b, 0, p))
    stats = pltpu.VMEM((len(DIL_PATTERNS), seq, LANES), F32)
    return pl.pallas_call(
        _dil_attn_kernel,
        grid=(batch, DIL_HEADS // 2),
        in_specs=[spec, spec, spec],
        out_specs=spec,
        out_shape=jax.ShapeDtypeStruct((batch, seq, DIL_WIDTH), F32),
        scratch_shapes=[stats, stats, stats],
        compiler_params=_params(("parallel", "parallel")),
        name="dil_attn",
    )(qd.reshape(batch, seq, DIL_WIDTH), kd.reshape(batch, seq, DIL_WIDTH),
      vd.reshape(batch, seq, DIL_WIDTH))


class _Ranked:
    def __init__(self, r8):
        self.r8 = r8
        self.rows = []
        self.lo = jnp.zeros(r8.shape, F32)
        self.hi = jnp.zeros(r8.shape, F32)

    def push(self, row):
        k = len(self.rows)
        self.rows.append(row)
        if k < 8:
            self.lo = jnp.where(self.r8 == k, row, self.lo)
        else:
            self.hi = jnp.where(self.r8 == k - 8, row, self.hi)


KEY_STACK = 4


def _top16(blocks, r8):
    groups = []
    for g in range(len(blocks) // KEY_STACK):
        vs = [blocks[g * KEY_STACK + d] for d in range(KEY_STACK)]
        ids = [r8 + float(8 * (g * KEY_STACK + d)) for d in range(KEY_STACK)]
        for a in (0, 2, 1, 0, 2, 1):
            swap = vs[a + 1] > vs[a]
            vs[a], vs[a + 1] = jnp.where(swap, vs[a + 1], vs[a]), jnp.where(swap, vs[a], vs[a + 1])
            ids[a], ids[a + 1] = jnp.where(swap, ids[a + 1], ids[a]), jnp.where(swap, ids[a], ids[a + 1])
        groups.append((vs, ids))

    tv, ti = _Ranked(r8), _Ranked(r8)
    for k in range(PEER_TOPK):
        m = jnp.max(functools.reduce(jnp.maximum, [vs[0] for vs, _ in groups]),
                    axis=0, keepdims=True)
        low = functools.reduce(jnp.minimum, [jnp.where(vs[0] == m, ids[0], float(PEER_KEYS))
                                            for vs, ids in groups])
        idx = jnp.min(low, axis=0, keepdims=True)
        tv.push(m)
        ti.push(idx)
        live = min(KEY_STACK, PEER_TOPK - 1 - k)
        for vs, ids in groups:
            if live == 0:
                break
            hit = ids[0] == idx
            for d in range(live):
                below = vs[d + 1] if d + 1 < KEY_STACK else -jnp.inf
                vs[d] = jnp.where(hit, below, vs[d])
                if d + 1 < KEY_STACK:
                    ids[d] = jnp.where(hit, ids[d + 1], ids[d])
    return tv, ti


_POS_SHIFT = float(PEER_EXPERTS)


def _pair_top16(v1, i1, v2, i2, r8):
    ninf = -jnp.inf
    big = 3.0e38
    lo_v, lo_k = [], []
    for j in range(PEER_TOPK):
        lists = PEER_TOPK // (j + 1)
        val = v1.lo + v2.rows[j]
        lo_v.append(val if lists >= 8 else jnp.where(r8 < lists, val, ninf))
        lo_k.append((r8 * 16.0 + j) * _POS_SHIFT + i1.lo * PEER_KEYS + i2.rows[j])
    hi_v = v1.hi + v2.rows[0]
    hi_k = (r8 * 16.0 + 128.0) * _POS_SHIFT + i1.hi * PEER_KEYS + i2.rows[0]

    top, sel = _Ranked(r8), _Ranked(r8)
    for k in range(PEER_TOPK):
        m = jnp.max(jnp.maximum(lo_v[0], hi_v), axis=0, keepdims=True)
        kk = jnp.min(jnp.minimum(jnp.where(lo_v[0] == m, lo_k[0], big),
                                 jnp.where(hi_v == m, hi_k, big)), axis=0, keepdims=True)
        top.push(m)
        sel.push(kk)
        live = PEER_TOPK - 1 - k
        if live:
            hit = lo_k[0] == kk
            for d in range(live):
                lo_v[d] = jnp.where(hit, lo_v[d + 1], lo_v[d])
                lo_k[d] = jnp.where(hit, lo_k[d + 1], lo_k[d])
            hi_v = jnp.where(hi_k == kk, ninf, hi_v)
    return top, sel


GATE_PITCH = PEER_KEYS + 8
GATE_BATCH = 8
ROUTE_UNROLL = 2


def _build_gates(slot_ref, first, count, stage_ref, g_ref):
    key_iota = lax.broadcasted_iota(jnp.int32, (1, PEER_KEYS, PEER_SLOTS), 1).astype(F32)
    for t0 in range(0, count, GATE_BATCH):
        rows = pl.ds(first + t0, GATE_BATCH)
        a = slot_ref[0, rows, :][:, None, :]
        b = slot_ref[1, rows, :][:, None, :]
        g = slot_ref[2, rows, :][:, None, :]
        one_a = jnp.where(key_iota == a, 1.0, 0.0).astype(BF16)
        g_at_b = jnp.where(key_iota == b, g, 0.0).astype(BF16)
        gates = lax.dot_general(one_a, g_at_b, (((2,), (2,)), ((0,), (0,))),
                                preferred_element_type=F32)
        for t in range(GATE_BATCH):
            r0 = (t0 + t) * GATE_PITCH
            stage_ref[r0:r0 + PEER_KEYS, :] = gates[t]
    for a in range(PEER_KEYS):
        g_ref[a, pl.ds(first, count), :] = stage_ref[
            pl.ds(a, count, stride=GATE_PITCH), :].astype(BF16)


def _route_kernel(om_ref, od_ref, x_ref, gom_ref, god_ref, wom_ref, wod_ref, gffn_ref,
                  wq_ref, kdt_ref,
                  h_ref, xn_ref, g_ref,
                  sc_ref, ranked_ref, slots_ref, stage_ref):
    step = pl.program_id(0)
    tile = x_ref.shape[0]

    @pl.when(step == 0)
    def _():
        slots_ref[...] = jnp.zeros_like(slots_ref)

    prev_slots = slots_ref.at[(step + 1) % 2]
    nm = _rms(om_ref[...], gom_ref[...]).astype(BF16)
    nd = _rms(od_ref[...], god_ref[...]).astype(BF16)
    h = x_ref[...] + _dot(nm, wom_ref[...]) + _dot(nd, wod_ref[...])
    h_ref[...] = h
    xn = _rms(h, gffn_ref[...]).astype(BF16)
    xn_ref[...] = xn
    q = _dot(xn, wq_ref[...])
    kdt = kdt_ref[...]
    columns = tile // LANES
    units = PEER_HEADS * columns
    for hd in range(PEER_HEADS):
        qh = q[:, hd * LANES:(hd + 1) * LANES].astype(BF16)
        for c in range(columns):
            sc_ref[hd * columns + c] = _dot_nt(kdt, qh[c * LANES:(c + 1) * LANES])

    r8 = lax.broadcasted_iota(jnp.int32, (8, LANES), 0).astype(F32)
    tokens_per_unit = tile // units

    def rank_and_build(unit, carry):
        def key_blocks(first):
            return [sc_ref[unit, first + 8 * d:first + 8 * d + 8, :] for d in range(PEER_KEYS // 8)]

        v1, i1 = _top16(key_blocks(0), r8)
        v2, i2 = _top16(key_blocks(PEER_KEYS), r8)
        top, sel = _pair_top16(v1, i1, v2, i2, r8)
        e_lo = jnp.exp(top.lo - top.rows[0])
        e_hi = jnp.exp(top.hi - top.rows[0])
        inv = 1.0 / (jnp.sum(e_lo, axis=0, keepdims=True) + jnp.sum(e_hi, axis=0, keepdims=True))
        for half, (s, e) in enumerate(((sel.lo, e_lo), (sel.hi, e_hi))):
            rows = slice(half * 8, half * 8 + 8)
            expert = s - jnp.floor(s * (1.0 / _POS_SHIFT)) * _POS_SHIFT
            a = jnp.floor(expert * (1.0 / PEER_KEYS))
            ranked_ref[0, unit, rows, :] = a
            ranked_ref[1, unit, rows, :] = expert - a * PEER_KEYS
            ranked_ref[2, unit, rows, :] = e * inv
        first = pl.multiple_of(unit * tokens_per_unit, tokens_per_unit)
        _build_gates(prev_slots, first, tokens_per_unit, stage_ref, g_ref)
        return carry

    lax.fori_loop(0, units, rank_and_build, 0, unroll=ROUTE_UNROLL)
    slots = slots_ref.at[step % 2]
    for k in range(3):
        for c in range(columns):
            by_slot = jnp.concatenate(
                [ranked_ref[k, hd * columns + c] for hd in range(PEER_HEADS)], axis=0)
            slots[k, c * LANES:(c + 1) * LANES, :] = by_slot.T


def _route(o_mla, o_dil, x2, g_om, g_od, w_om, w_od, g_ffn, w_q, kdt, tile):
    T = x2.shape[0]
    steps = T // tile
    full = lambda a: pl.BlockSpec(a.shape, lambda i: (0,) * a.ndim)
    row = lambda w: pl.BlockSpec((tile, w), lambda i: (jnp.minimum(i, steps - 1), 0))
    units = PEER_HEADS * (tile // LANES)
    return pl.pallas_call(
        _route_kernel,
        grid=(steps + 1,),
        in_specs=[row(o_mla.shape[1]), row(o_dil.shape[1]), row(D_MODEL), full(g_om), full(g_od),
                  full(w_om), full(w_od), full(g_ffn), full(w_q), full(kdt)],
        out_specs=[row(D_MODEL), row(D_MODEL),
                   pl.BlockSpec((PEER_KEYS, tile, PEER_KEYS),
                                lambda i: (0, jnp.maximum(i - 1, 0), 0))],
        out_shape=[jax.ShapeDtypeStruct((T, D_MODEL), F32),
                   jax.ShapeDtypeStruct((T, D_MODEL), BF16),
                   jax.ShapeDtypeStruct((PEER_KEYS, T, PEER_KEYS), BF16)],
        scratch_shapes=[pltpu.VMEM((units, 2 * PEER_KEYS, LANES), F32),
                        pltpu.VMEM((3, units, PEER_TOPK, LANES), F32),
                        pltpu.VMEM((2, 3, tile, PEER_SLOTS), F32),
                        pltpu.VMEM((tile // units * GATE_PITCH, PEER_KEYS), F32)],
        compiler_params=_params(("arbitrary",)),
        name="route",
    )(o_mla, o_dil, x2, g_om, g_od, w_om, w_od, g_ffn, w_q, kdt)


def _peer_kernel(xn_ref, u_ref, v_ref, g_ref, o_ref, *, a_per_step):
    pre = _dot_nt(xn_ref[...], u_ref[...].astype(BF16))
    act = 0.5 * pre * (1.0 + lax.erf(pre * float(1.0 / np.sqrt(2.0))))
    gate = jnp.concatenate([g_ref[a] for a in range(a_per_step)], axis=1)
    y = _dot((act * gate.astype(F32)).astype(BF16), v_ref[...].astype(BF16))

    @pl.when(pl.program_id(1) == 0)
    def _():
        o_ref[...] = y

    @pl.when(pl.program_id(1) != 0)
    def _():
        o_ref[...] += y


def _peer(xn, u_b, v_b, gates, tile, a_per_step):
    T = xn.shape[0]
    et = a_per_step * PEER_KEYS
    return pl.pallas_call(
        functools.partial(_peer_kernel, a_per_step=a_per_step),
        grid=(T // tile, PEER_EXPERTS // et),
        in_specs=[pl.BlockSpec((tile, D_MODEL), lambda i, j: (i, 0)),
                  pl.BlockSpec((et, D_MODEL), lambda i, j: (j, 0)),
                  pl.BlockSpec((et, D_MODEL), lambda i, j: (j, 0)),
                  pl.BlockSpec((a_per_step, tile, PEER_KEYS), lambda i, j: (j, i, 0))],
        out_specs=pl.BlockSpec((tile, D_MODEL), lambda i, j: (i, 0)),
        out_shape=jax.ShapeDtypeStruct((T, D_MODEL), F32),
        compiler_params=_params(("parallel", "arbitrary")),
        name="peer",
    )(xn, u_b, v_b, gates)


def _ple_final_kernel(h_ref, y_ref, p_ref, gple_ref, wg_ref, wp_ref, gfin_ref, o_ref, *, last_layer):
    h = h_ref[...] + y_ref[...]
    gate = jax.nn.sigmoid(_dot(_rms(h, gple_ref[...]).astype(BF16), wg_ref[...]))
    h = h + gate * _dot(p_ref[...].astype(BF16), wp_ref[...])
    o_ref[...] = _rms(h, gfin_ref[...]) if last_layer else h


def _ple_final(h1, y, p2, g_ple, w_gate, w_proj, g_final, tile, last_layer):
    T = h1.shape[0]
    full = lambda a: pl.BlockSpec(a.shape, lambda i: (0,) * a.ndim)
    row = lambda w: pl.BlockSpec((tile, w), lambda i: (i, 0))
    return pl.pallas_call(
        functools.partial(_ple_final_kernel, last_layer=last_layer),
        grid=(T // tile,),
        in_specs=[row(D_MODEL), row(D_MODEL), row(PLE_DIM), full(g_ple), full(w_gate),
                  full(w_proj), full(g_final)],
        out_specs=row(D_MODEL),
        out_shape=jax.ShapeDtypeStruct((T, D_MODEL), F32),
        compiler_params=_params(("parallel",)),
        name="ple_final",
    )(h1, y, p2, g_ple, w_gate, w_proj, g_final)


def _rope_lane_tables(seq, rot_dim, first_lane, period, scale):
    half = rot_dim // 2
    inv = ROPE_THETA ** (-jnp.arange(0, rot_dim, 2, dtype=F32) / rot_dim)
    ang = jnp.arange(seq, dtype=F32)[:, None] * inv[None, :]
    cos, sin = jnp.cos(ang), jnp.sin(ang)
    c = jnp.ones((seq, period), F32)
    c = c.at[:, first_lane:first_lane + half].set(cos).at[:, first_lane + half:first_lane + rot_dim].set(cos)
    s_lo = jnp.zeros((seq, period), F32).at[:, first_lane + half:first_lane + rot_dim].set(sin)
    s_hi = jnp.zeros((seq, period), F32).at[:, first_lane:first_lane + half].set(-sin)
    tabs = jnp.stack([c, s_lo, s_hi]) * scale
    return jnp.tile(tabs, (1, 1, LANES // period))


def _pick_tile(n, want):
    t = min(n, want)
    assert n % t == 0, (n, t)
    return t


def kernel(x, p, g_mix, w_in, g_cq, w_uq, g_ckv, w_ukv, g_out_mla, g_out_dil, w_out, g_ffn,
           w_peer_q, peer_keys1, peer_keys2, peer_u, peer_v, g_ple, w_ple_gate, w_ple_proj, g_final):
    batch, seq, _ = x.shape
    depth = p.shape[0]
    T = batch * seq
    mla_w = MLA_HEADS * MLA_V

    tabs = (_rope_lane_tables(seq, MLA_ROPE, MLA_NOPE, LANES, MLA_QK ** -0.5),
            _rope_lane_tables(seq, MLA_ROPE, MLA_NOPE, LANES, 1.0),
            _rope_lane_tables(seq, DIL_ROT, 0, DIL_HEAD_DIM, DIL_HEAD_DIM ** -0.5),
            _rope_lane_tables(seq, DIL_ROT, 0, DIL_HEAD_DIM, 1.0))

    h = x.reshape(T, D_MODEL)
    for i in range(depth):
        wi = w_in[i]
        o1 = MLA_Q_RANK + MLA_KV_RANK
        k_r_cols = jnp.pad(wi[:, o1:o1 + MLA_ROPE], ((0, 0), (MLA_NOPE, LANES - MLA_QK)))
        w_in_p = jnp.concatenate([wi[:, :o1], k_r_cols, wi[:, o1 + MLA_ROPE:]], axis=1).astype(BF16)
        w_uq_p = jnp.pad(w_uq[i].reshape(MLA_Q_RANK, MLA_HEADS, MLA_QK),
                         ((0, 0), (0, 0), (0, LANES - MLA_QK))).reshape(MLA_Q_RANK, -1).astype(BF16)
        w_ukv3 = w_ukv[i].reshape(MLA_KV_RANK, MLA_HEADS, MLA_NOPE + MLA_V)
        w_uk_p = jnp.pad(w_ukv3[:, :, :MLA_NOPE],
                         ((0, 0), (0, 0), (0, LANES - MLA_NOPE))).reshape(MLA_KV_RANK, -1).astype(BF16)
        w_uv = w_ukv3[:, :, MLA_NOPE:].reshape(MLA_KV_RANK, -1).astype(BF16)
        zeros = jnp.zeros((PEER_KEYS, PEER_HALF), F32)
        kdt = jnp.concatenate([jnp.concatenate([peer_keys1[i], zeros], axis=1),
                               jnp.concatenate([zeros, peer_keys2[i]], axis=1)], axis=0).astype(BF16)

        t_proj = _pick_tile(seq, 512)
        qm, km, vm, qd, kd, vd = _in_proj(
            h, g_mix[i][None], w_in_p, g_cq[i][None], w_uq_p, g_ckv[i][None], w_uk_p, w_uv,
            tabs, seq, t_proj)
        tq = _pick_tile(seq, 512)
        o_mla = _mla_attn(qm, km, vm, batch, seq, tq).reshape(T, mla_w)
        o_dil = _dil_attn(qd, kd, vd, batch, seq).reshape(T, DIL_WIDTH)

        h1, xn, gates = _route(
            o_mla, o_dil, h, g_out_mla[i][None], g_out_dil[i][None],
            w_out[i][:mla_w].astype(BF16), w_out[i][mla_w:].astype(BF16), g_ffn[i][None],
            w_peer_q[i].astype(BF16), kdt, _pick_tile(T, 256))
        y = _peer(xn, peer_u[i], peer_v[i], gates,
                  _pick_tile(T, 1024), 8)
        h = _ple_final(h1, y, p[i].reshape(T, PLE_DIM), g_ple[i][None], w_ple_gate[i].astype(BF16),
                       w_ple_proj[i].astype(BF16), g_final[None], _pick_tile(T, 512),
                       last_layer=(i == depth - 1))
    return h.reshape(batch, seq, D_MODEL)
```

```python
import functools

import numpy as np
import jax
import jax.numpy as jnp
from jax import lax
from jax.experimental import pallas as pl
from jax.experimental.pallas import tpu as pltpu

F32 = jnp.float32
BF16 = jnp.bfloat16

EPS = 1e-6
MASKED_SCORE = -1e30
ROPE_THETA = 500000.0

LANES = 128
D_MODEL = 1024
PLE_DIM = 256
MLA_HEADS = 8
MLA_NOPE = 64
MLA_ROPE = 32
MLA_V = 64
MLA_QK = MLA_NOPE + MLA_ROPE
MLA_Q_RANK = 384
MLA_KV_RANK = 256
DIL_HEADS = 8
DIL_HEAD_DIM = 64
DIL_ROT = 16
DIL_WIDTH = DIL_HEADS * DIL_HEAD_DIM
DIL_PATTERNS = ((128, 1), (512, 4), (2048, 16))
PEER_KEYS = 128
PEER_HEADS = 8
PEER_HALF = 64
PEER_TOPK = 16
PEER_SLOTS = PEER_HEADS * PEER_TOPK
PEER_EXPERTS = PEER_KEYS * PEER_KEYS

VMEM_LIMIT = 48 * 1024 * 1024


def _params(semantics):
    return pltpu.CompilerParams(dimension_semantics=semantics, vmem_limit_bytes=VMEM_LIMIT)


def _rms(x, g):
    return x * lax.rsqrt(jnp.mean(x * x, axis=-1, keepdims=True) + EPS) * g


def _rope(x, tab_ref, half):
    return (x * tab_ref[0]
            + pltpu.roll(x, half, 1) * tab_ref[1]
            + pltpu.roll(x, LANES - half, 1) * tab_ref[2])


def _dot(a, b):
    return jnp.dot(a, b, preferred_element_type=F32)


def _dot_nt(a, b):
    return lax.dot_general(a, b, (((1,), (1,)), ((), ())), preferred_element_type=F32)


def _in_proj_kernel(x_ref, gmix_ref, win_ref, gcq_ref, wuq_ref, gckv_ref, wuk_ref, wuv_ref,
                    tmq_ref, tmk_ref, tdq_ref, tdk_ref,
                    qm_ref, km_ref, vm_ref, qd_ref, kd_ref, vd_ref):
    hn = _rms(x_ref[...], gmix_ref[...]).astype(BF16)
    y = _dot(hn, win_ref[...])
    o = 0
    c_q = y[:, o:o + MLA_Q_RANK]; o += MLA_Q_RANK
    c_kv = y[:, o:o + MLA_KV_RANK]; o += MLA_KV_RANK
    k_r = y[:, o:o + LANES]; o += LANES
    q_d = y[:, o:o + DIL_WIDTH]; o += DIL_WIDTH
    k_d = y[:, o:o + DIL_WIDTH]; o += DIL_WIDTH
    v_d = y[:, o:o + DIL_WIDTH]

    q = _dot(_rms(c_q, gcq_ref[...]).astype(BF16), wuq_ref[...])
    ckvn = _rms(c_kv, gckv_ref[...]).astype(BF16)
    k_n = _dot(ckvn, wuk_ref[...])
    vm_ref[...] = _dot(ckvn, wuv_ref[...]).astype(BF16)
    k_rope = _rope(k_r, tmk_ref, MLA_ROPE // 2)
    for h in range(MLA_HEADS):
        sl = slice(h * LANES, (h + 1) * LANES)
        qm_ref[:, sl] = _rope(q[:, sl], tmq_ref, MLA_ROPE // 2).astype(BF16)
        km_ref[:, sl] = (k_n[:, sl] + k_rope).astype(BF16)
    for c in range(DIL_WIDTH // LANES):
        sl = slice(c * LANES, (c + 1) * LANES)
        qd_ref[:, sl] = _rope(q_d[:, sl], tdq_ref, DIL_ROT // 2)
        kd_ref[:, sl] = _rope(k_d[:, sl], tdk_ref, DIL_ROT // 2)
    vd_ref[...] = v_d


def _in_proj(x2, g_mix, w_in_p, g_cq, w_uq_p, g_ckv, w_uk_p, w_uv, tabs, seq, tile):
    T = x2.shape[0]
    n_pos = seq // tile
    full = lambda a: pl.BlockSpec(a.shape, lambda i: (0,) * a.ndim)
    tab = pl.BlockSpec((3, tile, LANES), lambda i: (0, i % n_pos, 0))
    row = lambda w: pl.BlockSpec((tile, w), lambda i: (i, 0))
    outs = [(T, MLA_HEADS * LANES), (T, MLA_HEADS * LANES), (T, MLA_HEADS * MLA_V),
            (T, DIL_WIDTH), (T, DIL_WIDTH), (T, DIL_WIDTH)]
    return pl.pallas_call(
        _in_proj_kernel,
        grid=(T // tile,),
        in_specs=[row(D_MODEL), full(g_mix), full(w_in_p), full(g_cq), full(w_uq_p),
                  full(g_ckv), full(w_uk_p), full(w_uv), tab, tab, tab, tab],
        out_specs=[row(s[1]) for s in outs],
        out_shape=[jax.ShapeDtypeStruct(s, BF16) for s in outs[:3]]
                  + [jax.ShapeDtypeStruct(s, F32) for s in outs[3:]],
        compiler_params=_params(("parallel",)),
        name="in_proj",
    )(x2, g_mix, w_in_p, g_cq, w_uq_p, g_ckv, w_uk_p, w_uv, *tabs)


def _mla_attn_kernel(q_ref, k_ref, v_ref, o_ref):
    v = v_ref[...]
    lane = lax.broadcasted_iota(jnp.int32, (q_ref.shape[0], LANES), 1)
    out = None
    for h in range(2):
        sl = slice(h * LANES, (h + 1) * LANES)
        s = _dot_nt(q_ref[:, sl], k_ref[:, sl])
        e = jnp.exp(s - jnp.max(s, axis=-1, keepdims=True))
        p = e.astype(BF16)
        half = p.shape[0] // 2
        o = jnp.concatenate([_dot(p[:half], v), _dot(p[half:], v)], axis=0) / jnp.sum(
            e, axis=-1, keepdims=True)
        out = o if out is None else jnp.where(lane < MLA_V, out, o)
    o_ref[...] = out


def _mla_attn(qm, km, vm, batch, seq, tq):
    q3 = qm.reshape(batch, seq, MLA_HEADS * LANES)
    k3 = km.reshape(batch, seq, MLA_HEADS * LANES)
    v3 = vm.reshape(batch, seq, MLA_HEADS * MLA_V)
    return pl.pallas_call(
        _mla_attn_kernel,
        grid=(batch, MLA_HEADS // 2, seq // tq),
        in_specs=[pl.BlockSpec((None, tq, 2 * LANES), lambda b, p, i: (b, i, p)),
                  pl.BlockSpec((None, seq, 2 * LANES), lambda b, p, i: (b, 0, p)),
                  pl.BlockSpec((None, seq, LANES), lambda b, p, i: (b, 0, p))],
        out_specs=pl.BlockSpec((None, tq, LANES), lambda b, p, i: (b, i, p)),
        out_shape=jax.ShapeDtypeStruct((batch, seq, MLA_HEADS * MLA_V), F32),
        compiler_params=_params(("parallel", "parallel", "parallel")),
        name="mla_attn",
    )(q3, k3, v3)


DIL_Q_BLOCK = 128
DIL_K_WINDOW = 256

def _dil_attn_kernel(q_ref, k_ref, v_ref, o_ref, acc_ref, m_ref, l_ref):
    seq = q_ref.shape[0]
    lane = lax.broadcasted_iota(jnp.int32, (1, 1, LANES), 2)
    head0 = lane < DIL_HEAD_DIM

    for p, (window, dil) in enumerate(DIL_PATTERNS):
        sub_len = seq // dil
        radius = window // (2 * dil)
        qb = min(DIL_Q_BLOCK, sub_len)
        kw = min(DIL_K_WINDOW, sub_len)
        assert kw == sub_len or kw >= qb + 2 * radius
        blocks = sub_len // qb
        rel = (lax.broadcasted_iota(jnp.int32, (qb, kw), 0)
               - lax.broadcasted_iota(jnp.int32, (qb, kw), 1))

        def rows(first, count, dil=dil):
            return pl.ds(first, count) if dil == 1 else pl.ds(first, count, stride=dil)

        q_rows, k_rows, shifts = [], [], []
        for res in range(dil):
            for jb in range(blocks):
                j0 = jb * qb
                k0 = min(max(j0 - (kw - qb) // 2, 0), sub_len - kw)
                q_rows.append(rows(res + dil * j0, qb))
                k_rows.append(rows(res + dil * k0, kw))
                shifts.append(j0 - k0)
        near_by_shift = {s: jnp.abs(rel + s) <= radius for s in sorted(set(shifts))}
        q =jnp.stack([q_ref[r, :] for r in q_rows])
        k = jnp.stack([k_ref[r, :] for r in k_rows]).astype(BF16)
        v = jnp.stack([v_ref[r, :] for r in k_rows]).astype(BF16)

        acc, m_all, l_all = None, None, None
        for h in range(2):
            qh = jnp.where(head0 if h == 0 else ~head0, q, 0.0).astype(BF16)
            s = lax.dot_general(qh, k, (((2,), (2,)), ((0,), (0,))), preferred_element_type=F32)
            s = jnp.stack([jnp.where(near_by_shift[shift], s[n], MASKED_SCORE)
                           for n, shift in enumerate(shifts)])
            m = jnp.max(s, axis=-1, keepdims=True)
            e = jnp.exp(s - m)
            l = jnp.sum(e, axis=-1, keepdims=True)
            a = lax.dot_general(e.astype(BF16), v, (((2,), (1,)), ((0,), (0,))),
                                preferred_element_type=F32)
            if h == 0:
                acc, m_all, l_all = a, m, l
            else:
                acc = jnp.where(head0, acc, a)
                m_all = jnp.where(head0, m_all, m)
                l_all = jnp.where(head0, l_all, l)
        for n, r in enumerate(q_rows):
            acc_ref[p, r, :] = acc[n]
            m_ref[p, r, :] = m_all[n]
            l_ref[p, r, :] = l_all[n]

    m = m_ref[...]
    w = jnp.exp(m - jnp.max(m, axis=0, keepdims=True))
    o_ref[...] = jnp.sum(w * acc_ref[...], axis=0) / jnp.sum(w * l_ref[...], axis=0)


def _dil_attn(qd, kd, vd, batch, seq):
    spec = pl.BlockSpec((None, seq, LANES), lambda b, p: (b, 0, p))
    stats = pltpu.VMEM((len(DIL_PATTERNS), seq, LANES), F32)
    return pl.pallas_call(
        _dil_attn_kernel,
        grid=(batch, DIL_HEADS // 2),
        in_specs=[spec, spec, spec],
        out_specs=spec,
        out_shape=jax.ShapeDtypeStruct((batch, seq, DIL_WIDTH), F32),
        scratch_shapes=[stats, stats, stats],
        compiler_params=_params(("parallel", "parallel")),
        name="dil_attn",
    )(qd.reshape(batch, seq, DIL_WIDTH), kd.reshape(batch, seq, DIL_WIDTH),
      vd.reshape(batch, seq, DIL_WIDTH))


class _Ranked:
    def __init__(self, r8):
        self.r8 = r8
        self.rows = []
        self.lo = jnp.zeros(r8.shape, F32)
        self.hi = jnp.zeros(r8.shape, F32)

    def push(self, row):
        k = len(self.rows)
        self.rows.append(row)
        if k < 8:
            self.lo = jnp.where(self.r8 == k, row, self.lo)
        else:
            self.hi = jnp.where(self.r8 == k - 8, row, self.hi)


KEY_STACK = 4


def _top16(blocks, r8):
    groups = []
    for g in range(len(blocks) // KEY_STACK):
        vs = [blocks[g * KEY_STACK + d] for d in range(KEY_STACK)]
        ids = [r8 + float(8 * (g * KEY_STACK + d)) for d in range(KEY_STACK)]
        for a in (0, 2, 1, 0, 2, 1):
            swap = vs[a + 1] > vs[a]
            vs[a], vs[a + 1] = jnp.where(swap, vs[a + 1], vs[a]), jnp.where(swap, vs[a], vs[a + 1])
            ids[a], ids[a + 1] = jnp.where(swap, ids[a + 1], ids[a]), jnp.where(swap, ids[a], ids[a + 1])
        groups.append((vs, ids))

    tv, ti = _Ranked(r8), _Ranked(r8)
    for k in range(PEER_TOPK):
        m = jnp.max(functools.reduce(jnp.maximum, [vs[0] for vs, _ in groups]),
                    axis=0, keepdims=True)
        low = functools.reduce(jnp.minimum, [jnp.where(vs[0] == m, ids[0], float(PEER_KEYS))
                                            for vs, ids in groups])
        idx = jnp.min(low, axis=0, keepdims=True)
        tv.push(m)
        ti.push(idx)
        live = min(KEY_STACK, PEER_TOPK - 1 - k)
        for vs, ids in groups:
            if live == 0:
                break
            hit = ids[0] == idx
            for d in range(live):
                below = vs[d + 1] if d + 1 < KEY_STACK else -jnp.inf
                vs[d] = jnp.where(hit, below, vs[d])
                if d + 1 < KEY_STACK:
                    ids[d] = jnp.where(hit, ids[d + 1], ids[d])
    return tv, ti


_POS_SHIFT = float(PEER_EXPERTS)


def _pair_top16(v1, i1, v2, i2, r8):
    ninf = -jnp.inf
    big = 3.0e38
    lo_v, lo_k = [], []
    for j in range(PEER_TOPK):
        lists = PEER_TOPK // (j + 1)
        val = v1.lo + v2.rows[j]
        lo_v.append(val if lists >= 8 else jnp.where(r8 < lists, val, ninf))
        lo_k.append((r8 * 16.0 + j) * _POS_SHIFT + i1.lo * PEER_KEYS + i2.rows[j])
    hi_v = v1.hi + v2.rows[0]
    hi_k = (r8 * 16.0 + 128.0) * _POS_SHIFT + i1.hi * PEER_KEYS + i2.rows[0]

    top, sel = _Ranked(r8), _Ranked(r8)
    for k in range(PEER_TOPK):
        m = jnp.max(jnp.maximum(lo_v[0], hi_v), axis=0, keepdims=True)
        kk = jnp.min(jnp.minimum(jnp.where(lo_v[0] == m, lo_k[0], big),
                                 jnp.where(hi_v == m, hi_k, big)), axis=0, keepdims=True)
        top.push(m)
        sel.push(kk)
        live = PEER_TOPK - 1 - k
        if live:
            hit = lo_k[0] == kk
            for d in range(live):
                lo_v[d] = jnp.where(hit, lo_v[d + 1], lo_v[d])
                lo_k[d] = jnp.where(hit, lo_k[d + 1], lo_k[d])
            hi_v = jnp.where(hi_k == kk, ninf, hi_v)
    return top, sel


GATE_PITCH = PEER_KEYS + 8
GATE_BATCH = 8
ROUTE_UNROLL = 2


def _build_gates(slot_ref, first, count, stage_ref, g_ref):
    key_iota = lax.broadcasted_iota(jnp.int32, (1, PEER_KEYS, PEER_SLOTS), 1).astype(F32)
    for t0 in range(0, count, GATE_BATCH):
        rows = pl.ds(first + t0, GATE_BATCH)
        a = slot_ref[0, rows, :][:, None, :]
        b = slot_ref[1, rows, :][:, None, :]
        g = slot_ref[2, rows, :][:, None, :]
        one_a = jnp.where(key_iota == a, 1.0, 0.0).astype(BF16)
        g_at_b = jnp.where(key_iota == b, g, 0.0).astype(BF16)
        gates = lax.dot_general(one_a, g_at_b, (((2,), (2,)), ((0,), (0,))),
                                preferred_element_type=F32)
        for t in range(GATE_BATCH):
            r0 = (t0 + t) * GATE_PITCH
            stage_ref[r0:r0 + PEER_KEYS, :] = gates[t]
    for a in range(PEER_KEYS):
        g_ref[a, pl.ds(first, count), :] = stage_ref[
            pl.ds(a, count, stride=GATE_PITCH), :].astype(BF16)


def _route_kernel(om_ref, od_ref, x_ref, gom_ref, god_ref, wom_ref, wod_ref, gffn_ref,
                  wq_ref, kdt_ref,
                  h_ref, xn_ref, g_ref,
                  sc_ref, ranked_ref, slots_ref, stage_ref):
    step = pl.program_id(0)
    tile = x_ref.shape[0]

    @pl.when(step == 0)
    def _():
        slots_ref[...] = jnp.zeros_like(slots_ref)

    prev_slots = slots_ref.at[(step + 1) % 2]
    nm = _rms(om_ref[...], gom_ref[...]).astype(BF16)
    nd = _rms(od_ref[...], god_ref[...]).astype(BF16)
    h = x_ref[...] + _dot(nm, wom_ref[...]) + _dot(nd, wod_ref[...])
    h_ref[...] = h
    xn = _rms(h, gffn_ref[...]).astype(BF16)
    xn_ref[...] = xn
    q = _dot(xn, wq_ref[...])
    kdt = kdt_ref[...]
    columns = tile // LANES
    units = PEER_HEADS * columns
    for hd in range(PEER_HEADS):
        qh = q[:, hd * LANES:(hd + 1) * LANES].astype(BF16)
        for c in range(columns):
            sc_ref[hd * columns + c] = _dot_nt(kdt, qh[c * LANES:(c + 1) * LANES])

    r8 = lax.broadcasted_iota(jnp.int32, (8, LANES), 0).astype(F32)
    tokens_per_unit = tile // units

    def rank_and_build(unit, carry):
        def key_blocks(first):
            return [sc_ref[unit, first + 8 * d:first + 8 * d + 8, :] for d in range(PEER_KEYS // 8)]

        v1, i1 = _top16(key_blocks(0), r8)
        v2, i2 = _top16(key_blocks(PEER_KEYS), r8)
        top, sel = _pair_top16(v1, i1, v2, i2, r8)
        e_lo = jnp.exp(top.lo - top.rows[0])
        e_hi = jnp.exp(top.hi - top.rows[0])
        inv = 1.0 / (jnp.sum(e_lo, axis=0, keepdims=True) + jnp.sum(e_hi, axis=0, keepdims=True))
        for half, (s, e) in enumerate(((sel.lo, e_lo), (sel.hi, e_hi))):
            rows = slice(half * 8, half * 8 + 8)
            expert = s - jnp.floor(s * (1.0 / _POS_SHIFT)) * _POS_SHIFT
            a = jnp.floor(expert * (1.0 / PEER_KEYS))
            ranked_ref[0, unit, rows, :] = a
            ranked_ref[1, unit, rows, :] = expert - a * PEER_KEYS
            ranked_ref[2, unit, rows, :] = e * inv
        first = pl.multiple_of(unit * tokens_per_unit, tokens_per_unit)
        _build_gates(prev_slots, first, tokens_per_unit, stage_ref, g_ref)
        return carry

    lax.fori_loop(0, units, rank_and_build, 0, unroll=ROUTE_UNROLL)
    slots = slots_ref.at[step % 2]
    for k in range(3):
        for c in range(columns):
            by_slot = jnp.concatenate(
                [ranked_ref[k, hd * columns + c] for hd in range(PEER_HEADS)], axis=0)
            slots[k, c * LANES:(c + 1) * LANES, :] = by_slot.T


def _route(o_mla, o_dil, x2, g_om, g_od, w_om, w_od, g_ffn, w_q, kdt, tile):
    T = x2.shape[0]
    steps = T // tile
    full = lambda a: pl.BlockSpec(a.shape, lambda i: (0,) * a.ndim)
    row = lambda w: pl.BlockSpec((tile, w), lambda i: (jnp.minimum(i, steps - 1), 0))
    units = PEER_HEADS * (tile // LANES)
    return pl.pallas_call(
        _route_kernel,
        grid=(steps + 1,),
        in_specs=[row(o_mla.shape[1]), row(o_dil.shape[1]), row(D_MODEL), full(g_om), full(g_od),
                  full(w_om), full(w_od), full(g_ffn), full(w_q), full(kdt)],
        out_specs=[row(D_MODEL), row(D_MODEL),
                   pl.BlockSpec((PEER_KEYS, tile, PEER_KEYS),
                                lambda i: (0, jnp.maximum(i - 1, 0), 0))],
        out_shape=[jax.ShapeDtypeStruct((T, D_MODEL), F32),
                   jax.ShapeDtypeStruct((T, D_MODEL), BF16),
                   jax.ShapeDtypeStruct((PEER_KEYS, T, PEER_KEYS), BF16)],
        scratch_shapes=[pltpu.VMEM((units, 2 * PEER_KEYS, LANES), F32),
                        pltpu.VMEM((3, units, PEER_TOPK, LANES), F32),
                        pltpu.VMEM((2, 3, tile, PEER_SLOTS), F32),
                        pltpu.VMEM((tile // units * GATE_PITCH, PEER_KEYS), F32)],
        compiler_params=_params(("arbitrary",)),
        name="route",
    )(o_mla, o_dil, x2, g_om, g_od, w_om, w_od, g_ffn, w_q, kdt)


def _peer_kernel(xn_ref, u_ref, v_ref, g_ref, o_ref, *, a_per_step):
    pre = _dot_nt(xn_ref[...], u_ref[...].astype(BF16))
    act = 0.5 * pre * (1.0 + lax.erf(pre * float(1.0 / np.sqrt(2.0))))
    gate = jnp.concatenate([g_ref[a] for a in range(a_per_step)], axis=1)
    y = _dot((act * gate.astype(F32)).astype(BF16), v_ref[...].astype(BF16))

    @pl.when(pl.program_id(1) == 0)
    def _():
        o_ref[...] = y

    @pl.when(pl.program_id(1) != 0)
    def _():
        o_ref[...] += y


def _peer(xn, u_b, v_b, gates, tile, a_per_step):
    T = xn.shape[0]
    et = a_per_step * PEER_KEYS
    return pl.pallas_call(
        functools.partial(_peer_kernel, a_per_step=a_per_step),
        grid=(T // tile, PEER_EXPERTS // et),
        in_specs=[pl.BlockSpec((tile, D_MODEL), lambda i, j: (i, 0)),
                  pl.BlockSpec((et, D_MODEL), lambda i, j: (j, 0)),
                  pl.BlockSpec((et, D_MODEL), lambda i, j: (j, 0)),
                  pl.BlockSpec((a_per_step, tile, PEER_KEYS), lambda i, j: (j, i, 0))],
        out_specs=pl.BlockSpec((tile, D_MODEL), lambda i, j: (i, 0)),
        out_shape=jax.ShapeDtypeStruct((T, D_MODEL), F32),
        compiler_params=_params(("parallel", "arbitrary")),
        name="peer",
    )(xn, u_b, v_b, gates)


def _ple_final_kernel(h_ref, y_ref, p_ref, gple_ref, wg_ref, wp_ref, gfin_ref, o_ref, *, last_layer):
    h = h_ref[...] + y_ref[...]
    gate = jax.nn.sigmoid(_dot(_rms(h, gple_ref[...]).astype(BF16), wg_ref[...]))
    h = h + gate * _dot(p_ref[...].astype(BF16), wp_ref[...])
    o_ref[...] = _rms(h, gfin_ref[...]) if last_layer else h


def _ple_final(h1, y, p2, g_ple, w_gate, w_proj, g_final, tile, last_layer):
    T = h1.shape[0]
    full = lambda a: pl.BlockSpec(a.shape, lambda i: (0,) * a.ndim)
    row = lambda w: pl.BlockSpec((tile, w), lambda i: (i, 0))
    return pl.pallas_call(
        functools.partial(_ple_final_kernel, last_layer=last_layer),
        grid=(T // tile,),
        in_specs=[row(D_MODEL), row(D_MODEL), row(PLE_DIM), full(g_ple), full(w_gate),
                  full(w_proj), full(g_final)],
        out_specs=row(D_MODEL),
        out_shape=jax.ShapeDtypeStruct((T, D_MODEL), F32),
        compiler_params=_params(("parallel",)),
        name="ple_final",
    )(h1, y, p2, g_ple, w_gate, w_proj, g_final)


def _rope_lane_tables(seq, rot_dim, first_lane, period, scale):
    half = rot_dim // 2
    inv = ROPE_THETA ** (-jnp.arange(0, rot_dim, 2, dtype=F32) / rot_dim)
    ang = jnp.arange(seq, dtype=F32)[:, None] * inv[None, :]
    cos, sin = jnp.cos(ang), jnp.sin(ang)
    c = jnp.ones((seq, period), F32)
    c = c.at[:, first_lane:first_lane + half].set(cos).at[:, first_lane + half:first_lane + rot_dim].set(cos)
    s_lo = jnp.zeros((seq, period), F32).at[:, first_lane + half:first_lane + rot_dim].set(sin)
    s_hi = jnp.zeros((seq, period), F32).at[:, first_lane:first_lane + half].set(-sin)
    tabs = jnp.stack([c, s_lo, s_hi]) * scale
    return jnp.tile(tabs, (1, 1, LANES // period))


def _pick_tile(n, want):
    t = min(n, want)
    assert n % t == 0, (n, t)
    return t


def kernel(x, p, g_mix, w_in, g_cq, w_uq, g_ckv, w_ukv, g_out_mla, g_out_dil, w_out, g_ffn,
           w_peer_q, peer_keys1, peer_keys2, peer_u, peer_v, g_ple, w_ple_gate, w_ple_proj, g_final):
    batch, seq, _ = x.shape
    depth = p.shape[0]
    T = batch * seq
    mla_w = MLA_HEADS * MLA_V

    tabs = (_rope_lane_tables(seq, MLA_ROPE, MLA_NOPE, LANES, MLA_QK ** -0.5),
            _rope_lane_tables(seq, MLA_ROPE, MLA_NOPE, LANES, 1.0),
            _rope_lane_tables(seq, DIL_ROT, 0, DIL_HEAD_DIM, DIL_HEAD_DIM ** -0.5),
            _rope_lane_tables(seq, DIL_ROT, 0, DIL_HEAD_DIM, 1.0))

    h = x.reshape(T, D_MODEL)
    for i in range(depth):
        wi = w_in[i]
        o1 = MLA_Q_RANK + MLA_KV_RANK
        k_r_cols = jnp.pad(wi[:, o1:o1 + MLA_ROPE], ((0, 0), (MLA_NOPE, LANES - MLA_QK)))
        w_in_p = jnp.concatenate([wi[:, :o1], k_r_cols, wi[:, o1 + MLA_ROPE:]], axis=1).astype(BF16)
        w_uq_p = jnp.pad(w_uq[i].reshape(MLA_Q_RANK, MLA_HEADS, MLA_QK),
                         ((0, 0), (0, 0), (0, LANES - MLA_QK))).reshape(MLA_Q_RANK, -1).astype(BF16)
        w_ukv3 = w_ukv[i].reshape(MLA_KV_RANK, MLA_HEADS, MLA_NOPE + MLA_V)
        w_uk_p = jnp.pad(w_ukv3[:, :, :MLA_NOPE],
                         ((0, 0), (0, 0), (0, LANES - MLA_NOPE))).reshape(MLA_KV_RANK, -1).astype(BF16)
        w_uv = w_ukv3[:, :, MLA_NOPE:].reshape(MLA_KV_RANK, -1).astype(BF16)
        zeros = jnp.zeros((PEER_KEYS, PEER_HALF), F32)
        kdt = jnp.concatenate([jnp.concatenate([peer_keys1[i], zeros], axis=1),
                               jnp.concatenate([zeros, peer_keys2[i]], axis=1)], axis=0).astype(BF16)

        t_proj = _pick_tile(seq, 512)
        qm, km, vm, qd, kd, vd = _in_proj(
            h, g_mix[i][None], w_in_p, g_cq[i][None], w_uq_p, g_ckv[i][None], w_uk_p, w_uv,
            tabs, seq, t_proj)
        tq = _pick_tile(seq, 1024)
        o_mla = _mla_attn(qm, km, vm, batch, seq, tq).reshape(T, mla_w)
        o_dil = _dil_attn(qd, kd, vd, batch, seq).reshape(T, DIL_WIDTH)

        h1, xn, gates = _route(
            o_mla, o_dil, h, g_out_mla[i][None], g_out_dil[i][None],
            w_out[i][:mla_w].astype(BF16), w_out[i][mla_w:].astype(BF16), g_ffn[i][None],
            w_peer_q[i].astype(BF16), kdt, _pick_tile(T, 256))
        y = _peer(xn, peer_u[i], peer_v[i], gates,
                  _pick_tile(T, 1024), 8)
        h = _ple_final(h1, y, p[i].reshape(T, PLE_DIM), g_ple[i][None], w_ple_gate[i].astype(BF16),
                       w_ple_proj[i].astype(BF16), g_final[None], _pick_tile(T, 512),
                       last_layer=(i == depth - 1))
    return h.reshape(batch, seq, D_MODEL)
```

```python
import functools

import numpy as np
import jax
import jax.numpy as jnp
from jax import lax
from jax.experimental import pallas as pl
from jax.experimental.pallas import tpu as pltpu

F32 = jnp.float32
BF16 = jnp.bfloat16

EPS = 1e-6
MASKED_SCORE = -1e30
ROPE_THETA = 500000.0

LANES = 128
D_MODEL = 1024
PLE_DIM = 256
MLA_HEADS = 8
MLA_NOPE = 64
MLA_ROPE = 32
MLA_V = 64
MLA_QK = MLA_NOPE + MLA_ROPE
MLA_Q_RANK = 384
MLA_KV_RANK = 256
DIL_HEADS = 8
DIL_HEAD_DIM = 64
DIL_ROT = 16
DIL_WIDTH = DIL_HEADS * DIL_HEAD_DIM
DIL_PATTERNS = ((128, 1), (512, 4), (2048, 16))
PEER_KEYS = 128
PEER_HEADS = 8
PEER_HALF = 64
PEER_TOPK = 16
PEER_SLOTS = PEER_HEADS * PEER_TOPK
PEER_EXPERTS = PEER_KEYS * PEER_KEYS

VMEM_LIMIT = 48 * 1024 * 1024

PROJ_TILE = 512
MLA_Q_TILE = 1024
MLA_PV_CHUNKS = 2
ROUTE_TILE = 256
PEER_TILE = 1024
PEER_A_PER_STEP = 8


def _params(semantics):
    return pltpu.CompilerParams(dimension_semantics=semantics, vmem_limit_bytes=VMEM_LIMIT)


def _rms(x, g):
    return x * lax.rsqrt(jnp.mean(x * x, axis=-1, keepdims=True) + EPS) * g


def _rope(x, tab_ref, half):
    return (x * tab_ref[0]
            + pltpu.roll(x, half, 1) * tab_ref[1]
            + pltpu.roll(x, LANES - half, 1) * tab_ref[2])


def _dot(a, b):
    return jnp.dot(a, b, preferred_element_type=F32)


def _dot_nt(a, b):
    return lax.dot_general(a, b, (((1,), (1,)), ((), ())), preferred_element_type=F32)


def _in_proj_kernel(x_ref, gmix_ref, win_ref, gcq_ref, wuq_ref, gckv_ref, wuk_ref, wuv_ref,
                    tmq_ref, tmk_ref, tdq_ref, tdk_ref,
                    qm_ref, km_ref, vm_ref, qd_ref, kd_ref, vd_ref):
    hn = _rms(x_ref[...], gmix_ref[...]).astype(BF16)
    y = _dot(hn, win_ref[...])
    o = 0
    c_q = y[:, o:o + MLA_Q_RANK]; o += MLA_Q_RANK
    c_kv = y[:, o:o + MLA_KV_RANK]; o += MLA_KV_RANK
    k_r = y[:, o:o + LANES]; o += LANES
    q_d = y[:, o:o + DIL_WIDTH]; o += DIL_WIDTH
    k_d = y[:, o:o + DIL_WIDTH]; o += DIL_WIDTH
    v_d = y[:, o:o + DIL_WIDTH]

    q = _dot(_rms(c_q, gcq_ref[...]).astype(BF16), wuq_ref[...])
    ckvn = _rms(c_kv, gckv_ref[...]).astype(BF16)
    k_n = _dot(ckvn, wuk_ref[...])
    vm_ref[...] = _dot(ckvn, wuv_ref[...]).astype(BF16)
    k_rope = _rope(k_r, tmk_ref, MLA_ROPE // 2)
    for h in range(MLA_HEADS):
        sl = slice(h * LANES, (h + 1) * LANES)
        qm_ref[:, sl] = _rope(q[:, sl], tmq_ref, MLA_ROPE // 2).astype(BF16)
        km_ref[:, sl] = (k_n[:, sl] + k_rope).astype(BF16)
    for c in range(DIL_WIDTH // LANES):
        sl = slice(c * LANES, (c + 1) * LANES)
        qd_ref[:, sl] = _rope(q_d[:, sl], tdq_ref, DIL_ROT // 2)
        kd_ref[:, sl] = _rope(k_d[:, sl], tdk_ref, DIL_ROT // 2)
    vd_ref[...] = v_d


def _in_proj(x2, g_mix, w_in_p, g_cq, w_uq_p, g_ckv, w_uk_p, w_uv, tabs, seq, tile):
    T = x2.shape[0]
    n_pos = seq // tile
    full = lambda a: pl.BlockSpec(a.shape, lambda i: (0,) * a.ndim)
    tab = pl.BlockSpec((3, tile, LANES), lambda i: (0, i % n_pos, 0))
    row = lambda w: pl.BlockSpec((tile, w), lambda i: (i, 0))
    outs = [(T, MLA_HEADS * LANES), (T, MLA_HEADS * LANES), (T, MLA_HEADS * MLA_V),
            (T, DIL_WIDTH), (T, DIL_WIDTH), (T, DIL_WIDTH)]
    return pl.pallas_call(
        _in_proj_kernel,
        grid=(T // tile,),
        in_specs=[row(D_MODEL), full(g_mix), full(w_in_p), full(g_cq), full(w_uq_p),
                  full(g_ckv), full(w_uk_p), full(w_uv), tab, tab, tab, tab],
        out_specs=[row(s[1]) for s in outs],
        out_shape=[jax.ShapeDtypeStruct(s, BF16) for s in outs[:3]]
                  + [jax.ShapeDtypeStruct(s, F32) for s in outs[3:]],
        compiler_params=_params(("parallel",)),
        name="in_proj",
    )(x2, g_mix, w_in_p, g_cq, w_uq_p, g_ckv, w_uk_p, w_uv, *tabs)


def _mla_attn_kernel(q_ref, k_ref, v_ref, o_ref):
    v = v_ref[...]
    lane = lax.broadcasted_iota(jnp.int32, (q_ref.shape[0], LANES), 1)
    out = None
    for h in range(2):
        sl = slice(h * LANES, (h + 1) * LANES)
        s = _dot_nt(q_ref[:, sl], k_ref[:, sl])
        e = jnp.exp(s - jnp.max(s, axis=-1, keepdims=True))
        p = e.astype(BF16)
        rows = p.shape[0] // MLA_PV_CHUNKS
        o = jnp.concatenate([_dot(p[c * rows:(c + 1) * rows], v) for c in range(MLA_PV_CHUNKS)],
                            axis=0) / jnp.sum(e, axis=-1, keepdims=True)
        out = o if out is None else jnp.where(lane < MLA_V, out, o)
    o_ref[...] = out


def _mla_attn(qm, km, vm, batch, seq, tq):
    q3 = qm.reshape(batch, seq, MLA_HEADS * LANES)
    k3 = km.reshape(batch, seq, MLA_HEADS * LANES)
    v3 = vm.reshape(batch, seq, MLA_HEADS * MLA_V)
    return pl.pallas_call(
        _mla_attn_kernel,
        grid=(batch, MLA_HEADS // 2, seq // tq),
        in_specs=[pl.BlockSpec((None, tq, 2 * LANES), lambda b, p, i: (b, i, p)),
                  pl.BlockSpec((None, seq, 2 * LANES), lambda b, p, i: (b, 0, p)),
                  pl.BlockSpec((None, seq, LANES), lambda b, p, i: (b, 0, p))],
        out_specs=pl.BlockSpec((None, tq, LANES), lambda b, p, i: (b, i, p)),
        out_shape=jax.ShapeDtypeStruct((batch, seq, MLA_HEADS * MLA_V), F32),
        compiler_params=_params(("parallel", "parallel", "parallel")),
        name="mla_attn",
    )(q3, k3, v3)


DIL_Q_BLOCK = 128
DIL_K_WINDOW = 256


def _dil_attn_kernel(q_ref, k_ref, v_ref, o_ref, acc_ref, m_ref, l_ref):
    seq = q_ref.shape[0]
    lane = lax.broadcasted_iota(jnp.int32, (1, 1, LANES), 2)
    head0 = lane < DIL_HEAD_DIM

    for p, (window, dil) in enumerate(DIL_PATTERNS):
        sub_len = seq // dil
        radius = window // (2 * dil)
        qb = min(DIL_Q_BLOCK, sub_len)
        kw = min(DIL_K_WINDOW, sub_len)
        assert kw == sub_len or kw >= qb + 2 * radius
        blocks = sub_len // qb
        rel = (lax.broadcasted_iota(jnp.int32, (qb, kw), 0)
               - lax.broadcasted_iota(jnp.int32, (qb, kw), 1))

        def rows(first, count, dil=dil):
            return pl.ds(first, count) if dil == 1 else pl.ds(first, count, stride=dil)

        q_rows, k_rows, shifts = [], [], []
        for res in range(dil):
            for jb in range(blocks):
                j0 = jb * qb
                k0 = min(max(j0 - (kw - qb) // 2, 0), sub_len - kw)
                q_rows.append(rows(res + dil * j0, qb))
                k_rows.append(rows(res + dil * k0, kw))
                shifts.append(j0 - k0)
        near_by_shift = {s: jnp.abs(rel + s) <= radius for s in sorted(set(shifts))}
        q = jnp.stack([q_ref[r, :] for r in q_rows])
        k = jnp.stack([k_ref[r, :] for r in k_rows]).astype(BF16)
        v = jnp.stack([v_ref[r, :] for r in k_rows]).astype(BF16)

        acc, m_all, l_all = None, None, None
        for h in range(2):
            qh = jnp.where(head0 if h == 0 else ~head0, q, 0.0).astype(BF16)
            s = lax.dot_general(qh, k, (((2,), (2,)), ((0,), (0,))), preferred_element_type=F32)
            s = jnp.stack([jnp.where(near_by_shift[shift], s[n], MASKED_SCORE)
                           for n, shift in enumerate(shifts)])
            m = jnp.max(s, axis=-1, keepdims=True)
            e = jnp.exp(s - m)
            l = jnp.sum(e, axis=-1, keepdims=True)
            a = lax.dot_general(e.astype(BF16), v, (((2,), (1,)), ((0,), (0,))),
                                preferred_element_type=F32)
            if h == 0:
                acc, m_all, l_all = a, m, l
            else:
                acc = jnp.where(head0, acc, a)
                m_all = jnp.where(head0, m_all, m)
                l_all = jnp.where(head0, l_all, l)
        for n, r in enumerate(q_rows):
            acc_ref[p, r, :] = acc[n]
            m_ref[p, r, :] = m_all[n]
            l_ref[p, r, :] = l_all[n]

    m = m_ref[...]
    w = jnp.exp(m - jnp.max(m, axis=0, keepdims=True))
    o_ref[...] = jnp.sum(w * acc_ref[...], axis=0) / jnp.sum(w * l_ref[...], axis=0)


def _dil_attn(qd, kd, vd, batch, seq):
    spec = pl.BlockSpec((None, seq, LANES), lambda b, p: (b, 0, p))
    stats = pltpu.VMEM((len(DIL_PATTERNS), seq, LANES), F32)
    return pl.pallas_call(
        _dil_attn_kernel,
        grid=(batch, DIL_HEADS // 2),
        in_specs=[spec, spec, spec],
        out_specs=spec,
        out_shape=jax.ShapeDtypeStruct((batch, seq, DIL_WIDTH), F32),
        scratch_shapes=[stats, stats, stats],
        compiler_params=_params(("parallel", "parallel")),
        name="dil_attn",
    )(qd.reshape(batch, seq, DIL_WIDTH), kd.reshape(batch, seq, DIL_WIDTH),
      vd.reshape(batch, seq, DIL_WIDTH))


class _Ranked:
    def __init__(self, r8):
        self.r8 = r8
        self.rows = []
        self.lo = jnp.zeros(r8.shape, F32)
        self.hi = jnp.zeros(r8.shape, F32)

    def push(self, row):
        k = len(self.rows)
        self.rows.append(row)
        if k < 8:
            self.lo = jnp.where(self.r8 == k, row, self.lo)
        else:
            self.hi = jnp.where(self.r8 == k - 8, row, self.hi)


KEY_STACK = 4


def _top16(blocks, r8):
    groups = []
    for g in range(len(blocks) // KEY_STACK):
        vs = [blocks[g * KEY_STACK + d] for d in range(KEY_STACK)]
        ids = [r8 + float(8 * (g * KEY_STACK + d)) for d in range(KEY_STACK)]
        for a in (0, 2, 1, 0, 2, 1):
            swap = vs[a + 1] > vs[a]
            vs[a], vs[a + 1] = jnp.where(swap, vs[a + 1], vs[a]), jnp.where(swap, vs[a], vs[a + 1])
            ids[a], ids[a + 1] = jnp.where(swap, ids[a + 1], ids[a]), jnp.where(swap, ids[a], ids[a + 1])
        groups.append((vs, ids))

    tv, ti = _Ranked(r8), _Ranked(r8)
    for k in range(PEER_TOPK):
        m = jnp.max(functools.reduce(jnp.maximum, [vs[0] for vs, _ in groups]),
                    axis=0, keepdims=True)
        low = functools.reduce(jnp.minimum, [jnp.where(vs[0] == m, ids[0], float(PEER_KEYS))
                                            for vs, ids in groups])
        idx = jnp.min(low, axis=0, keepdims=True)
        tv.push(m)
        ti.push(idx)
        live = min(KEY_STACK, PEER_TOPK - 1 - k)
        for vs, ids in groups:
            if live == 0:
                break
            hit = ids[0] == idx
            for d in range(live):
                below = vs[d + 1] if d + 1 < KEY_STACK else -jnp.inf
                vs[d] = jnp.where(hit, below, vs[d])
                if d + 1 < KEY_STACK:
                    ids[d] = jnp.where(hit, ids[d + 1], ids[d])
    return tv, ti


_POS_SHIFT = float(PEER_EXPERTS)


def _pair_top16(v1, i1, v2, i2, r8):
    ninf = -jnp.inf
    big = 3.0e38
    lo_v, lo_k = [], []
    for j in range(PEER_TOPK):
        lists = PEER_TOPK // (j + 1)
        val = v1.lo + v2.rows[j]
        lo_v.append(val if lists >= 8 else jnp.where(r8 < lists, val, ninf))
        lo_k.append((r8 * 16.0 + j) * _POS_SHIFT + i1.lo * PEER_KEYS + i2.rows[j])
    hi_v = v1.hi + v2.rows[0]
    hi_k = (r8 * 16.0 + 128.0) * _POS_SHIFT + i1.hi * PEER_KEYS + i2.rows[0]

    top, sel = _Ranked(r8), _Ranked(r8)
    for k in range(PEER_TOPK):
        m = jnp.max(jnp.maximum(lo_v[0], hi_v), axis=0, keepdims=True)
        kk = jnp.min(jnp.minimum(jnp.where(lo_v[0] == m, lo_k[0], big),
                                 jnp.where(hi_v == m, hi_k, big)), axis=0, keepdims=True)
        top.push(m)
        sel.push(kk)
        live = PEER_TOPK - 1 - k
        if live:
            hit = lo_k[0] == kk
            for d in range(live):
                lo_v[d] = jnp.where(hit, lo_v[d + 1], lo_v[d])
                lo_k[d] = jnp.where(hit, lo_k[d + 1], lo_k[d])
            hi_v = jnp.where(hi_k == kk, ninf, hi_v)
    return top, sel


GATE_PITCH = PEER_KEYS + 8
GATE_BATCH = 8
ROUTE_UNROLL = 2


def _build_gates(slot_ref, first, count, stage_ref, g_ref):
    key_iota = lax.broadcasted_iota(jnp.int32, (1, PEER_KEYS, PEER_SLOTS), 1).astype(F32)
    for t0 in range(0, count, GATE_BATCH):
        rows = pl.ds(first + t0, GATE_BATCH)
        a = slot_ref[0, rows, :][:, None, :]
        b = slot_ref[1, rows, :][:, None, :]
        g = slot_ref[2, rows, :][:, None, :]
        one_a = jnp.where(key_iota == a, 1.0, 0.0).astype(BF16)
        g_at_b = jnp.where(key_iota == b, g, 0.0).astype(BF16)
        gates = lax.dot_general(one_a, g_at_b, (((2,), (2,)), ((0,), (0,))),
                                preferred_element_type=F32)
        for t in range(GATE_BATCH):
            r0 = (t0 + t) * GATE_PITCH
            stage_ref[r0:r0 + PEER_KEYS, :] = gates[t]
    for a in range(PEER_KEYS):
        g_ref[a, pl.ds(first, count), :] = stage_ref[
            pl.ds(a, count, stride=GATE_PITCH), :].astype(BF16)


def _route_kernel(om_ref, od_ref, x_ref, gom_ref, god_ref, wom_ref, wod_ref, gffn_ref,
                  wq_ref, kdt_ref,
                  h_ref, xn_ref, g_ref,
                  sc_ref, ranked_ref, slots_ref, stage_ref):
    step = pl.program_id(0)
    tile = x_ref.shape[0]

    @pl.when(step == 0)
    def _():
        slots_ref[...] = jnp.zeros_like(slots_ref)

    prev_slots = slots_ref.at[(step + 1) % 2]
    nm = _rms(om_ref[...], gom_ref[...]).astype(BF16)
    nd = _rms(od_ref[...], god_ref[...]).astype(BF16)
    h = x_ref[...] + _dot(nm, wom_ref[...]) + _dot(nd, wod_ref[...])
    h_ref[...] = h
    xn = _rms(h, gffn_ref[...]).astype(BF16)
    xn_ref[...] = xn
    q = _dot(xn, wq_ref[...])
    kdt = kdt_ref[...]
    columns = tile // LANES
    units = PEER_HEADS * columns
    for hd in range(PEER_HEADS):
        qh = q[:, hd * LANES:(hd + 1) * LANES].astype(BF16)
        for c in range(columns):
            sc_ref[hd * columns + c] = _dot_nt(kdt, qh[c * LANES:(c + 1) * LANES])

    r8 = lax.broadcasted_iota(jnp.int32, (8, LANES), 0).astype(F32)
    tokens_per_unit = tile // units

    def rank_and_build(unit, carry):
        def key_blocks(first):
            return [sc_ref[unit, first + 8 * d:first + 8 * d + 8, :] for d in range(PEER_KEYS // 8)]

        v1, i1 = _top16(key_blocks(0), r8)
        v2, i2 = _top16(key_blocks(PEER_KEYS), r8)
        top, sel = _pair_top16(v1, i1, v2, i2, r8)
        e_lo = jnp.exp(top.lo - top.rows[0])
        e_hi = jnp.exp(top.hi - top.rows[0])
        inv = 1.0 / (jnp.sum(e_lo, axis=0, keepdims=True) + jnp.sum(e_hi, axis=0, keepdims=True))
        for half, (s, e) in enumerate(((sel.lo, e_lo), (sel.hi, e_hi))):
            rows = slice(half * 8, half * 8 + 8)
            expert = s - jnp.floor(s * (1.0 / _POS_SHIFT)) * _POS_SHIFT
            a = jnp.floor(expert * (1.0 / PEER_KEYS))
            ranked_ref[0, unit, rows, :] = a
            ranked_ref[1, unit, rows, :] = expert - a * PEER_KEYS
            ranked_ref[2, unit, rows, :] = e * inv
        first = pl.multiple_of(unit * tokens_per_unit, tokens_per_unit)
        _build_gates(prev_slots, first, tokens_per_unit, stage_ref, g_ref)
        return carry

    lax.fori_loop(0, units, rank_and_build, 0, unroll=ROUTE_UNROLL)
    slots = slots_ref.at[step % 2]
    for k in range(3):
        for c in range(columns):
            by_slot = jnp.concatenate(
                [ranked_ref[k, hd * columns + c] for hd in range(PEER_HEADS)], axis=0)
            slots[k, c * LANES:(c + 1) * LANES, :] = by_slot.T


def _route(o_mla, o_dil, x2, g_om, g_od, w_om, w_od, g_ffn, w_q, kdt, tile):
    T = x2.shape[0]
    steps = T // tile
    full = lambda a: pl.BlockSpec(a.shape, lambda i: (0,) * a.ndim)
    row = lambda w: pl.BlockSpec((tile, w), lambda i: (jnp.minimum(i, steps - 1), 0))
    units = PEER_HEADS * (tile // LANES)
    return pl.pallas_call(
        _route_kernel,
        grid=(steps + 1,),
        in_specs=[row(o_mla.shape[1]), row(o_dil.shape[1]), row(D_MODEL), full(g_om), full(g_od),
                  full(w_om), full(w_od), full(g_ffn), full(w_q), full(kdt)],
        out_specs=[row(D_MODEL), row(D_MODEL),
                   pl.BlockSpec((PEER_KEYS, tile, PEER_KEYS),
                                lambda i: (0, jnp.maximum(i - 1, 0), 0))],
        out_shape=[jax.ShapeDtypeStruct((T, D_MODEL), F32),
                   jax.ShapeDtypeStruct((T, D_MODEL), BF16),
                   jax.ShapeDtypeStruct((PEER_KEYS, T, PEER_KEYS), BF16)],
        scratch_shapes=[pltpu.VMEM((units, 2 * PEER_KEYS, LANES), F32),
                        pltpu.VMEM((3, units, PEER_TOPK, LANES), F32),
                        pltpu.VMEM((2, 3, tile, PEER_SLOTS), F32),
                        pltpu.VMEM((tile // units * GATE_PITCH, PEER_KEYS), F32)],
        compiler_params=_params(("arbitrary",)),
        name="route",
    )(o_mla, o_dil, x2, g_om, g_od, w_om, w_od, g_ffn, w_q, kdt)


def _peer_kernel(xn_ref, u_ref, v_ref, g_ref, o_ref, *, a_per_step):
    pre = _dot_nt(xn_ref[...], u_ref[...].astype(BF16))
    act = 0.5 * pre * (1.0 + lax.erf(pre * float(1.0 / np.sqrt(2.0))))
    gate = jnp.concatenate([g_ref[a] for a in range(a_per_step)], axis=1)
    y = _dot((act * gate.astype(F32)).astype(BF16), v_ref[...].astype(BF16))

    @pl.when(pl.program_id(1) == 0)
    def _():
        o_ref[...] = y

    @pl.when(pl.program_id(1) != 0)
    def _():
        o_ref[...] += y


def _peer(xn, u, v, gates, tile, a_per_step):
    T = xn.shape[0]
    et = a_per_step * PEER_KEYS
    return pl.pallas_call(
        functools.partial(_peer_kernel, a_per_step=a_per_step),
        grid=(T // tile, PEER_EXPERTS // et),
        in_specs=[pl.BlockSpec((tile, D_MODEL), lambda i, j: (i, 0)),
                  pl.BlockSpec((et, D_MODEL), lambda i, j: (j, 0)),
                  pl.BlockSpec((et, D_MODEL), lambda i, j: (j, 0)),
                  pl.BlockSpec((a_per_step, tile, PEER_KEYS), lambda i, j: (j, i, 0))],
        out_specs=pl.BlockSpec((tile, D_MODEL), lambda i, j: (i, 0)),
        out_shape=jax.ShapeDtypeStruct((T, D_MODEL), F32),
        compiler_params=_params(("parallel", "arbitrary")),
        name="peer",
    )(xn, u, v, gates)


def _ple_final_kernel(h_ref, y_ref, p_ref, gple_ref, wg_ref, wp_ref, gfin_ref, o_ref, *, last_layer):
    h = h_ref[...] + y_ref[...]
    gate = jax.nn.sigmoid(_dot(_rms(h, gple_ref[...]).astype(BF16), wg_ref[...]))
    h = h + gate * _dot(p_ref[...].astype(BF16), wp_ref[...])
    o_ref[...] = _rms(h, gfin_ref[...]) if last_layer else h


def _ple_final(h1, y, p2, g_ple, w_gate, w_proj, g_final, tile, last_layer):
    T = h1.shape[0]
    full = lambda a: pl.BlockSpec(a.shape, lambda i: (0,) * a.ndim)
    row = lambda w: pl.BlockSpec((tile, w), lambda i: (i, 0))
    return pl.pallas_call(
        functools.partial(_ple_final_kernel, last_layer=last_layer),
        grid=(T // tile,),
        in_specs=[row(D_MODEL), row(D_MODEL), row(PLE_DIM), full(g_ple), full(w_gate),
                  full(w_proj), full(g_final)],
        out_specs=row(D_MODEL),
        out_shape=jax.ShapeDtypeStruct((T, D_MODEL), F32),
        compiler_params=_params(("parallel",)),
        name="ple_final",
    )(h1, y, p2, g_ple, w_gate, w_proj, g_final)


def _rope_lane_tables(seq, rot_dim, first_lane, period, scale):
    half = rot_dim // 2
    inv = ROPE_THETA ** (-jnp.arange(0, rot_dim, 2, dtype=F32) / rot_dim)
    ang = jnp.arange(seq, dtype=F32)[:, None] * inv[None, :]
    cos, sin = jnp.cos(ang), jnp.sin(ang)
    c = jnp.ones((seq, period), F32)
    c = c.at[:, first_lane:first_lane + half].set(cos).at[:, first_lane + half:first_lane + rot_dim].set(cos)
    s_lo = jnp.zeros((seq, period), F32).at[:, first_lane + half:first_lane + rot_dim].set(sin)
    s_hi = jnp.zeros((seq, period), F32).at[:, first_lane:first_lane + half].set(-sin)
    tabs = jnp.stack([c, s_lo, s_hi]) * scale
    return jnp.tile(tabs, (1, 1, LANES // period))


def _pick_tile(n, want):
    t = min(n, want)
    assert n % t == 0, (n, t)
    return t


def kernel(x, p, g_mix, w_in, g_cq, w_uq, g_ckv, w_ukv, g_out_mla, g_out_dil, w_out, g_ffn,
           w_peer_q, peer_keys1, peer_keys2, peer_u, peer_v, g_ple, w_ple_gate, w_ple_proj, g_final):
    batch, seq, _ = x.shape
    depth = p.shape[0]
    T = batch * seq
    mla_w = MLA_HEADS * MLA_V

    tabs = (_rope_lane_tables(seq, MLA_ROPE, MLA_NOPE, LANES, MLA_QK ** -0.5),
            _rope_lane_tables(seq, MLA_ROPE, MLA_NOPE, LANES, 1.0),
            _rope_lane_tables(seq, DIL_ROT, 0, DIL_HEAD_DIM, DIL_HEAD_DIM ** -0.5),
            _rope_lane_tables(seq, DIL_ROT, 0, DIL_HEAD_DIM, 1.0))

    h = x.reshape(T, D_MODEL)
    for i in range(depth):
        wi = w_in[i]
        o1 = MLA_Q_RANK + MLA_KV_RANK
        k_r_cols = jnp.pad(wi[:, o1:o1 + MLA_ROPE], ((0, 0), (MLA_NOPE, LANES - MLA_QK)))
        w_in_p = jnp.concatenate([wi[:, :o1], k_r_cols, wi[:, o1 + MLA_ROPE:]], axis=1).astype(BF16)
        w_uq_p = jnp.pad(w_uq[i].reshape(MLA_Q_RANK, MLA_HEADS, MLA_QK),
                         ((0, 0), (0, 0), (0, LANES - MLA_QK))).reshape(MLA_Q_RANK, -1).astype(BF16)
        w_ukv3 = w_ukv[i].reshape(MLA_KV_RANK, MLA_HEADS, MLA_NOPE + MLA_V)
        w_uk_p = jnp.pad(w_ukv3[:, :, :MLA_NOPE],
                         ((0, 0), (0, 0), (0, LANES - MLA_NOPE))).reshape(MLA_KV_RANK, -1).astype(BF16)
        w_uv = w_ukv3[:, :, MLA_NOPE:].reshape(MLA_KV_RANK, -1).astype(BF16)
        zeros = jnp.zeros((PEER_KEYS, PEER_HALF), F32)
        kdt = jnp.concatenate([jnp.concatenate([peer_keys1[i], zeros], axis=1),
                               jnp.concatenate([zeros, peer_keys2[i]], axis=1)], axis=0).astype(BF16)

        qm, km, vm, qd, kd, vd = _in_proj(
            h, g_mix[i][None], w_in_p, g_cq[i][None], w_uq_p, g_ckv[i][None], w_uk_p, w_uv,
            tabs, seq, _pick_tile(seq, PROJ_TILE))
        o_mla = _mla_attn(qm, km, vm, batch, seq, _pick_tile(seq, MLA_Q_TILE)).reshape(T, mla_w)
        o_dil = _dil_attn(qd, kd, vd, batch, seq).reshape(T, DIL_WIDTH)

        h1, xn, gates = _route(
            o_mla, o_dil, h, g_out_mla[i][None], g_out_dil[i][None],
            w_out[i][:mla_w].astype(BF16), w_out[i][mla_w:].astype(BF16), g_ffn[i][None],
            w_peer_q[i].astype(BF16), kdt, _pick_tile(T, ROUTE_TILE))
        y = _peer(xn, peer_u[i], peer_v[i], gates, _pick_tile(T, PEER_TILE), PEER_A_PER_STEP)
        h = _ple_final(h1, y, p[i].reshape(T, PLE_DIM), g_ple[i][None], w_ple_gate[i].astype(BF16),
                       w_ple_proj[i].astype(BF16), g_final[None], _pick_tile(T, PROJ_TILE),
                       last_layer=(i == depth - 1))
    return h.reshape(batch, seq, D_MODEL)
```

```python
import functools

import numpy as np
import jax
import jax.numpy as jnp
from jax import lax
from jax.experimental import pallas as pl
from jax.experimental.pallas import tpu as pltpu

F32 = jnp.float32
BF16 = jnp.bfloat16

EPS = 1e-6
MASKED_SCORE = -1e30
ROPE_THETA = 500000.0

LANES = 128
D_MODEL = 1024
PLE_DIM = 256
MLA_HEADS = 8
MLA_NOPE = 64
MLA_ROPE = 32
MLA_V = 64
MLA_QK = MLA_NOPE + MLA_ROPE
MLA_Q_RANK = 384
MLA_KV_RANK = 256
DIL_HEADS = 8
DIL_HEAD_DIM = 64
DIL_ROT = 16
DIL_WIDTH = DIL_HEADS * DIL_HEAD_DIM
DIL_PATTERNS = ((128, 1), (512, 4), (2048, 16))
PEER_KEYS = 128
PEER_HEADS = 8
PEER_HALF = 64
PEER_TOPK = 16
PEER_SLOTS = PEER_HEADS * PEER_TOPK
PEER_EXPERTS = PEER_KEYS * PEER_KEYS

VMEM_LIMIT = 48 * 1024 * 1024

PROJ_TILE = 512
MLA_Q_TILE = 1024
MLA_PV_CHUNKS = 2
ROUTE_TILE = 256
PEER_TILE = 1024
PEER_A_PER_STEP = 8


def _params(semantics):
    return pltpu.CompilerParams(dimension_semantics=semantics, vmem_limit_bytes=VMEM_LIMIT)


def _rms(x, g):
    return x * lax.rsqrt(jnp.mean(x * x, axis=-1, keepdims=True) + EPS) * g


def _rope(x, tab_ref, half):
    return (x * tab_ref[0]
            + pltpu.roll(x, half, 1) * tab_ref[1]
            + pltpu.roll(x, LANES - half, 1) * tab_ref[2])


def _dot(a, b):
    return jnp.dot(a, b, preferred_element_type=F32)


def _dot_nt(a, b):
    return lax.dot_general(a, b, (((1,), (1,)), ((), ())), preferred_element_type=F32)


def _in_proj_kernel(x_ref, gmix_ref, win_ref, gcq_ref, wuq_ref, gckv_ref, wuk_ref, wuv_ref,
                    tmq_ref, tmk_ref, tdq_ref, tdk_ref,
                    qm_ref, km_ref, vm_ref, qd_ref, kd_ref, vd_ref):
    hn = _rms(x_ref[...], gmix_ref[...]).astype(BF16)
    y = _dot(hn, win_ref[...])
    o = 0
    c_q = y[:, o:o + MLA_Q_RANK]; o += MLA_Q_RANK
    c_kv = y[:, o:o + MLA_KV_RANK]; o += MLA_KV_RANK
    k_r = y[:, o:o + LANES]; o += LANES
    q_d = y[:, o:o + DIL_WIDTH]; o += DIL_WIDTH
    k_d = y[:, o:o + DIL_WIDTH]; o += DIL_WIDTH
    v_d = y[:, o:o + DIL_WIDTH]

    q = _dot(_rms(c_q, gcq_ref[...]).astype(BF16), wuq_ref[...])
    ckvn = _rms(c_kv, gckv_ref[...]).astype(BF16)
    k_n = _dot(ckvn, wuk_ref[...])
    vm_ref[...] = _dot(ckvn, wuv_ref[...]).astype(BF16)
    k_rope = _rope(k_r, tmk_ref, MLA_ROPE // 2)
    for h in range(MLA_HEADS):
        sl = slice(h * LANES, (h + 1) * LANES)
        qm_ref[:, sl] = _rope(q[:, sl], tmq_ref, MLA_ROPE // 2).astype(BF16)
        km_ref[:, sl] = (k_n[:, sl] + k_rope).astype(BF16)
    for c in range(DIL_WIDTH // LANES):
        sl = slice(c * LANES, (c + 1) * LANES)
        qd_ref[:, sl] = _rope(q_d[:, sl], tdq_ref, DIL_ROT // 2)
        kd_ref[:, sl] = _rope(k_d[:, sl], tdk_ref, DIL_ROT // 2)
    vd_ref[...] = v_d


def _in_proj(x2, g_mix, w_in_p, g_cq, w_uq_p, g_ckv, w_uk_p, w_uv, tabs, seq, tile):
    T = x2.shape[0]
    n_pos = seq // tile
    full = lambda a: pl.BlockSpec(a.shape, lambda i: (0,) * a.ndim)
    tab = pl.BlockSpec((3, tile, LANES), lambda i: (0, i % n_pos, 0))
    row = lambda w: pl.BlockSpec((tile, w), lambda i: (i, 0))
    outs = [(T, MLA_HEADS * LANES), (T, MLA_HEADS * LANES), (T, MLA_HEADS * MLA_V),
            (T, DIL_WIDTH), (T, DIL_WIDTH), (T, DIL_WIDTH)]
    return pl.pallas_call(
        _in_proj_kernel,
        grid=(T // tile,),
        in_specs=[row(D_MODEL), full(g_mix), full(w_in_p), full(g_cq), full(w_uq_p),
                  full(g_ckv), full(w_uk_p), full(w_uv), tab, tab, tab, tab],
        out_specs=[row(s[1]) for s in outs],
        out_shape=[jax.ShapeDtypeStruct(s, BF16) for s in outs[:3]]
                  + [jax.ShapeDtypeStruct(s, F32) for s in outs[3:]],
        compiler_params=_params(("parallel",)),
        name="in_proj",
    )(x2, g_mix, w_in_p, g_cq, w_uq_p, g_ckv, w_uk_p, w_uv, *tabs)


def _mla_attn_kernel(q_ref, k_ref, v_ref, o_ref):
    v = v_ref[...]
    lane = lax.broadcasted_iota(jnp.int32, (q_ref.shape[0], LANES), 1)
    out = None
    for h in range(2):
        sl = slice(h * LANES, (h + 1) * LANES)
        s = _dot_nt(q_ref[:, sl], k_ref[:, sl])
        e = jnp.exp(s - jnp.max(s, axis=-1, keepdims=True))
        p = e.astype(BF16)
        rows = p.shape[0] // MLA_PV_CHUNKS
        o = jnp.concatenate([_dot(p[c * rows:(c + 1) * rows], v) for c in range(MLA_PV_CHUNKS)],
                            axis=0) / jnp.sum(e, axis=-1, keepdims=True)
        out = o if out is None else jnp.where(lane < MLA_V, out, o)
    o_ref[...] = out


def _mla_attn(qm, km, vm, batch, seq, tq):
    q3 = qm.reshape(batch, seq, MLA_HEADS * LANES)
    k3 = km.reshape(batch, seq, MLA_HEADS * LANES)
    v3 = vm.reshape(batch, seq, MLA_HEADS * MLA_V)
    return pl.pallas_call(
        _mla_attn_kernel,
        grid=(batch, MLA_HEADS // 2, seq // tq),
        in_specs=[pl.BlockSpec((None, tq, 2 * LANES), lambda b, p, i: (b, i, p)),
                  pl.BlockSpec((None, seq, 2 * LANES), lambda b, p, i: (b, 0, p)),
                  pl.BlockSpec((None, seq, LANES), lambda b, p, i: (b, 0, p))],
        out_specs=pl.BlockSpec((None, tq, LANES), lambda b, p, i: (b, i, p)),
        out_shape=jax.ShapeDtypeStruct((batch, seq, MLA_HEADS * MLA_V), F32),
        compiler_params=_params(("parallel", "parallel", "parallel")),
        name="mla_attn",
    )(q3, k3, v3)


DIL_Q_BLOCK = 128
DIL_K_WINDOW = 256


def _dil_attn_kernel(q_ref, k_ref, v_ref, o_ref, acc_ref, m_ref, l_ref):
    seq = q_ref.shape[0]
    lane = lax.broadcasted_iota(jnp.int32, (1, 1, LANES), 2)
    head0 = lane < DIL_HEAD_DIM

    for p, (window, dil) in enumerate(DIL_PATTERNS):
        sub_len = seq // dil
        radius = window // (2 * dil)
        qb = min(DIL_Q_BLOCK, sub_len)
        kw = min(DIL_K_WINDOW, sub_len)
        assert kw == sub_len or kw >= qb + 2 * radius
        blocks = sub_len // qb
        rel = (lax.broadcasted_iota(jnp.int32, (qb, kw), 0)
               - lax.broadcasted_iota(jnp.int32, (qb, kw), 1))

        def rows(first, count, dil=dil):
            return pl.ds(first, count) if dil == 1 else pl.ds(first, count, stride=dil)

        q_rows, k_rows, shifts = [], [], []
        for res in range(dil):
            for jb in range(blocks):
                j0 = jb * qb
                k0 = min(max(j0 - (kw - qb) // 2, 0), sub_len - kw)
                q_rows.append(rows(res + dil * j0, qb))
                k_rows.append(rows(res + dil * k0, kw))
                shifts.append(j0 - k0)
        near_by_shift = {s: jnp.abs(rel + s) <= radius for s in sorted(set(shifts))}
        q = jnp.stack([q_ref[r, :] for r in q_rows])
        k = jnp.stack([k_ref[r, :] for r in k_rows]).astype(BF16)
        v = jnp.stack([v_ref[r, :] for r in k_rows]).astype(BF16)

        acc, m_all, l_all = None, None, None
        for h in range(2):
            qh = jnp.where(head0 if h == 0 else ~head0, q, 0.0).astype(BF16)
            s = lax.dot_general(qh, k, (((2,), (2,)), ((0,), (0,))), preferred_element_type=F32)
            s = jnp.stack([jnp.where(near_by_shift[shift], s[n], MASKED_SCORE)
                           for n, shift in enumerate(shifts)])
            m = jnp.max(s, axis=-1, keepdims=True)
            e = jnp.exp(s - m)
            l = jnp.sum(e, axis=-1, keepdims=True)
            a = lax.dot_general(e.astype(BF16), v, (((2,), (1,)), ((0,), (0,))),
                                preferred_element_type=F32)
            if h == 0:
                acc, m_all, l_all = a, m, l
            else:
                acc = jnp.where(head0, acc, a)
                m_all = jnp.where(head0, m_all, m)
                l_all = jnp.where(head0, l_all, l)
        for n, r in enumerate(q_rows):
            acc_ref[p, r, :] = acc[n]
            m_ref[p, r, :] = m_all[n]
            l_ref[p, r, :] = l_all[n]

    m = m_ref[...]
    w = jnp.exp(m - jnp.max(m, axis=0, keepdims=True))
    o_ref[...] = jnp.sum(w * acc_ref[...], axis=0) / jnp.sum(w * l_ref[...], axis=0)


def _dil_attn(qd, kd, vd, batch, seq):
    spec = pl.BlockSpec((None, seq, LANES), lambda b, p: (b, 0, p))
    stats = pltpu.VMEM((len(DIL_PATTERNS), seq, LANES), F32)
    return pl.pallas_call(
        _dil_attn_kernel,
        grid=(batch, DIL_HEADS // 2),
        in_specs=[spec, spec, spec],
        out_specs=spec,
        out_shape=jax.ShapeDtypeStruct((batch, seq, DIL_WIDTH), F32),
        scratch_shapes=[stats, stats, stats],
        compiler_params=_params(("parallel", "parallel")),
        name="dil_attn",
    )(qd.reshape(batch, seq, DIL_WIDTH), kd.reshape(batch, seq, DIL_WIDTH),
      vd.reshape(batch, seq, DIL_WIDTH))


class _Ranked:
    def __init__(self, r8):
        self.r8 = r8
        self.rows = []
        self.lo = jnp.zeros(r8.shape, F32)
        self.hi = jnp.zeros(r8.shape, F32)

    def push(self, row):
        k = len(self.rows)
        self.rows.append(row)
        if k < 8:
            self.lo = jnp.where(self.r8 == k, row, self.lo)
        else:
            self.hi = jnp.where(self.r8 == k - 8, row, self.hi)


KEY_STACK = 8


def _top16(blocks, r8):
    groups = []
    for g in range(len(blocks) // KEY_STACK):
        vs = [blocks[g * KEY_STACK + d] for d in range(KEY_STACK)]
        ids = [r8 + float(8 * (g * KEY_STACK + d)) for d in range(KEY_STACK)]
        for a in [a for r in range(KEY_STACK) for a in range(r % 2, KEY_STACK - 1, 2)]:
            swap = vs[a + 1] > vs[a]
            vs[a], vs[a + 1] = jnp.where(swap, vs[a + 1], vs[a]), jnp.where(swap, vs[a], vs[a + 1])
            ids[a], ids[a + 1] = jnp.where(swap, ids[a + 1], ids[a]), jnp.where(swap, ids[a], ids[a + 1])
        groups.append((vs, ids))

    tv, ti = _Ranked(r8), _Ranked(r8)
    for k in range(PEER_TOPK):
        m = jnp.max(functools.reduce(jnp.maximum, [vs[0] for vs, _ in groups]),
                    axis=0, keepdims=True)
        low = functools.reduce(jnp.minimum, [jnp.where(vs[0] == m, ids[0], float(PEER_KEYS))
                                            for vs, ids in groups])
        idx = jnp.min(low, axis=0, keepdims=True)
        tv.push(m)
        ti.push(idx)
        live = min(KEY_STACK, PEER_TOPK - 1 - k)
        for vs, ids in groups:
            if live == 0:
                break
            hit = ids[0] == idx
            for d in range(live):
                below = vs[d + 1] if d + 1 < KEY_STACK else -jnp.inf
                vs[d] = jnp.where(hit, below, vs[d])
                if d + 1 < KEY_STACK:
                    ids[d] = jnp.where(hit, ids[d + 1], ids[d])
    return tv, ti


_POS_SHIFT = float(PEER_EXPERTS)


def _pair_top16(v1, i1, v2, i2, r8):
    ninf = -jnp.inf
    big = 3.0e38
    lo_v, lo_k = [], []
    for j in range(PEER_TOPK):
        lists = PEER_TOPK // (j + 1)
        val = v1.lo + v2.rows[j]
        lo_v.append(val if lists >= 8 else jnp.where(r8 < lists, val, ninf))
        lo_k.append((r8 * 16.0 + j) * _POS_SHIFT + i1.lo * PEER_KEYS + i2.rows[j])
    hi_v = v1.hi + v2.rows[0]
    hi_k = (r8 * 16.0 + 128.0) * _POS_SHIFT + i1.hi * PEER_KEYS + i2.rows[0]

    top, sel = _Ranked(r8), _Ranked(r8)
    for k in range(PEER_TOPK):
        m = jnp.max(jnp.maximum(lo_v[0], hi_v), axis=0, keepdims=True)
        kk = jnp.min(jnp.minimum(jnp.where(lo_v[0] == m, lo_k[0], big),
                                 jnp.where(hi_v == m, hi_k, big)), axis=0, keepdims=True)
        top.push(m)
        sel.push(kk)
        live = PEER_TOPK - 1 - k
        if live:
            hit = lo_k[0] == kk
            for d in range(live):
                lo_v[d] = jnp.where(hit, lo_v[d + 1], lo_v[d])
                lo_k[d] = jnp.where(hit, lo_k[d + 1], lo_k[d])
            hi_v = jnp.where(hi_k == kk, ninf, hi_v)
    return top, sel


GATE_PITCH = PEER_KEYS + 8
GATE_BATCH = 8
ROUTE_UNROLL = 2


def _build_gates(slot_ref, first, count, stage_ref, g_ref):
    key_iota = lax.broadcasted_iota(jnp.int32, (1, PEER_KEYS, PEER_SLOTS), 1).astype(F32)
    for t0 in range(0, count, GATE_BATCH):
        rows = pl.ds(first + t0, GATE_BATCH)
        a = slot_ref[0, rows, :][:, None, :]
        b = slot_ref[1, rows, :][:, None, :]
        g = slot_ref[2, rows, :][:, None, :]
        one_a = jnp.where(key_iota == a, 1.0, 0.0).astype(BF16)
        g_at_b = jnp.where(key_iota == b, g, 0.0).astype(BF16)
        gates = lax.dot_general(one_a, g_at_b, (((2,), (2,)), ((0,), (0,))),
                                preferred_element_type=F32)
        for t in range(GATE_BATCH):
            r0 = (t0 + t) * GATE_PITCH
            stage_ref[r0:r0 + PEER_KEYS, :] = gates[t]
    for a in range(PEER_KEYS):
        g_ref[a, pl.ds(first, count), :] = stage_ref[
            pl.ds(a, count, stride=GATE_PITCH), :].astype(BF16)


def _route_kernel(om_ref, od_ref, x_ref, gom_ref, god_ref, wom_ref, wod_ref, gffn_ref,
                  wq_ref, kdt_ref,
                  h_ref, xn_ref, g_ref,
                  sc_ref, ranked_ref, slots_ref, stage_ref):
    step = pl.program_id(0)
    tile = x_ref.shape[0]

    @pl.when(step == 0)
    def _():
        slots_ref[...] = jnp.zeros_like(slots_ref)

    prev_slots = slots_ref.at[(step + 1) % 2]
    nm = _rms(om_ref[...], gom_ref[...]).astype(BF16)
    nd = _rms(od_ref[...], god_ref[...]).astype(BF16)
    h = x_ref[...] + _dot(nm, wom_ref[...]) + _dot(nd, wod_ref[...])
    h_ref[...] = h
    xn = _rms(h, gffn_ref[...]).astype(BF16)
    xn_ref[...] = xn
    q = _dot(xn, wq_ref[...])
    kdt = kdt_ref[...]
    columns = tile // LANES
    units = PEER_HEADS * columns
    for hd in range(PEER_HEADS):
        qh = q[:, hd * LANES:(hd + 1) * LANES].astype(BF16)
        for c in range(columns):
            sc_ref[hd * columns + c] = _dot_nt(kdt, qh[c * LANES:(c + 1) * LANES])

    r8 = lax.broadcasted_iota(jnp.int32, (8, LANES), 0).astype(F32)
    tokens_per_unit = tile // units

    def rank_and_build(unit, carry):
        def key_blocks(first):
            return [sc_ref[unit, first + 8 * d:first + 8 * d + 8, :] for d in range(PEER_KEYS // 8)]

        v1, i1 = _top16(key_blocks(0), r8)
        v2, i2 = _top16(key_blocks(PEER_KEYS), r8)
        top, sel = _pair_top16(v1, i1, v2, i2, r8)
        e_lo = jnp.exp(top.lo - top.rows[0])
        e_hi = jnp.exp(top.hi - top.rows[0])
        inv = 1.0 / (jnp.sum(e_lo, axis=0, keepdims=True) + jnp.sum(e_hi, axis=0, keepdims=True))
        for half, (s, e) in enumerate(((sel.lo, e_lo), (sel.hi, e_hi))):
            rows = slice(half * 8, half * 8 + 8)
            expert = s - jnp.floor(s * (1.0 / _POS_SHIFT)) * _POS_SHIFT
            a = jnp.floor(expert * (1.0 / PEER_KEYS))
            ranked_ref[0, unit, rows, :] = a
            ranked_ref[1, unit, rows, :] = expert - a * PEER_KEYS
            ranked_ref[2, unit, rows, :] = e * inv
        first = pl.multiple_of(unit * tokens_per_unit, tokens_per_unit)
        _build_gates(prev_slots, first, tokens_per_unit, stage_ref, g_ref)
        return carry

    lax.fori_loop(0, units, rank_and_build, 0, unroll=ROUTE_UNROLL)
    slots = slots_ref.at[step % 2]
    for k in range(3):
        for c in range(columns):
            by_slot = jnp.concatenate(
                [ranked_ref[k, hd * columns + c] for hd in range(PEER_HEADS)], axis=0)
            slots[k, c * LANES:(c + 1) * LANES, :] = by_slot.T


def _route(o_mla, o_dil, x2, g_om, g_od, w_om, w_od, g_ffn, w_q, kdt, tile):
    T = x2.shape[0]
    steps = T // tile
    full = lambda a: pl.BlockSpec(a.shape, lambda i: (0,) * a.ndim)
    row = lambda w: pl.BlockSpec((tile, w), lambda i: (jnp.minimum(i, steps - 1), 0))
    units = PEER_HEADS * (tile // LANES)
    return pl.pallas_call(
        _route_kernel,
        grid=(steps + 1,),
        in_specs=[row(o_mla.shape[1]), row(o_dil.shape[1]), row(D_MODEL), full(g_om), full(g_od),
                  full(w_om), full(w_od), full(g_ffn), full(w_q), full(kdt)],
        out_specs=[row(D_MODEL), row(D_MODEL),
                   pl.BlockSpec((PEER_KEYS, tile, PEER_KEYS),
                                lambda i: (0, jnp.maximum(i - 1, 0), 0))],
        out_shape=[jax.ShapeDtypeStruct((T, D_MODEL), F32),
                   jax.ShapeDtypeStruct((T, D_MODEL), BF16),
                   jax.ShapeDtypeStruct((PEER_KEYS, T, PEER_KEYS), BF16)],
        scratch_shapes=[pltpu.VMEM((units, 2 * PEER_KEYS, LANES), F32),
                        pltpu.VMEM((3, units, PEER_TOPK, LANES), F32),
                        pltpu.VMEM((2, 3, tile, PEER_SLOTS), F32),
                        pltpu.VMEM((tile // units * GATE_PITCH, PEER_KEYS), F32)],
        compiler_params=_params(("arbitrary",)),
        name="route",
    )(o_mla, o_dil, x2, g_om, g_od, w_om, w_od, g_ffn, w_q, kdt)


def _peer_kernel(xn_ref, u_ref, v_ref, g_ref, o_ref, *, a_per_step):
    pre = _dot_nt(xn_ref[...], u_ref[...].astype(BF16))
    act = 0.5 * pre * (1.0 + lax.erf(pre * float(1.0 / np.sqrt(2.0))))
    gate = jnp.concatenate([g_ref[a] for a in range(a_per_step)], axis=1)
    y = _dot((act * gate.astype(F32)).astype(BF16), v_ref[...].astype(BF16))

    @pl.when(pl.program_id(1) == 0)
    def _():
        o_ref[...] = y

    @pl.when(pl.program_id(1) != 0)
    def _():
        o_ref[...] += y


def _peer(xn, u, v, gates, tile, a_per_step):
    T = xn.shape[0]
    et = a_per_step * PEER_KEYS
    return pl.pallas_call(
        functools.partial(_peer_kernel, a_per_step=a_per_step),
        grid=(T // tile, PEER_EXPERTS // et),
        in_specs=[pl.BlockSpec((tile, D_MODEL), lambda i, j: (i, 0)),
                  pl.BlockSpec((et, D_MODEL), lambda i, j: (j, 0)),
                  pl.BlockSpec((et, D_MODEL), lambda i, j: (j, 0)),
                  pl.BlockSpec((a_per_step, tile, PEER_KEYS), lambda i, j: (j, i, 0))],
        out_specs=pl.BlockSpec((tile, D_MODEL), lambda i, j: (i, 0)),
        out_shape=jax.ShapeDtypeStruct((T, D_MODEL), F32),
        compiler_params=_params(("parallel", "arbitrary")),
        name="peer",
    )(xn, u, v, gates)


def _ple_final_kernel(h_ref, y_ref, p_ref, gple_ref, wg_ref, wp_ref, gfin_ref, o_ref, *, last_layer):
    h = h_ref[...] + y_ref[...]
    gate = jax.nn.sigmoid(_dot(_rms(h, gple_ref[...]).astype(BF16), wg_ref[...]))
    h = h + gate * _dot(p_ref[...].astype(BF16), wp_ref[...])
    o_ref[...] = _rms(h, gfin_ref[...]) if last_layer else h


def _ple_final(h1, y, p2, g_ple, w_gate, w_proj, g_final, tile, last_layer):
    T = h1.shape[0]
    full = lambda a: pl.BlockSpec(a.shape, lambda i: (0,) * a.ndim)
    row = lambda w: pl.BlockSpec((tile, w), lambda i: (i, 0))
    return pl.pallas_call(
        functools.partial(_ple_final_kernel, last_layer=last_layer),
        grid=(T // tile,),
        in_specs=[row(D_MODEL), row(D_MODEL), row(PLE_DIM), full(g_ple), full(w_gate),
                  full(w_proj), full(g_final)],
        out_specs=row(D_MODEL),
        out_shape=jax.ShapeDtypeStruct((T, D_MODEL), F32),
        compiler_params=_params(("parallel",)),
        name="ple_final",
    )(h1, y, p2, g_ple, w_gate, w_proj, g_final)


def _rope_lane_tables(seq, rot_dim, first_lane, period, scale):
    half = rot_dim // 2
    inv = ROPE_THETA ** (-jnp.arange(0, rot_dim, 2, dtype=F32) / rot_dim)
    ang = jnp.arange(seq, dtype=F32)[:, None] * inv[None, :]
    cos, sin = jnp.cos(ang), jnp.sin(ang)
    c = jnp.ones((seq, period), F32)
    c = c.at[:, first_lane:first_lane + half].set(cos).at[:, first_lane + half:first_lane + rot_dim].set(cos)
    s_lo = jnp.zeros((seq, period), F32).at[:, first_lane + half:first_lane + rot_dim].set(sin)
    s_hi = jnp.zeros((seq, period), F32).at[:, first_lane:first_lane + half].set(-sin)
    tabs = jnp.stack([c, s_lo, s_hi]) * scale
    return jnp.tile(tabs, (1, 1, LANES // period))


def _pick_tile(n, want):
    t = min(n, want)
    assert n % t == 0, (n, t)
    return t


def kernel(x, p, g_mix, w_in, g_cq, w_uq, g_ckv, w_ukv, g_out_mla, g_out_dil, w_out, g_ffn,
           w_peer_q, peer_keys1, peer_keys2, peer_u, peer_v, g_ple, w_ple_gate, w_ple_proj, g_final):
    batch, seq, _ = x.shape
    depth = p.shape[0]
    T = batch * seq
    mla_w = MLA_HEADS * MLA_V

    tabs = (_rope_lane_tables(seq, MLA_ROPE, MLA_NOPE, LANES, MLA_QK ** -0.5),
            _rope_lane_tables(seq, MLA_ROPE, MLA_NOPE, LANES, 1.0),
            _rope_lane_tables(seq, DIL_ROT, 0, DIL_HEAD_DIM, DIL_HEAD_DIM ** -0.5),
            _rope_lane_tables(seq, DIL_ROT, 0, DIL_HEAD_DIM, 1.0))

    h = x.reshape(T, D_MODEL)
    for i in range(depth):
        wi = w_in[i]
        o1 = MLA_Q_RANK + MLA_KV_RANK
        k_r_cols = jnp.pad(wi[:, o1:o1 + MLA_ROPE], ((0, 0), (MLA_NOPE, LANES - MLA_QK)))
        w_in_p = jnp.concatenate([wi[:, :o1], k_r_cols, wi[:, o1 + MLA_ROPE:]], axis=1).astype(BF16)
        w_uq_p = jnp.pad(w_uq[i].reshape(MLA_Q_RANK, MLA_HEADS, MLA_QK),
                         ((0, 0), (0, 0), (0, LANES - MLA_QK))).reshape(MLA_Q_RANK, -1).astype(BF16)
        w_ukv3 = w_ukv[i].reshape(MLA_KV_RANK, MLA_HEADS, MLA_NOPE + MLA_V)
        w_uk_p = jnp.pad(w_ukv3[:, :, :MLA_NOPE],
                         ((0, 0), (0, 0), (0, LANES - MLA_NOPE))).reshape(MLA_KV_RANK, -1).astype(BF16)
        w_uv = w_ukv3[:, :, MLA_NOPE:].reshape(MLA_KV_RANK, -1).astype(BF16)
        zeros = jnp.zeros((PEER_KEYS, PEER_HALF), F32)
        kdt = jnp.concatenate([jnp.concatenate([peer_keys1[i], zeros], axis=1),
                               jnp.concatenate([zeros, peer_keys2[i]], axis=1)], axis=0).astype(BF16)

        qm, km, vm, qd, kd, vd = _in_proj(
            h, g_mix[i][None], w_in_p, g_cq[i][None], w_uq_p, g_ckv[i][None], w_uk_p, w_uv,
            tabs, seq, _pick_tile(seq, PROJ_TILE))
        o_mla = _mla_attn(qm, km, vm, batch, seq, _pick_tile(seq, MLA_Q_TILE)).reshape(T, mla_w)
        o_dil = _dil_attn(qd, kd, vd, batch, seq).reshape(T, DIL_WIDTH)

        h1, xn, gates = _route(
            o_mla, o_dil, h, g_out_mla[i][None], g_out_dil[i][None],
            w_out[i][:mla_w].astype(BF16), w_out[i][mla_w:].astype(BF16), g_ffn[i][None],
            w_peer_q[i].astype(BF16), kdt, _pick_tile(T, ROUTE_TILE))
        y = _peer(xn, peer_u[i], peer_v[i], gates, _pick_tile(T, PEER_TILE), PEER_A_PER_STEP)
        h = _ple_final(h1, y, p[i].reshape(T, PLE_DIM), g_ple[i][None], w_ple_gate[i].astype(BF16),
                       w_ple_proj[i].astype(BF16), g_final[None], _pick_tile(T, PROJ_TILE),
                       last_layer=(i == depth - 1))
    return h.reshape(batch, seq, D_MODEL)
```

```python
import functools

import numpy as np
import jax
import jax.numpy as jnp
from jax import lax
from jax.experimental import pallas as pl
from jax.experimental.pallas import tpu as pltpu

F32 = jnp.float32
BF16 = jnp.bfloat16

EPS = 1e-6
MASKED_SCORE = -1e30
ROPE_THETA = 500000.0

LANES = 128
D_MODEL = 1024
PLE_DIM = 256
MLA_HEADS = 8
MLA_NOPE = 64
MLA_ROPE = 32
MLA_V = 64
MLA_QK = MLA_NOPE + MLA_ROPE
MLA_Q_RANK = 384
MLA_KV_RANK = 256
DIL_HEADS = 8
DIL_HEAD_DIM = 64
DIL_ROT = 16
DIL_WIDTH = DIL_HEADS * DIL_HEAD_DIM
DIL_PATTERNS = ((128, 1), (512, 4), (2048, 16))
PEER_KEYS = 128
PEER_HEADS = 8
PEER_HALF = 64
PEER_TOPK = 16
PEER_SLOTS = PEER_HEADS * PEER_TOPK
PEER_EXPERTS = PEER_KEYS * PEER_KEYS

VMEM_LIMIT = 48 * 1024 * 1024

PROJ_TILE = 512
MLA_Q_TILE = 1024
MLA_PV_CHUNKS = 2
ROUTE_TILE = 512
PEER_TILE = 1024
PEER_A_PER_STEP = 8
PEER_VMEM_LIMIT = 58 * 1024 * 1024


def _params(semantics):
    return pltpu.CompilerParams(dimension_semantics=semantics, vmem_limit_bytes=VMEM_LIMIT)


def _rms(x, g):
    return x * lax.rsqrt(jnp.mean(x * x, axis=-1, keepdims=True) + EPS) * g


def _rope(x, tab_ref, half):
    return (x * tab_ref[0]
            + pltpu.roll(x, half, 1) * tab_ref[1]
            + pltpu.roll(x, LANES - half, 1) * tab_ref[2])


def _dot(a, b):
    return jnp.dot(a, b, preferred_element_type=F32)


def _dot_nt(a, b):
    return lax.dot_general(a, b, (((1,), (1,)), ((), ())), preferred_element_type=F32)


def _in_proj_kernel(x_ref, gmix_ref, win_ref, gcq_ref, wuq_ref, gckv_ref, wuk_ref, wuv_ref,
                    tmq_ref, tmk_ref, tdq_ref, tdk_ref,
                    qm_ref, km_ref, vm_ref, qd_ref, kd_ref, vd_ref):
    hn = _rms(x_ref[...], gmix_ref[...]).astype(BF16)
    y = _dot(hn, win_ref[...])
    o = 0
    c_q = y[:, o:o + MLA_Q_RANK]; o += MLA_Q_RANK
    c_kv = y[:, o:o + MLA_KV_RANK]; o += MLA_KV_RANK
    k_r = y[:, o:o + LANES]; o += LANES
    q_d = y[:, o:o + DIL_WIDTH]; o += DIL_WIDTH
    k_d = y[:, o:o + DIL_WIDTH]; o += DIL_WIDTH
    v_d = y[:, o:o + DIL_WIDTH]

    q = _dot(_rms(c_q, gcq_ref[...]).astype(BF16), wuq_ref[...])
    ckvn = _rms(c_kv, gckv_ref[...]).astype(BF16)
    k_n = _dot(ckvn, wuk_ref[...])
    vm_ref[...] = _dot(ckvn, wuv_ref[...]).astype(BF16)
    k_rope = _rope(k_r, tmk_ref, MLA_ROPE // 2)
    for h in range(MLA_HEADS):
        sl = slice(h * LANES, (h + 1) * LANES)
        qm_ref[:, sl] = _rope(q[:, sl], tmq_ref, MLA_ROPE // 2).astype(BF16)
        km_ref[:, sl] = (k_n[:, sl] + k_rope).astype(BF16)
    for c in range(DIL_WIDTH // LANES):
        sl = slice(c * LANES, (c + 1) * LANES)
        qd_ref[:, sl] = _rope(q_d[:, sl], tdq_ref, DIL_ROT // 2)
        kd_ref[:, sl] = _rope(k_d[:, sl], tdk_ref, DIL_ROT // 2)
    vd_ref[...] = v_d


def _in_proj(x2, g_mix, w_in_p, g_cq, w_uq_p, g_ckv, w_uk_p, w_uv, tabs, seq, tile):
    T = x2.shape[0]
    n_pos = seq // tile
    full = lambda a: pl.BlockSpec(a.shape, lambda i: (0,) * a.ndim)
    tab = pl.BlockSpec((3, tile, LANES), lambda i: (0, i % n_pos, 0))
    row = lambda w: pl.BlockSpec((tile, w), lambda i: (i, 0))
    outs = [(T, MLA_HEADS * LANES), (T, MLA_HEADS * LANES), (T, MLA_HEADS * MLA_V),
            (T, DIL_WIDTH), (T, DIL_WIDTH), (T, DIL_WIDTH)]
    return pl.pallas_call(
        _in_proj_kernel,
        grid=(T // tile,),
        in_specs=[row(D_MODEL), full(g_mix), full(w_in_p), full(g_cq), full(w_uq_p),
                  full(g_ckv), full(w_uk_p), full(w_uv), tab, tab, tab, tab],
        out_specs=[row(s[1]) for s in outs],
        out_shape=[jax.ShapeDtypeStruct(s, BF16) for s in outs[:3]]
                  + [jax.ShapeDtypeStruct(s, F32) for s in outs[3:]],
        compiler_params=_params(("parallel",)),
        name="in_proj",
    )(x2, g_mix, w_in_p, g_cq, w_uq_p, g_ckv, w_uk_p, w_uv, *tabs)


def _mla_attn_kernel(q_ref, k_ref, v_ref, o_ref):
    v = v_ref[...]
    lane = lax.broadcasted_iota(jnp.int32, (q_ref.shape[0], LANES), 1)
    out = None
    for h in range(2):
        sl = slice(h * LANES, (h + 1) * LANES)
        s = _dot_nt(q_ref[:, sl], k_ref[:, sl])
        e = jnp.exp(s - jnp.max(s, axis=-1, keepdims=True))
        p = e.astype(BF16)
        rows = p.shape[0] // MLA_PV_CHUNKS
        o = jnp.concatenate([_dot(p[c * rows:(c + 1) * rows], v) for c in range(MLA_PV_CHUNKS)],
                            axis=0) / jnp.sum(e, axis=-1, keepdims=True)
        out = o if out is None else jnp.where(lane < MLA_V, out, o)
    o_ref[...] = out


def _mla_attn(qm, km, vm, batch, seq, tq):
    q3 = qm.reshape(batch, seq, MLA_HEADS * LANES)
    k3 = km.reshape(batch, seq, MLA_HEADS * LANES)
    v3 = vm.reshape(batch, seq, MLA_HEADS * MLA_V)
    return pl.pallas_call(
        _mla_attn_kernel,
        grid=(batch, MLA_HEADS // 2, seq // tq),
        in_specs=[pl.BlockSpec((None, tq, 2 * LANES), lambda b, p, i: (b, i, p)),
                  pl.BlockSpec((None, seq, 2 * LANES), lambda b, p, i: (b, 0, p)),
                  pl.BlockSpec((None, seq, LANES), lambda b, p, i: (b, 0, p))],
        out_specs=pl.BlockSpec((None, tq, LANES), lambda b, p, i: (b, i, p)),
        out_shape=jax.ShapeDtypeStruct((batch, seq, MLA_HEADS * MLA_V), F32),
        compiler_params=_params(("parallel", "parallel", "parallel")),
        name="mla_attn",
    )(q3, k3, v3)


DIL_Q_BLOCK = 128
DIL_K_WINDOW = 256


def _dil_attn_kernel(q_ref, k_ref, v_ref, o_ref, acc_ref, m_ref, l_ref):
    seq = q_ref.shape[0]
    lane = lax.broadcasted_iota(jnp.int32, (1, 1, LANES), 2)
    head0 = lane < DIL_HEAD_DIM

    for p, (window, dil) in enumerate(DIL_PATTERNS):
        sub_len = seq // dil
        radius = window // (2 * dil)
        qb = min(DIL_Q_BLOCK, sub_len)
        kw = min(DIL_K_WINDOW, sub_len)
        assert kw == sub_len or kw >= qb + 2 * radius
        blocks = sub_len // qb
        rel = (lax.broadcasted_iota(jnp.int32, (qb, kw), 0)
               - lax.broadcasted_iota(jnp.int32, (qb, kw), 1))

        def rows(first, count, dil=dil):
            return pl.ds(first, count) if dil == 1 else pl.ds(first, count, stride=dil)

        q_rows, k_rows, shifts = [], [], []
        for res in range(dil):
            for jb in range(blocks):
                j0 = jb * qb
                k0 = min(max(j0 - (kw - qb) // 2, 0), sub_len - kw)
                q_rows.append(rows(res + dil * j0, qb))
                k_rows.append(rows(res + dil * k0, kw))
                shifts.append(j0 - k0)
        near_by_shift = {s: jnp.abs(rel + s) <= radius for s in sorted(set(shifts))}
        q = jnp.stack([q_ref[r, :] for r in q_rows])
        k = jnp.stack([k_ref[r, :] for r in k_rows]).astype(BF16)
        v = jnp.stack([v_ref[r, :] for r in k_rows]).astype(BF16)

        acc, m_all, l_all = None, None, None
        for h in range(2):
            qh = jnp.where(head0 if h == 0 else ~head0, q, 0.0).astype(BF16)
            s = lax.dot_general(qh, k, (((2,), (2,)), ((0,), (0,))), preferred_element_type=F32)
            s = jnp.stack([jnp.where(near_by_shift[shift], s[n], MASKED_SCORE)
                           for n, shift in enumerate(shifts)])
            m = jnp.max(s, axis=-1, keepdims=True)
            e = jnp.exp(s - m)
            l = jnp.sum(e, axis=-1, keepdims=True)
            a = lax.dot_general(e.astype(BF16), v, (((2,), (1,)), ((0,), (0,))),
                                preferred_element_type=F32)
            if h == 0:
                acc, m_all, l_all = a, m, l
            else:
                acc = jnp.where(head0, acc, a)
                m_all = jnp.where(head0, m_all, m)
                l_all = jnp.where(head0, l_all, l)
        for n, r in enumerate(q_rows):
            acc_ref[p, r, :] = acc[n]
            m_ref[p, r, :] = m_all[n]
            l_ref[p, r, :] = l_all[n]

    m = m_ref[...]
    w = jnp.exp(m - jnp.max(m, axis=0, keepdims=True))
    o_ref[...] = jnp.sum(w * acc_ref[...], axis=0) / jnp.sum(w * l_ref[...], axis=0)


def _dil_attn(qd, kd, vd, batch, seq):
    spec = pl.BlockSpec((None, seq, LANES), lambda b, p: (b, 0, p))
    stats = pltpu.VMEM((len(DIL_PATTERNS), seq, LANES), F32)
    return pl.pallas_call(
        _dil_attn_kernel,
        grid=(batch, DIL_HEADS // 2),
        in_specs=[spec, spec, spec],
        out_specs=spec,
        out_shape=jax.ShapeDtypeStruct((batch, seq, DIL_WIDTH), F32),
        scratch_shapes=[stats, stats, stats],
        compiler_params=_params(("parallel", "parallel")),
        name="dil_attn",
    )(qd.reshape(batch, seq, DIL_WIDTH), kd.reshape(batch, seq, DIL_WIDTH),
      vd.reshape(batch, seq, DIL_WIDTH))


class _Ranked:
    def __init__(self, r8):
        self.r8 = r8
        self.rows = []
        self.lo = jnp.zeros(r8.shape, F32)
        self.hi = jnp.zeros(r8.shape, F32)

    def push(self, row):
        k = len(self.rows)
        self.rows.append(row)
        if k < 8:
            self.lo = jnp.where(self.r8 == k, row, self.lo)
        else:
            self.hi = jnp.where(self.r8 == k - 8, row, self.hi)


KEY_STACK = 8


def _top16(blocks, r8):
    groups = []
    for g in range(len(blocks) // KEY_STACK):
        vs = [blocks[g * KEY_STACK + d] for d in range(KEY_STACK)]
        ids = [r8 + float(8 * (g * KEY_STACK + d)) for d in range(KEY_STACK)]
        for a in [a for r in range(KEY_STACK) for a in range(r % 2, KEY_STACK - 1, 2)]:
            swap = vs[a + 1] > vs[a]
            vs[a], vs[a + 1] = jnp.where(swap, vs[a + 1], vs[a]), jnp.where(swap, vs[a], vs[a + 1])
            ids[a], ids[a + 1] = jnp.where(swap, ids[a + 1], ids[a]), jnp.where(swap, ids[a], ids[a + 1])
        groups.append((vs, ids))

    tv, ti = _Ranked(r8), _Ranked(r8)
    for k in range(PEER_TOPK):
        m = jnp.max(functools.reduce(jnp.maximum, [vs[0] for vs, _ in groups]),
                    axis=0, keepdims=True)
        low = functools.reduce(jnp.minimum, [jnp.where(vs[0] == m, ids[0], float(PEER_KEYS))
                                            for vs, ids in groups])
        idx = jnp.min(low, axis=0, keepdims=True)
        tv.push(m)
        ti.push(idx)
        live = min(KEY_STACK, PEER_TOPK - 1 - k)
        for vs, ids in groups:
            if live == 0:
                break
            hit = ids[0] == idx
            for d in range(live):
                below = vs[d + 1] if d + 1 < KEY_STACK else -jnp.inf
                vs[d] = jnp.where(hit, below, vs[d])
                if d + 1 < KEY_STACK:
                    ids[d] = jnp.where(hit, ids[d + 1], ids[d])
    return tv, ti


_POS_SHIFT = float(PEER_EXPERTS)


def _pair_top16(v1, i1, v2, i2, r8):
    ninf = -jnp.inf
    big = 3.0e38
    lo_v, lo_k = [], []
    for j in range(PEER_TOPK):
        lists = PEER_TOPK // (j + 1)
        val = v1.lo + v2.rows[j]
        lo_v.append(val if lists >= 8 else jnp.where(r8 < lists, val, ninf))
        lo_k.append((r8 * 16.0 + j) * _POS_SHIFT + i1.lo * PEER_KEYS + i2.rows[j])
    hi_v = v1.hi + v2.rows[0]
    hi_k = (r8 * 16.0 + 128.0) * _POS_SHIFT + i1.hi * PEER_KEYS + i2.rows[0]

    top, sel = _Ranked(r8), _Ranked(r8)
    for k in range(PEER_TOPK):
        m = jnp.max(jnp.maximum(lo_v[0], hi_v), axis=0, keepdims=True)
        kk = jnp.min(jnp.minimum(jnp.where(lo_v[0] == m, lo_k[0], big),
                                 jnp.where(hi_v == m, hi_k, big)), axis=0, keepdims=True)
        top.push(m)
        sel.push(kk)
        live = PEER_TOPK - 1 - k
        if live:
            hit = lo_k[0] == kk
            for d in range(live):
                lo_v[d] = jnp.where(hit, lo_v[d + 1], lo_v[d])
                lo_k[d] = jnp.where(hit, lo_k[d + 1], lo_k[d])
            hi_v = jnp.where(hi_k == kk, ninf, hi_v)
    return top, sel


GATE_PITCH = PEER_KEYS + 8
GATE_BATCH = 8


def _build_gates(slot_ref, first, count, stage_ref, g_ref, g_first):
    key_iota = lax.broadcasted_iota(jnp.int32, (1, PEER_KEYS, PEER_SLOTS), 1).astype(F32)
    for t0 in range(0, count, GATE_BATCH):
        rows = pl.ds(first + t0, GATE_BATCH)
        a = slot_ref[0, rows, :][:, None, :]
        b = slot_ref[1, rows, :][:, None, :]
        g = slot_ref[2, rows, :][:, None, :]
        one_a = jnp.where(key_iota == a, 1.0, 0.0).astype(BF16)
        g_at_b = jnp.where(key_iota == b, g, 0.0).astype(BF16)
        gates = lax.dot_general(one_a, g_at_b, (((2,), (2,)), ((0,), (0,))),
                                preferred_element_type=F32)
        for t in range(GATE_BATCH):
            r0 = (t0 + t) * GATE_PITCH
            stage_ref[r0:r0 + PEER_KEYS, :] = gates[t]
    for a in range(PEER_KEYS):
        g_ref[a, pl.ds(g_first, count), :] = stage_ref[
            pl.ds(a, count, stride=GATE_PITCH), :].astype(BF16)


def _rank_unit(sc_ref, unit, r8, ranked_ref):
    def key_blocks(first):
        return [sc_ref[unit, first + 8 * d:first + 8 * d + 8, :] for d in range(PEER_KEYS // 8)]

    v1, i1 = _top16(key_blocks(0), r8)
    v2, i2 = _top16(key_blocks(PEER_KEYS), r8)
    top, sel = _pair_top16(v1, i1, v2, i2, r8)
    e_lo = jnp.exp(top.lo - top.rows[0])
    e_hi = jnp.exp(top.hi - top.rows[0])
    inv = 1.0 / (jnp.sum(e_lo, axis=0, keepdims=True) + jnp.sum(e_hi, axis=0, keepdims=True))
    for half, (s, e) in enumerate(((sel.lo, e_lo), (sel.hi, e_hi))):
        rows = slice(half * 8, half * 8 + 8)
        expert = s - jnp.floor(s * (1.0 / _POS_SHIFT)) * _POS_SHIFT
        a = jnp.floor(expert * (1.0 / PEER_KEYS))
        ranked_ref[0, unit, rows, :] = a
        ranked_ref[1, unit, rows, :] = expert - a * PEER_KEYS
        ranked_ref[2, unit, rows, :] = e * inv


def _route_kernel(om_ref, od_ref, x_ref, gom_ref, god_ref, wom_ref, wod_ref, gffn_ref,
                  wq_ref, kdt_ref,
                  h_ref, xn_ref, sc_ref):
    tile = x_ref.shape[0]
    nm = _rms(om_ref[...], gom_ref[...]).astype(BF16)
    nd = _rms(od_ref[...], god_ref[...]).astype(BF16)
    h = x_ref[...] + _dot(nm, wom_ref[...]) + _dot(nd, wod_ref[...])
    h_ref[...] = h
    xn = _rms(h, gffn_ref[...]).astype(BF16)
    xn_ref[...] = xn
    q = _dot(xn, wq_ref[...])
    kdt = kdt_ref[...]
    for c in range(tile // LANES):
        for hd in range(PEER_HEADS):
            qh = q[c * LANES:(c + 1) * LANES, hd * LANES:(hd + 1) * LANES].astype(BF16)
            sc_ref[c * PEER_HEADS + hd] = _dot_nt(kdt, qh)


def _route(o_mla, o_dil, x2, g_om, g_od, w_om, w_od, g_ffn, w_q, kdt, tile):
    T = x2.shape[0]
    units = PEER_HEADS * (tile // LANES)
    full = lambda a: pl.BlockSpec(a.shape, lambda i: (0,) * a.ndim)
    row = lambda w: pl.BlockSpec((tile, w), lambda i: (i, 0))
    return pl.pallas_call(
        _route_kernel,
        grid=(T // tile,),
        in_specs=[row(o_mla.shape[1]), row(o_dil.shape[1]), row(D_MODEL), full(g_om), full(g_od),
                  full(w_om), full(w_od), full(g_ffn), full(w_q), full(kdt)],
        out_specs=[row(D_MODEL), row(D_MODEL),
                   pl.BlockSpec((units, 2 * PEER_KEYS, LANES), lambda i: (i, 0, 0))],
        out_shape=[jax.ShapeDtypeStruct((T, D_MODEL), F32),
                   jax.ShapeDtypeStruct((T, D_MODEL), BF16),
                   jax.ShapeDtypeStruct((T // LANES * PEER_HEADS, 2 * PEER_KEYS, LANES), F32)],
        compiler_params=_params(("parallel",)),
        name="route",
    )(o_mla, o_dil, x2, g_om, g_od, w_om, w_od, g_ffn, w_q, kdt)


def _peer_kernel(*refs, do_peer, do_gates, do_rank, a_per_step, units_per_step, tokens_per_step):
    refs = list(refs)
    if do_peer:
        xn_ref, u_ref, v_ref, g_ref = refs[:4]
        del refs[:4]
    if do_gates:
        ranked_in_ref = refs.pop(0)
    if do_rank:
        sc_ref = refs.pop(0)
    if do_peer:
        o_ref = refs.pop(0)
    if do_gates:
        g_out_ref = refs.pop(0)
    if do_rank:
        ranked_out_ref = refs.pop(0)
    if do_gates:
        slots_ref, stage_ref = refs
    j = pl.program_id(0)

    if do_peer:
        pre = _dot_nt(xn_ref[...], u_ref[...].astype(BF16))
        act = 0.5 * pre * (1.0 + lax.erf(pre * float(1.0 / np.sqrt(2.0))))
        gate = jnp.concatenate([g_ref[a] for a in range(a_per_step)], axis=1)
        y = _dot((act * gate.astype(F32)).astype(BF16), v_ref[...].astype(BF16))

    if do_rank:
        r8 = lax.broadcasted_iota(jnp.int32, (8, LANES), 0).astype(F32)
        for unit in range(units_per_step):
            _rank_unit(sc_ref, unit, r8, ranked_out_ref)

    if do_gates:
        token0 = j * tokens_per_step
        column = token0 // LANES
        for k in range(3):
            by_slot = jnp.concatenate(
                [ranked_in_ref[k, column * PEER_HEADS + hd] for hd in range(PEER_HEADS)], axis=0)
            slots_ref[k] = by_slot.T
        first = pl.multiple_of(token0 % LANES, tokens_per_step)
        _build_gates(slots_ref, first, tokens_per_step, stage_ref, g_out_ref, 0)

    if do_peer:
        @pl.when(j == 0)
        def _():
            o_ref[...] = y

        @pl.when(j != 0)
        def _():
            o_ref[...] += y


def _peer_call(t, n_tiles, xn, u, v, gates, ranked, scores, tile, a_per_step):
    do_peer = 0 <= t
    do_gates = 0 <= t + 1 < n_tiles
    do_rank = t + 2 < n_tiles
    et = a_per_step * PEER_KEYS
    steps = PEER_EXPERTS // et
    units = tile // LANES * PEER_HEADS
    units_per_step = units // steps
    tokens_per_step = tile // steps
    assert units_per_step * steps == units and tokens_per_step % GATE_BATCH == 0
    ins, in_specs, out_specs, out_shape, scratch = [], [], [], [], []
    if do_peer:
        ins += [xn, u, v, gates]
        in_specs += [pl.BlockSpec((tile, D_MODEL), lambda j: (t, 0)),
                     pl.BlockSpec((et, D_MODEL), lambda j: (j, 0)),
                     pl.BlockSpec((et, D_MODEL), lambda j: (j, 0)),
                     pl.BlockSpec((a_per_step, tile, PEER_KEYS), lambda j: (j, 0, 0))]
        out_specs.append(pl.BlockSpec((tile, D_MODEL), lambda j: (0, 0)))
        out_shape.append(jax.ShapeDtypeStruct((tile, D_MODEL), F32))
    if do_gates:
        ins.append(ranked)
        in_specs.append(pl.BlockSpec(ranked.shape, lambda j: (0, 0, 0, 0)))
        out_specs.append(pl.BlockSpec((PEER_KEYS, tokens_per_step, PEER_KEYS), lambda j: (0, j, 0)))
        out_shape.append(jax.ShapeDtypeStruct((PEER_KEYS, tile, PEER_KEYS), BF16))
        scratch += [pltpu.VMEM((3, LANES, PEER_SLOTS), F32),
                    pltpu.VMEM((tokens_per_step * GATE_PITCH, PEER_KEYS), F32)]
    if do_rank:
        ins.append(scores)
        first_block = (t + 2) * steps
        in_specs.append(pl.BlockSpec((units_per_step, 2 * PEER_KEYS, LANES),
                                     lambda j: (first_block + j, 0, 0)))
        out_specs.append(pl.BlockSpec((3, units_per_step, PEER_TOPK, LANES), lambda j: (0, j, 0, 0)))
        out_shape.append(jax.ShapeDtypeStruct((3, units, PEER_TOPK, LANES), F32))
    outs = pl.pallas_call(
        functools.partial(_peer_kernel, do_peer=do_peer, do_gates=do_gates, do_rank=do_rank,
                          a_per_step=a_per_step, units_per_step=units_per_step,
                          tokens_per_step=tokens_per_step),
        grid=(steps,),
        in_specs=in_specs,
        out_specs=out_specs,
        out_shape=out_shape,
        scratch_shapes=scratch,
        compiler_params=pltpu.CompilerParams(dimension_semantics=("arbitrary",),
                                             vmem_limit_bytes=PEER_VMEM_LIMIT),
        name="peer",
    )(*ins)
    outs = list(outs)
    y = outs.pop(0) if do_peer else None
    gates_next = outs.pop(0) if do_gates else None
    ranked_next = outs.pop(0) if do_rank else None
    return y, gates_next, ranked_next


def _ple_final_kernel(h_ref, y_ref, p_ref, gple_ref, wg_ref, wp_ref, gfin_ref, o_ref, *, last_layer):
    h = h_ref[...] + y_ref[...]
    gate = jax.nn.sigmoid(_dot(_rms(h, gple_ref[...]).astype(BF16), wg_ref[...]))
    h = h + gate * _dot(p_ref[...].astype(BF16), wp_ref[...])
    o_ref[...] = _rms(h, gfin_ref[...]) if last_layer else h


def _ple_final(h1, y, p2, g_ple, w_gate, w_proj, g_final, tile, last_layer):
    T = h1.shape[0]
    full = lambda a: pl.BlockSpec(a.shape, lambda i: (0,) * a.ndim)
    row = lambda w: pl.BlockSpec((tile, w), lambda i: (i, 0))
    return pl.pallas_call(
        functools.partial(_ple_final_kernel, last_layer=last_layer),
        grid=(T // tile,),
        in_specs=[row(D_MODEL), row(D_MODEL), row(PLE_DIM), full(g_ple), full(w_gate),
                  full(w_proj), full(g_final)],
        out_specs=row(D_MODEL),
        out_shape=jax.ShapeDtypeStruct((T, D_MODEL), F32),
        compiler_params=_params(("parallel",)),
        name="ple_final",
    )(h1, y, p2, g_ple, w_gate, w_proj, g_final)


def _rope_lane_tables(seq, rot_dim, first_lane, period, scale):
    half = rot_dim // 2
    inv = ROPE_THETA ** (-jnp.arange(0, rot_dim, 2, dtype=F32) / rot_dim)
    ang = jnp.arange(seq, dtype=F32)[:, None] * inv[None, :]
    cos, sin = jnp.cos(ang), jnp.sin(ang)
    c = jnp.ones((seq, period), F32)
    c = c.at[:, first_lane:first_lane + half].set(cos).at[:, first_lane + half:first_lane + rot_dim].set(cos)
    s_lo = jnp.zeros((seq, period), F32).at[:, first_lane + half:first_lane + rot_dim].set(sin)
    s_hi = jnp.zeros((seq, period), F32).at[:, first_lane:first_lane + half].set(-sin)
    tabs = jnp.stack([c, s_lo, s_hi]) * scale
    return jnp.tile(tabs, (1, 1, LANES // period))


def _pick_tile(n, want):
    t = min(n, want)
    assert n % t == 0, (n, t)
    return t


def kernel(x, p, g_mix, w_in, g_cq, w_uq, g_ckv, w_ukv, g_out_mla, g_out_dil, w_out, g_ffn,
           w_peer_q, peer_keys1, peer_keys2, peer_u, peer_v, g_ple, w_ple_gate, w_ple_proj, g_final):
    batch, seq, _ = x.shape
    depth = p.shape[0]
    T = batch * seq
    mla_w = MLA_HEADS * MLA_V

    tabs = (_rope_lane_tables(seq, MLA_ROPE, MLA_NOPE, LANES, MLA_QK ** -0.5),
            _rope_lane_tables(seq, MLA_ROPE, MLA_NOPE, LANES, 1.0),
            _rope_lane_tables(seq, DIL_ROT, 0, DIL_HEAD_DIM, DIL_HEAD_DIM ** -0.5),
            _rope_lane_tables(seq, DIL_ROT, 0, DIL_HEAD_DIM, 1.0))

    h = x.reshape(T, D_MODEL)
    for i in range(depth):
        wi = w_in[i]
        o1 = MLA_Q_RANK + MLA_KV_RANK
        k_r_cols = jnp.pad(wi[:, o1:o1 + MLA_ROPE], ((0, 0), (MLA_NOPE, LANES - MLA_QK)))
        w_in_p = jnp.concatenate([wi[:, :o1], k_r_cols, wi[:, o1 + MLA_ROPE:]], axis=1).astype(BF16)
        w_uq_p = jnp.pad(w_uq[i].reshape(MLA_Q_RANK, MLA_HEADS, MLA_QK),
                         ((0, 0), (0, 0), (0, LANES - MLA_QK))).reshape(MLA_Q_RANK, -1).astype(BF16)
        w_ukv3 = w_ukv[i].reshape(MLA_KV_RANK, MLA_HEADS, MLA_NOPE + MLA_V)
        w_uk_p = jnp.pad(w_ukv3[:, :, :MLA_NOPE],
                         ((0, 0), (0, 0), (0, LANES - MLA_NOPE))).reshape(MLA_KV_RANK, -1).astype(BF16)
        w_uv = w_ukv3[:, :, MLA_NOPE:].reshape(MLA_KV_RANK, -1).astype(BF16)
        zeros = jnp.zeros((PEER_KEYS, PEER_HALF), F32)
        kdt = jnp.concatenate([jnp.concatenate([peer_keys1[i], zeros], axis=1),
                               jnp.concatenate([zeros, peer_keys2[i]], axis=1)], axis=0).astype(BF16)

        qm, km, vm, qd, kd, vd = _in_proj(
            h, g_mix[i][None], w_in_p, g_cq[i][None], w_uq_p, g_ckv[i][None], w_uk_p, w_uv,
            tabs, seq, _pick_tile(seq, PROJ_TILE))
        o_mla = _mla_attn(qm, km, vm, batch, seq, _pick_tile(seq, MLA_Q_TILE)).reshape(T, mla_w)
        o_dil = _dil_attn(qd, kd, vd, batch, seq).reshape(T, DIL_WIDTH)

        h1, xn, scores = _route(
            o_mla, o_dil, h, g_out_mla[i][None], g_out_dil[i][None],
            w_out[i][:mla_w].astype(BF16), w_out[i][mla_w:].astype(BF16), g_ffn[i][None],
            w_peer_q[i].astype(BF16), kdt, _pick_tile(T, ROUTE_TILE))
        peer_tile = _pick_tile(T, PEER_TILE)
        n_tiles = T // peer_tile
        gates, ranked, ys = None, None, []
        for t in range(-2, n_tiles):
            y_t, gates_next, ranked_next = _peer_call(
                t, n_tiles, xn, peer_u[i], peer_v[i], gates, ranked, scores, peer_tile,
                PEER_A_PER_STEP)
            if y_t is not None:
                ys.append(y_t)
            gates, ranked = gates_next, ranked_next
        y = jnp.concatenate(ys, axis=0)
        h = _ple_final(h1, y, p[i].reshape(T, PLE_DIM), g_ple[i][None], w_ple_gate[i].astype(BF16),
                       w_ple_proj[i].astype(BF16), g_final[None], _pick_tile(T, PROJ_TILE),
                       last_layer=(i == depth - 1))
    return h.reshape(batch, seq, D_MODEL)
```

```python
import functools

import numpy as np
import jax
import jax.numpy as jnp
from jax import lax
from jax.experimental import pallas as pl
from jax.experimental.pallas import tpu as pltpu

F32 = jnp.float32
BF16 = jnp.bfloat16

EPS = 1e-6
MASKED_SCORE = -1e30
ROPE_THETA = 500000.0

LANES = 128
D_MODEL = 1024
PLE_DIM = 256
MLA_HEADS = 8
MLA_NOPE = 64
MLA_ROPE = 32
MLA_V = 64
MLA_QK = MLA_NOPE + MLA_ROPE
MLA_Q_RANK = 384
MLA_KV_RANK = 256
DIL_HEADS = 8
DIL_HEAD_DIM = 64
DIL_ROT = 16
DIL_WIDTH = DIL_HEADS * DIL_HEAD_DIM
DIL_PATTERNS = ((128, 1), (512, 4), (2048, 16))
PEER_KEYS = 128
PEER_HEADS = 8
PEER_HALF = 64
PEER_TOPK = 16
PEER_SLOTS = PEER_HEADS * PEER_TOPK
PEER_EXPERTS = PEER_KEYS * PEER_KEYS

VMEM_LIMIT = 48 * 1024 * 1024

PROJ_TILE = 512
MLA_Q_TILE = 2048
MLA_PV_CHUNKS = 2
ROUTE_TILE = 256
PEER_TILE = 1024
PEER_A_PER_STEP = 8


def _params(semantics):
    return pltpu.CompilerParams(dimension_semantics=semantics, vmem_limit_bytes=VMEM_LIMIT)


def _rms(x, g):
    return x * lax.rsqrt(jnp.mean(x * x, axis=-1, keepdims=True) + EPS) * g


def _rope(x, tab_ref, half):
    return (x * tab_ref[0]
            + pltpu.roll(x, half, 1) * tab_ref[1]
            + pltpu.roll(x, LANES - half, 1) * tab_ref[2])


def _dot(a, b):
    return jnp.dot(a, b, preferred_element_type=F32)


def _dot_nt(a, b):
    return lax.dot_general(a, b, (((1,), (1,)), ((), ())), preferred_element_type=F32)


def _in_proj_kernel(x_ref, gmix_ref, win_ref, gcq_ref, wuq_ref, gckv_ref, wuk_ref, wuv_ref,
                    tmq_ref, tmk_ref, tdq_ref, tdk_ref,
                    qm_ref, km_ref, vm_ref, qd_ref, kd_ref, vd_ref):
    hn = _rms(x_ref[...], gmix_ref[...]).astype(BF16)
    y = _dot(hn, win_ref[...])
    o = 0
    c_q = y[:, o:o + MLA_Q_RANK]; o += MLA_Q_RANK
    c_kv = y[:, o:o + MLA_KV_RANK]; o += MLA_KV_RANK
    k_r = y[:, o:o + LANES]; o += LANES
    q_d = y[:, o:o + DIL_WIDTH]; o += DIL_WIDTH
    k_d = y[:, o:o + DIL_WIDTH]; o += DIL_WIDTH
    v_d = y[:, o:o + DIL_WIDTH]

    q = _dot(_rms(c_q, gcq_ref[...]).astype(BF16), wuq_ref[...])
    ckvn = _rms(c_kv, gckv_ref[...]).astype(BF16)
    k_n = _dot(ckvn, wuk_ref[...])
    vm_ref[...] = _dot(ckvn, wuv_ref[...]).astype(BF16)
    k_rope = _rope(k_r, tmk_ref, MLA_ROPE // 2)
    for h in range(MLA_HEADS):
        sl = slice(h * LANES, (h + 1) * LANES)
        qm_ref[:, sl] = _rope(q[:, sl], tmq_ref, MLA_ROPE // 2).astype(BF16)
        km_ref[:, sl] = (k_n[:, sl] + k_rope).astype(BF16)
    for c in range(DIL_WIDTH // LANES):
        sl = slice(c * LANES, (c + 1) * LANES)
        qd_ref[:, sl] = _rope(q_d[:, sl], tdq_ref, DIL_ROT // 2)
        kd_ref[:, sl] = _rope(k_d[:, sl], tdk_ref, DIL_ROT // 2)
    vd_ref[...] = v_d


def _in_proj(x2, g_mix, w_in_p, g_cq, w_uq_p, g_ckv, w_uk_p, w_uv, tabs, seq, tile):
    T = x2.shape[0]
    n_pos = seq // tile
    full = lambda a: pl.BlockSpec(a.shape, lambda i: (0,) * a.ndim)
    tab = pl.BlockSpec((3, tile, LANES), lambda i: (0, i % n_pos, 0))
    row = lambda w: pl.BlockSpec((tile, w), lambda i: (i, 0))
    outs = [(T, MLA_HEADS * LANES), (T, MLA_HEADS * LANES), (T, MLA_HEADS * MLA_V),
            (T, DIL_WIDTH), (T, DIL_WIDTH), (T, DIL_WIDTH)]
    return pl.pallas_call(
        _in_proj_kernel,
        grid=(T // tile,),
        in_specs=[row(D_MODEL), full(g_mix), full(w_in_p), full(g_cq), full(w_uq_p),
                  full(g_ckv), full(w_uk_p), full(w_uv), tab, tab, tab, tab],
        out_specs=[row(s[1]) for s in outs],
        out_shape=[jax.ShapeDtypeStruct(s, BF16) for s in outs[:3]]
                  + [jax.ShapeDtypeStruct(s, F32) for s in outs[3:]],
        compiler_params=_params(("parallel",)),
        name="in_proj",
    )(x2, g_mix, w_in_p, g_cq, w_uq_p, g_ckv, w_uk_p, w_uv, *tabs)


def _mla_attn_kernel(q_ref, k_ref, v_ref, o_ref):
    v = v_ref[...]
    lane = lax.broadcasted_iota(jnp.int32, (q_ref.shape[0], LANES), 1)
    out = None
    for h in range(2):
        sl = slice(h * LANES, (h + 1) * LANES)
        s = _dot_nt(q_ref[:, sl], k_ref[:, sl])
        e = jnp.exp(s - jnp.max(s, axis=-1, keepdims=True))
        p = e.astype(BF16)
        rows = p.shape[0] // MLA_PV_CHUNKS
        o = jnp.concatenate([_dot(p[c * rows:(c + 1) * rows], v) for c in range(MLA_PV_CHUNKS)],
                            axis=0) / jnp.sum(e, axis=-1, keepdims=True)
        out = o if out is None else jnp.where(lane < MLA_V, out, o)
    o_ref[...] = out


def _mla_attn(qm, km, vm, batch, seq, tq):
    q3 = qm.reshape(batch, seq, MLA_HEADS * LANES)
    k3 = km.reshape(batch, seq, MLA_HEADS * LANES)
    v3 = vm.reshape(batch, seq, MLA_HEADS * MLA_V)
    return pl.pallas_call(
        _mla_attn_kernel,
        grid=(batch, MLA_HEADS // 2, seq // tq),
        in_specs=[pl.BlockSpec((None, tq, 2 * LANES), lambda b, p, i: (b, i, p)),
                  pl.BlockSpec((None, seq, 2 * LANES), lambda b, p, i: (b, 0, p)),
                  pl.BlockSpec((None, seq, LANES), lambda b, p, i: (b, 0, p))],
        out_specs=pl.BlockSpec((None, tq, LANES), lambda b, p, i: (b, i, p)),
        out_shape=jax.ShapeDtypeStruct((batch, seq, MLA_HEADS * MLA_V), F32),
        compiler_params=_params(("parallel", "parallel", "parallel")),
        name="mla_attn",
    )(q3, k3, v3)


DIL_Q_BLOCK = 128
DIL_K_WINDOW = 256


def _dil_attn_kernel(q_ref, k_ref, v_ref, o_ref, acc_ref, m_ref, l_ref):
    seq = q_ref.shape[0]
    lane = lax.broadcasted_iota(jnp.int32, (1, 1, LANES), 2)
    head0 = lane < DIL_HEAD_DIM

    for p, (window, dil) in enumerate(DIL_PATTERNS):
        sub_len = seq // dil
        radius = window // (2 * dil)
        qb = min(DIL_Q_BLOCK, sub_len)
        kw = min(DIL_K_WINDOW, sub_len)
        assert kw == sub_len or kw >= qb + 2 * radius
        blocks = sub_len // qb
        rel = (lax.broadcasted_iota(jnp.int32, (qb, kw), 0)
               - lax.broadcasted_iota(jnp.int32, (qb, kw), 1))

        def rows(first, count, dil=dil):
            return pl.ds(first, count) if dil == 1 else pl.ds(first, count, stride=dil)

        q_rows, k_rows, shifts = [], [], []
        for res in range(dil):
            for jb in range(blocks):
                j0 = jb * qb
                k0 = min(max(j0 - (kw - qb) // 2, 0), sub_len - kw)
                q_rows.append(rows(res + dil * j0, qb))
                k_rows.append(rows(res + dil * k0, kw))
                shifts.append(j0 - k0)
        near_by_shift = {s: jnp.abs(rel + s) <= radius for s in sorted(set(shifts))}
        q = jnp.stack([q_ref[r, :] for r in q_rows])
        k = jnp.stack([k_ref[r, :] for r in k_rows]).astype(BF16)
        v = jnp.stack([v_ref[r, :] for r in k_rows]).astype(BF16)

        acc, m_all, l_all = None, None, None
        for h in range(2):
            qh = jnp.where(head0 if h == 0 else ~head0, q, 0.0).astype(BF16)
            s = lax.dot_general(qh, k, (((2,), (2,)), ((0,), (0,))), preferred_element_type=F32)
            s = jnp.stack([jnp.where(near_by_shift[shift], s[n], MASKED_SCORE)
                           for n, shift in enumerate(shifts)])
            m = jnp.max(s, axis=-1, keepdims=True)
            e = jnp.exp(s - m)
            l = jnp.sum(e, axis=-1, keepdims=True)
            a = lax.dot_general(e.astype(BF16), v, (((2,), (1,)), ((0,), (0,))),
                                preferred_element_type=F32)
            if h == 0:
                acc, m_all, l_all = a, m, l
            else:
                acc = jnp.where(head0, acc, a)
                m_all = jnp.where(head0, m_all, m)
                l_all = jnp.where(head0, l_all, l)
        for n, r in enumerate(q_rows):
            acc_ref[p, r, :] = acc[n]
            m_ref[p, r, :] = m_all[n]
            l_ref[p, r, :] = l_all[n]

    m = m_ref[...]
    w = jnp.exp(m - jnp.max(m, axis=0, keepdims=True))
    o_ref[...] = jnp.sum(w * acc_ref[...], axis=0) / jnp.sum(w * l_ref[...], axis=0)


def _dil_attn(qd, kd, vd, batch, seq):
    spec = pl.BlockSpec((None, seq, LANES), lambda b, p: (b, 0, p))
    stats = pltpu.VMEM((len(DIL_PATTERNS), seq, LANES), F32)
    return pl.pallas_call(
        _dil_attn_kernel,
        grid=(batch, DIL_HEADS // 2),
        in_specs=[spec, spec, spec],
        out_specs=spec,
        out_shape=jax.ShapeDtypeStruct((batch, seq, DIL_WIDTH), F32),
        scratch_shapes=[stats, stats, stats],
        compiler_params=_params(("parallel", "parallel")),
        name="dil_attn",
    )(qd.reshape(batch, seq, DIL_WIDTH), kd.reshape(batch, seq, DIL_WIDTH),
      vd.reshape(batch, seq, DIL_WIDTH))


class _Ranked:
    def __init__(self, r8):
        self.r8 = r8
        self.rows = []
        self.lo = jnp.zeros(r8.shape, F32)
        self.hi = jnp.zeros(r8.shape, F32)

    def push(self, row):
        k = len(self.rows)
        self.rows.append(row)
        if k < 8:
            self.lo = jnp.where(self.r8 == k, row, self.lo)
        else:
            self.hi = jnp.where(self.r8 == k - 8, row, self.hi)


KEY_STACK = 8


def _top16(blocks, r8):
    groups = []
    for g in range(len(blocks) // KEY_STACK):
        vs = [blocks[g * KEY_STACK + d] for d in range(KEY_STACK)]
        ids = [r8 + float(8 * (g * KEY_STACK + d)) for d in range(KEY_STACK)]
        for a in [a for r in range(KEY_STACK) for a in range(r % 2, KEY_STACK - 1, 2)]:
            swap = vs[a + 1] > vs[a]
            vs[a], vs[a + 1] = jnp.where(swap, vs[a + 1], vs[a]), jnp.where(swap, vs[a], vs[a + 1])
            ids[a], ids[a + 1] = jnp.where(swap, ids[a + 1], ids[a]), jnp.where(swap, ids[a], ids[a + 1])
        groups.append((vs, ids))

    tv, ti = _Ranked(r8), _Ranked(r8)
    for k in range(PEER_TOPK):
        m = jnp.max(functools.reduce(jnp.maximum, [vs[0] for vs, _ in groups]),
                    axis=0, keepdims=True)
        low = functools.reduce(jnp.minimum, [jnp.where(vs[0] == m, ids[0], float(PEER_KEYS))
                                            for vs, ids in groups])
        idx = jnp.min(low, axis=0, keepdims=True)
        tv.push(m)
        ti.push(idx)
        live = min(KEY_STACK, PEER_TOPK - 1 - k)
        for vs, ids in groups:
            if live == 0:
                break
            hit = ids[0] == idx
            for d in range(live):
                below = vs[d + 1] if d + 1 < KEY_STACK else -jnp.inf
                vs[d] = jnp.where(hit, below, vs[d])
                if d + 1 < KEY_STACK:
                    ids[d] = jnp.where(hit, ids[d + 1], ids[d])
    return tv, ti


_POS_SHIFT = float(PEER_EXPERTS)


def _pair_top16(v1, i1, v2, i2, r8):
    ninf = -jnp.inf
    big = 3.0e38
    lo_v, lo_k = [], []
    for j in range(PEER_TOPK):
        lists = PEER_TOPK // (j + 1)
        val = v1.lo + v2.rows[j]
        lo_v.append(val if lists >= 8 else jnp.where(r8 < lists, val, ninf))
        lo_k.append((r8 * 16.0 + j) * _POS_SHIFT + i1.lo * PEER_KEYS + i2.rows[j])
    hi_v = v1.hi + v2.rows[0]
    hi_k = (r8 * 16.0 + 128.0) * _POS_SHIFT + i1.hi * PEER_KEYS + i2.rows[0]

    top, sel = _Ranked(r8), _Ranked(r8)
    for k in range(PEER_TOPK):
        m = jnp.max(jnp.maximum(lo_v[0], hi_v), axis=0, keepdims=True)
        kk = jnp.min(jnp.minimum(jnp.where(lo_v[0] == m, lo_k[0], big),
                                 jnp.where(hi_v == m, hi_k, big)), axis=0, keepdims=True)
        top.push(m)
        sel.push(kk)
        live = PEER_TOPK - 1 - k
        if live:
            hit = lo_k[0] == kk
            for d in range(live):
                lo_v[d] = jnp.where(hit, lo_v[d + 1], lo_v[d])
                lo_k[d] = jnp.where(hit, lo_k[d + 1], lo_k[d])
            hi_v = jnp.where(hi_k == kk, ninf, hi_v)
    return top, sel


GATE_PITCH = PEER_KEYS + 8
GATE_BATCH = 8
ROUTE_UNROLL = 2


def _build_gates(slot_ref, first, count, stage_ref, g_ref):
    key_iota = lax.broadcasted_iota(jnp.int32, (1, PEER_KEYS, PEER_SLOTS), 1).astype(F32)
    for t0 in range(0, count, GATE_BATCH):
        rows = pl.ds(first + t0, GATE_BATCH)
        a = slot_ref[0, rows, :][:, None, :]
        b = slot_ref[1, rows, :][:, None, :]
        g = slot_ref[2, rows, :][:, None, :]
        one_a = jnp.where(key_iota == a, 1.0, 0.0).astype(BF16)
        g_at_b = jnp.where(key_iota == b, g, 0.0).astype(BF16)
        gates = lax.dot_general(one_a, g_at_b, (((2,), (2,)), ((0,), (0,))),
                                preferred_element_type=F32)
        for t in range(GATE_BATCH):
            r0 = (t0 + t) * GATE_PITCH
            stage_ref[r0:r0 + PEER_KEYS, :] = gates[t]
    for a in range(PEER_KEYS):
        g_ref[a, pl.ds(first, count), :] = stage_ref[
            pl.ds(a, count, stride=GATE_PITCH), :].astype(BF16)


def _route_kernel(om_ref, od_ref, x_ref, gom_ref, god_ref, wom_ref, wod_ref, gffn_ref,
                  wq_ref, kdt_ref,
                  h_ref, xn_ref, g_ref,
                  sc_ref, ranked_ref, slots_ref, stage_ref):
    step = pl.program_id(0)
    tile = x_ref.shape[0]

    @pl.when(step == 0)
    def _():
        slots_ref[...] = jnp.zeros_like(slots_ref)

    prev_slots = slots_ref.at[(step + 1) % 2]
    nm = _rms(om_ref[...], gom_ref[...]).astype(BF16)
    nd = _rms(od_ref[...], god_ref[...]).astype(BF16)
    h = x_ref[...] + _dot(nm, wom_ref[...]) + _dot(nd, wod_ref[...])
    h_ref[...] = h
    xn = _rms(h, gffn_ref[...]).astype(BF16)
    xn_ref[...] = xn
    q = _dot(xn, wq_ref[...])
    kdt = kdt_ref[...]
    columns = tile // LANES
    units = PEER_HEADS * columns
    for hd in range(PEER_HEADS):
        qh = q[:, hd * LANES:(hd + 1) * LANES].astype(BF16)
        for c in range(columns):
            sc_ref[hd * columns + c] = _dot_nt(kdt, qh[c * LANES:(c + 1) * LANES])

    r8 = lax.broadcasted_iota(jnp.int32, (8, LANES), 0).astype(F32)
    tokens_per_unit = tile // units

    def rank_and_build(unit, carry):
        def key_blocks(first):
            return [sc_ref[unit, first + 8 * d:first + 8 * d + 8, :] for d in range(PEER_KEYS // 8)]

        v1, i1 = _top16(key_blocks(0), r8)
        v2, i2 = _top16(key_blocks(PEER_KEYS), r8)
        top, sel = _pair_top16(v1, i1, v2, i2, r8)
        e_lo = jnp.exp(top.lo - top.rows[0])
        e_hi = jnp.exp(top.hi - top.rows[0])
        inv = 1.0 / (jnp.sum(e_lo, axis=0, keepdims=True) + jnp.sum(e_hi, axis=0, keepdims=True))
        for half, (s, e) in enumerate(((sel.lo, e_lo), (sel.hi, e_hi))):
            rows = slice(half * 8, half * 8 + 8)
            expert = s - jnp.floor(s * (1.0 / _POS_SHIFT)) * _POS_SHIFT
            a = jnp.floor(expert * (1.0 / PEER_KEYS))
            ranked_ref[0, unit, rows, :] = a
            ranked_ref[1, unit, rows, :] = expert - a * PEER_KEYS
            ranked_ref[2, unit, rows, :] = e * inv
        first = pl.multiple_of(unit * tokens_per_unit, tokens_per_unit)
        _build_gates(prev_slots, first, tokens_per_unit, stage_ref, g_ref)
        return carry

    lax.fori_loop(0, units, rank_and_build, 0, unroll=ROUTE_UNROLL)
    slots = slots_ref.at[step % 2]
    for k in range(3):
        for c in range(columns):
            by_slot = jnp.concatenate(
                [ranked_ref[k, hd * columns + c] for hd in range(PEER_HEADS)], axis=0)
            slots[k, c * LANES:(c + 1) * LANES, :] = by_slot.T


def _route(o_mla, o_dil, x2, g_om, g_od, w_om, w_od, g_ffn, w_q, kdt, tile):
    T = x2.shape[0]
    steps = T // tile
    full = lambda a: pl.BlockSpec(a.shape, lambda i: (0,) * a.ndim)
    row = lambda w: pl.BlockSpec((tile, w), lambda i: (jnp.minimum(i, steps - 1), 0))
    units = PEER_HEADS * (tile // LANES)
    return pl.pallas_call(
        _route_kernel,
        grid=(steps + 1,),
        in_specs=[row(o_mla.shape[1]), row(o_dil.shape[1]), row(D_MODEL), full(g_om), full(g_od),
                  full(w_om), full(w_od), full(g_ffn), full(w_q), full(kdt)],
        out_specs=[row(D_MODEL), row(D_MODEL),
                   pl.BlockSpec((PEER_KEYS, tile, PEER_KEYS),
                                lambda i: (0, jnp.maximum(i - 1, 0), 0))],
        out_shape=[jax.ShapeDtypeStruct((T, D_MODEL), F32),
                   jax.ShapeDtypeStruct((T, D_MODEL), BF16),
                   jax.ShapeDtypeStruct((PEER_KEYS, T, PEER_KEYS), BF16)],
        scratch_shapes=[pltpu.VMEM((units, 2 * PEER_KEYS, LANES), F32),
                        pltpu.VMEM((3, units, PEER_TOPK, LANES), F32),
                        pltpu.VMEM((2, 3, tile, PEER_SLOTS), F32),
                        pltpu.VMEM((tile // units * GATE_PITCH, PEER_KEYS), F32)],
        compiler_params=_params(("arbitrary",)),
        name="route",
    )(o_mla, o_dil, x2, g_om, g_od, w_om, w_od, g_ffn, w_q, kdt)


def _peer_kernel(xn_ref, u_ref, v_ref, g_ref, o_ref, *, a_per_step):
    @pl.when(pl.program_id(1) == 0)
    def _():
        o_ref[...] = jnp.zeros_like(o_ref)

    pre = _dot_nt(xn_ref[...], u_ref[...].astype(BF16))
    act = 0.5 * pre * (1.0 + lax.erf(pre * float(1.0 / np.sqrt(2.0))))
    gate = jnp.concatenate([g_ref[a] for a in range(a_per_step)], axis=1)
    o_ref[...] += _dot((act * gate.astype(F32)).astype(BF16), v_ref[...].astype(BF16))


def _peer(xn, u, v, gates, tile, a_per_step):
    T = xn.shape[0]
    et = a_per_step * PEER_KEYS
    return pl.pallas_call(
        functools.partial(_peer_kernel, a_per_step=a_per_step),
        grid=(T // tile, PEER_EXPERTS // et),
        in_specs=[pl.BlockSpec((tile, D_MODEL), lambda i, j: (i, 0)),
                  pl.BlockSpec((et, D_MODEL), lambda i, j: (j, 0)),
                  pl.BlockSpec((et, D_MODEL), lambda i, j: (j, 0)),
                  pl.BlockSpec((a_per_step, tile, PEER_KEYS), lambda i, j: (j, i, 0))],
        out_specs=pl.BlockSpec((tile, D_MODEL), lambda i, j: (i, 0)),
        out_shape=jax.ShapeDtypeStruct((T, D_MODEL), F32),
        compiler_params=_params(("parallel", "arbitrary")),
        name="peer",
    )(xn, u, v, gates)


def _ple_final_kernel(h_ref, y_ref, p_ref, gple_ref, wg_ref, wp_ref, gfin_ref, o_ref, *, last_layer):
    h = h_ref[...] + y_ref[...]
    gate = jax.nn.sigmoid(_dot(_rms(h, gple_ref[...]).astype(BF16), wg_ref[...]))
    h = h + gate * _dot(p_ref[...].astype(BF16), wp_ref[...])
    o_ref[...] = _rms(h, gfin_ref[...]) if last_layer else h


def _ple_final(h1, y, p2, g_ple, w_gate, w_proj, g_final, tile, last_layer):
    T = h1.shape[0]
    full = lambda a: pl.BlockSpec(a.shape, lambda i: (0,) * a.ndim)
    row = lambda w: pl.BlockSpec((tile, w), lambda i: (i, 0))
    return pl.pallas_call(
        functools.partial(_ple_final_kernel, last_layer=last_layer),
        grid=(T // tile,),
        in_specs=[row(D_MODEL), row(D_MODEL), row(PLE_DIM), full(g_ple), full(w_gate),
                  full(w_proj), full(g_final)],
        out_specs=row(D_MODEL),
        out_shape=jax.ShapeDtypeStruct((T, D_MODEL), F32),
        compiler_params=_params(("parallel",)),
        name="ple_final",
    )(h1, y, p2, g_ple, w_gate, w_proj, g_final)


def _rope_lane_tables(seq, rot_dim, first_lane, period, scale):
    half = rot_dim // 2
    inv = ROPE_THETA ** (-jnp.arange(0, rot_dim, 2, dtype=F32) / rot_dim)
    ang = jnp.arange(seq, dtype=F32)[:, None] * inv[None, :]
    cos, sin = jnp.cos(ang), jnp.sin(ang)
    c = jnp.ones((seq, period), F32)
    c = c.at[:, first_lane:first_lane + half].set(cos).at[:, first_lane + half:first_lane + rot_dim].set(cos)
    s_lo = jnp.zeros((seq, period), F32).at[:, first_lane + half:first_lane + rot_dim].set(sin)
    s_hi = jnp.zeros((seq, period), F32).at[:, first_lane:first_lane + half].set(-sin)
    tabs = jnp.stack([c, s_lo, s_hi]) * scale
    return jnp.tile(tabs, (1, 1, LANES // period))


def _pick_tile(n, want):
    t = min(n, want)
    assert n % t == 0, (n, t)
    return t


def kernel(x, p, g_mix, w_in, g_cq, w_uq, g_ckv, w_ukv, g_out_mla, g_out_dil, w_out, g_ffn,
           w_peer_q, peer_keys1, peer_keys2, peer_u, peer_v, g_ple, w_ple_gate, w_ple_proj, g_final):
    batch, seq, _ = x.shape
    depth = p.shape[0]
    T = batch * seq
    mla_w = MLA_HEADS * MLA_V

    tabs = (_rope_lane_tables(seq, MLA_ROPE, MLA_NOPE, LANES, MLA_QK ** -0.5),
            _rope_lane_tables(seq, MLA_ROPE, MLA_NOPE, LANES, 1.0),
            _rope_lane_tables(seq, DIL_ROT, 0, DIL_HEAD_DIM, DIL_HEAD_DIM ** -0.5),
            _rope_lane_tables(seq, DIL_ROT, 0, DIL_HEAD_DIM, 1.0))

    h = x.reshape(T, D_MODEL)
    for i in range(depth):
        wi = w_in[i]
        o1 = MLA_Q_RANK + MLA_KV_RANK
        k_r_cols = jnp.pad(wi[:, o1:o1 + MLA_ROPE], ((0, 0), (MLA_NOPE, LANES - MLA_QK)))
        w_in_p = jnp.concatenate([wi[:, :o1], k_r_cols, wi[:, o1 + MLA_ROPE:]], axis=1).astype(BF16)
        w_uq_p = jnp.pad(w_uq[i].reshape(MLA_Q_RANK, MLA_HEADS, MLA_QK),
                         ((0, 0), (0, 0), (0, LANES - MLA_QK))).reshape(MLA_Q_RANK, -1).astype(BF16)
        w_ukv3 = w_ukv[i].reshape(MLA_KV_RANK, MLA_HEADS, MLA_NOPE + MLA_V)
        w_uk_p = jnp.pad(w_ukv3[:, :, :MLA_NOPE],
                         ((0, 0), (0, 0), (0, LANES - MLA_NOPE))).reshape(MLA_KV_RANK, -1).astype(BF16)
        w_uv = w_ukv3[:, :, MLA_NOPE:].reshape(MLA_KV_RANK, -1).astype(BF16)
        zeros = jnp.zeros((PEER_KEYS, PEER_HALF), F32)
        kdt = jnp.concatenate([jnp.concatenate([peer_keys1[i], zeros], axis=1),
                               jnp.concatenate([zeros, peer_keys2[i]], axis=1)], axis=0).astype(BF16)

        qm, km, vm, qd, kd, vd = _in_proj(
            h, g_mix[i][None], w_in_p, g_cq[i][None], w_uq_p, g_ckv[i][None], w_uk_p, w_uv,
            tabs, seq, _pick_tile(seq, PROJ_TILE))
        o_mla = _mla_attn(qm, km, vm, batch, seq, _pick_tile(seq, MLA_Q_TILE)).reshape(T, mla_w)
        o_dil = _dil_attn(qd, kd, vd, batch, seq).reshape(T, DIL_WIDTH)

        h1, xn, gates = _route(
            o_mla, o_dil, h, g_out_mla[i][None], g_out_dil[i][None],
            w_out[i][:mla_w].astype(BF16), w_out[i][mla_w:].astype(BF16), g_ffn[i][None],
            w_peer_q[i].astype(BF16), kdt, _pick_tile(T, ROUTE_TILE))
        y = _peer(xn, peer_u[i], peer_v[i], gates, _pick_tile(T, PEER_TILE), PEER_A_PER_STEP)
        h = _ple_final(h1, y, p[i].reshape(T, PLE_DIM), g_ple[i][None], w_ple_gate[i].astype(BF16),
                       w_ple_proj[i].astype(BF16), g_final[None], _pick_tile(T, PROJ_TILE),
                       last_layer=(i == depth - 1))
    return h.reshape(batch, seq, D_MODEL)
```

```python
import functools

import numpy as np
import jax
import jax.numpy as jnp
from jax import lax
from jax.experimental import pallas as pl
from jax.experimental.pallas import tpu as pltpu

F32 = jnp.float32
BF16 = jnp.bfloat16

EPS = 1e-6
MASKED_SCORE = -1e30
ROPE_THETA = 500000.0

LANES = 128
D_MODEL = 1024
PLE_DIM = 256
MLA_HEADS = 8
MLA_NOPE = 64
MLA_ROPE = 32
MLA_V = 64
MLA_QK = MLA_NOPE + MLA_ROPE
MLA_Q_RANK = 384
MLA_KV_RANK = 256
DIL_HEADS = 8
DIL_HEAD_DIM = 64
DIL_ROT = 16
DIL_WIDTH = DIL_HEADS * DIL_HEAD_DIM
DIL_PATTERNS = ((128, 1), (512, 4), (2048, 16))
PEER_KEYS = 128
PEER_HEADS = 8
PEER_HALF = 64
PEER_TOPK = 16
PEER_SLOTS = PEER_HEADS * PEER_TOPK
PEER_EXPERTS = PEER_KEYS * PEER_KEYS

VMEM_LIMIT = 48 * 1024 * 1024

PROJ_TILE = 512
MLA_Q_TILE = 2048
MLA_PV_CHUNKS = 2
ROUTE_TILE = 256
PEER_TILE = 1024
PEER_A_PER_STEP = 8
PEER_VMEM_LIMIT = 58 * 1024 * 1024


def _params(semantics):
    return pltpu.CompilerParams(dimension_semantics=semantics, vmem_limit_bytes=VMEM_LIMIT)


def _rms(x, g):
    return x * lax.rsqrt(jnp.mean(x * x, axis=-1, keepdims=True) + EPS) * g


def _rope(x, tab_ref, half):
    return (x * tab_ref[0]
            + pltpu.roll(x, half, 1) * tab_ref[1]
            + pltpu.roll(x, LANES - half, 1) * tab_ref[2])


def _dot(a, b):
    return jnp.dot(a, b, preferred_element_type=F32)


def _dot_nt(a, b):
    return lax.dot_general(a, b, (((1,), (1,)), ((), ())), preferred_element_type=F32)


def _in_proj_kernel(x_ref, gmix_ref, win_ref, gcq_ref, wuq_ref, gckv_ref, wuk_ref, wuv_ref,
                    tmq_ref, tmk_ref, tdq_ref, tdk_ref,
                    qm_ref, km_ref, vm_ref, qd_ref, kd_ref, vd_ref):
    hn = _rms(x_ref[...], gmix_ref[...]).astype(BF16)
    y = _dot(hn, win_ref[...])
    o = 0
    c_q = y[:, o:o + MLA_Q_RANK]; o += MLA_Q_RANK
    c_kv = y[:, o:o + MLA_KV_RANK]; o += MLA_KV_RANK
    k_r = y[:, o:o + LANES]; o += LANES
    q_d = y[:, o:o + DIL_WIDTH]; o += DIL_WIDTH
    k_d = y[:, o:o + DIL_WIDTH]; o += DIL_WIDTH
    v_d = y[:, o:o + DIL_WIDTH]

    q = _dot(_rms(c_q, gcq_ref[...]).astype(BF16), wuq_ref[...])
    ckvn = _rms(c_kv, gckv_ref[...]).astype(BF16)
    k_n = _dot(ckvn, wuk_ref[...])
    vm_ref[...] = _dot(ckvn, wuv_ref[...]).astype(BF16)
    k_rope = _rope(k_r, tmk_ref, MLA_ROPE // 2)
    for h in range(MLA_HEADS):
        sl = slice(h * LANES, (h + 1) * LANES)
        qm_ref[:, sl] = _rope(q[:, sl], tmq_ref, MLA_ROPE // 2).astype(BF16)
        km_ref[:, sl] = (k_n[:, sl] + k_rope).astype(BF16)
    for c in range(DIL_WIDTH // LANES):
        sl = slice(c * LANES, (c + 1) * LANES)
        qd_ref[:, sl] = _rope(q_d[:, sl], tdq_ref, DIL_ROT // 2)
        kd_ref[:, sl] = _rope(k_d[:, sl], tdk_ref, DIL_ROT // 2)
    vd_ref[...] = v_d


def _in_proj(x2, g_mix, w_in_p, g_cq, w_uq_p, g_ckv, w_uk_p, w_uv, tabs, seq, tile):
    T = x2.shape[0]
    n_pos = seq // tile
    full = lambda a: pl.BlockSpec(a.shape, lambda i: (0,) * a.ndim)
    tab = pl.BlockSpec((3, tile, LANES), lambda i: (0, i % n_pos, 0))
    row = lambda w: pl.BlockSpec((tile, w), lambda i: (i, 0))
    outs = [(T, MLA_HEADS * LANES), (T, MLA_HEADS * LANES), (T, MLA_HEADS * MLA_V),
            (T, DIL_WIDTH), (T, DIL_WIDTH), (T, DIL_WIDTH)]
    return pl.pallas_call(
        _in_proj_kernel,
        grid=(T // tile,),
        in_specs=[row(D_MODEL), full(g_mix), full(w_in_p), full(g_cq), full(w_uq_p),
                  full(g_ckv), full(w_uk_p), full(w_uv), tab, tab, tab, tab],
        out_specs=[row(s[1]) for s in outs],
        out_shape=[jax.ShapeDtypeStruct(s, BF16) for s in outs[:3]]
                  + [jax.ShapeDtypeStruct(s, F32) for s in outs[3:]],
        compiler_params=_params(("parallel",)),
        name="in_proj",
    )(x2, g_mix, w_in_p, g_cq, w_uq_p, g_ckv, w_uk_p, w_uv, *tabs)


def _mla_attn_kernel(q_ref, k_ref, v_ref, o_ref):
    v = v_ref[...]
    lane = lax.broadcasted_iota(jnp.int32, (q_ref.shape[0], LANES), 1)
    out = None
    for h in range(2):
        sl = slice(h * LANES, (h + 1) * LANES)
        s = _dot_nt(q_ref[:, sl], k_ref[:, sl])
        e = jnp.exp(s - jnp.max(s, axis=-1, keepdims=True))
        p = e.astype(BF16)
        rows = p.shape[0] // MLA_PV_CHUNKS
        o = jnp.concatenate([_dot(p[c * rows:(c + 1) * rows], v) for c in range(MLA_PV_CHUNKS)],
                            axis=0) / jnp.sum(e, axis=-1, keepdims=True)
        out = o if out is None else jnp.where(lane < MLA_V, out, o)
    o_ref[...] = out


def _mla_attn(qm, km, vm, batch, seq, tq):
    q3 = qm.reshape(batch, seq, MLA_HEADS * LANES)
    k3 = km.reshape(batch, seq, MLA_HEADS * LANES)
    v3 = vm.reshape(batch, seq, MLA_HEADS * MLA_V)
    return pl.pallas_call(
        _mla_attn_kernel,
        grid=(batch, MLA_HEADS // 2, seq // tq),
        in_specs=[pl.BlockSpec((None, tq, 2 * LANES), lambda b, p, i: (b, i, p)),
                  pl.BlockSpec((None, seq, 2 * LANES), lambda b, p, i: (b, 0, p)),
                  pl.BlockSpec((None, seq, LANES), lambda b, p, i: (b, 0, p))],
        out_specs=pl.BlockSpec((None, tq, LANES), lambda b, p, i: (b, i, p)),
        out_shape=jax.ShapeDtypeStruct((batch, seq, MLA_HEADS * MLA_V), F32),
        compiler_params=_params(("parallel", "parallel", "parallel")),
        name="mla_attn",
    )(q3, k3, v3)


DIL_Q_BLOCK = 128
DIL_K_WINDOW = 256


def _dil_attn_kernel(q_ref, k_ref, v_ref, o_ref, acc_ref, m_ref, l_ref):
    seq = q_ref.shape[0]
    lane = lax.broadcasted_iota(jnp.int32, (1, 1, LANES), 2)
    head0 = lane < DIL_HEAD_DIM

    for p, (window, dil) in enumerate(DIL_PATTERNS):
        sub_len = seq // dil
        radius = window // (2 * dil)
        qb = min(DIL_Q_BLOCK, sub_len)
        kw = min(DIL_K_WINDOW, sub_len)
        assert kw == sub_len or kw >= qb + 2 * radius
        blocks = sub_len // qb
        rel = (lax.broadcasted_iota(jnp.int32, (qb, kw), 0)
               - lax.broadcasted_iota(jnp.int32, (qb, kw), 1))

        def rows(first, count, dil=dil):
            return pl.ds(first, count) if dil == 1 else pl.ds(first, count, stride=dil)

        q_rows, k_rows, shifts = [], [], []
        for res in range(dil):
            for jb in range(blocks):
                j0 = jb * qb
                k0 = min(max(j0 - (kw - qb) // 2, 0), sub_len - kw)
                q_rows.append(rows(res + dil * j0, qb))
                k_rows.append(rows(res + dil * k0, kw))
                shifts.append(j0 - k0)
        near_by_shift = {s: jnp.abs(rel + s) <= radius for s in sorted(set(shifts))}
        q = jnp.stack([q_ref[r, :] for r in q_rows])
        k = jnp.stack([k_ref[r, :] for r in k_rows]).astype(BF16)
        v = jnp.stack([v_ref[r, :] for r in k_rows]).astype(BF16)

        acc, m_all, l_all = None, None, None
        for h in range(2):
            qh = jnp.where(head0 if h == 0 else ~head0, q, 0.0).astype(BF16)
            s = lax.dot_general(qh, k, (((2,), (2,)), ((0,), (0,))), preferred_element_type=F32)
            s = jnp.stack([jnp.where(near_by_shift[shift], s[n], MASKED_SCORE)
                           for n, shift in enumerate(shifts)])
            m = jnp.max(s, axis=-1, keepdims=True)
            e = jnp.exp(s - m)
            l = jnp.sum(e, axis=-1, keepdims=True)
            a = lax.dot_general(e.astype(BF16), v, (((2,), (1,)), ((0,), (0,))),
                                preferred_element_type=F32)
            if h == 0:
                acc, m_all, l_all = a, m, l
            else:
                acc = jnp.where(head0, acc, a)
                m_all = jnp.where(head0, m_all, m)
                l_all = jnp.where(head0, l_all, l)
        for n, r in enumerate(q_rows):
            acc_ref[p, r, :] = acc[n]
            m_ref[p, r, :] = m_all[n]
            l_ref[p, r, :] = l_all[n]

    m = m_ref[...]
    w = jnp.exp(m - jnp.max(m, axis=0, keepdims=True))
    o_ref[...] = jnp.sum(w * acc_ref[...], axis=0) / jnp.sum(w * l_ref[...], axis=0)


def _dil_attn(qd, kd, vd, batch, seq):
    spec = pl.BlockSpec((None, seq, LANES), lambda b, p: (b, 0, p))
    stats = pltpu.VMEM((len(DIL_PATTERNS), seq, LANES), F32)
    return pl.pallas_call(
        _dil_attn_kernel,
        grid=(batch, DIL_HEADS // 2),
        in_specs=[spec, spec, spec],
        out_specs=spec,
        out_shape=jax.ShapeDtypeStruct((batch, seq, DIL_WIDTH), F32),
        scratch_shapes=[stats, stats, stats],
        compiler_params=_params(("parallel", "parallel")),
        name="dil_attn",
    )(qd.reshape(batch, seq, DIL_WIDTH), kd.reshape(batch, seq, DIL_WIDTH),
      vd.reshape(batch, seq, DIL_WIDTH))


class _Ranked:
    def __init__(self, r8):
        self.r8 = r8
        self.rows = []
        self.lo = jnp.zeros(r8.shape, F32)
        self.hi = jnp.zeros(r8.shape, F32)

    def push(self, row):
        k = len(self.rows)
        self.rows.append(row)
        if k < 8:
            self.lo = jnp.where(self.r8 == k, row, self.lo)
        else:
            self.hi = jnp.where(self.r8 == k - 8, row, self.hi)


KEY_STACK = 8


def _top16(blocks, r8):
    groups = []
    for g in range(len(blocks) // KEY_STACK):
        vs = [blocks[g * KEY_STACK + d] for d in range(KEY_STACK)]
        ids = [r8 + float(8 * (g * KEY_STACK + d)) for d in range(KEY_STACK)]
        for a in [a for r in range(KEY_STACK) for a in range(r % 2, KEY_STACK - 1, 2)]:
            swap = vs[a + 1] > vs[a]
            vs[a], vs[a + 1] = jnp.where(swap, vs[a + 1], vs[a]), jnp.where(swap, vs[a], vs[a + 1])
            ids[a], ids[a + 1] = jnp.where(swap, ids[a + 1], ids[a]), jnp.where(swap, ids[a], ids[a + 1])
        groups.append((vs, ids))

    tv, ti = _Ranked(r8), _Ranked(r8)
    for k in range(PEER_TOPK):
        m = jnp.max(functools.reduce(jnp.maximum, [vs[0] for vs, _ in groups]),
                    axis=0, keepdims=True)
        low = functools.reduce(jnp.minimum, [jnp.where(vs[0] == m, ids[0], float(PEER_KEYS))
                                            for vs, ids in groups])
        idx = jnp.min(low, axis=0, keepdims=True)
        tv.push(m)
        ti.push(idx)
        live = min(KEY_STACK, PEER_TOPK - 1 - k)
        for vs, ids in groups:
            if live == 0:
                break
            hit = ids[0] == idx
            for d in range(live):
                below = vs[d + 1] if d + 1 < KEY_STACK else -jnp.inf
                vs[d] = jnp.where(hit, below, vs[d])
                if d + 1 < KEY_STACK:
                    ids[d] = jnp.where(hit, ids[d + 1], ids[d])
    return tv, ti


_POS_SHIFT = float(PEER_EXPERTS)


def _pair_top16(v1, i1, v2, i2, r8):
    ninf = -jnp.inf
    big = 3.0e38
    lo_v, lo_k = [], []
    for j in range(PEER_TOPK):
        lists = PEER_TOPK // (j + 1)
        val = v1.lo + v2.rows[j]
        lo_v.append(val if lists >= 8 else jnp.where(r8 < lists, val, ninf))
        lo_k.append((r8 * 16.0 + j) * _POS_SHIFT + i1.lo * PEER_KEYS + i2.rows[j])
    hi_v = v1.hi + v2.rows[0]
    hi_k = (r8 * 16.0 + 128.0) * _POS_SHIFT + i1.hi * PEER_KEYS + i2.rows[0]

    top, sel = _Ranked(r8), _Ranked(r8)
    for k in range(PEER_TOPK):
        m = jnp.max(jnp.maximum(lo_v[0], hi_v), axis=0, keepdims=True)
        kk = jnp.min(jnp.minimum(jnp.where(lo_v[0] == m, lo_k[0], big),
                                 jnp.where(hi_v == m, hi_k, big)), axis=0, keepdims=True)
        top.push(m)
        sel.push(kk)
        live = PEER_TOPK - 1 - k
        if live:
            hit = lo_k[0] == kk
            for d in range(live):
                lo_v[d] = jnp.where(hit, lo_v[d + 1], lo_v[d])
                lo_k[d] = jnp.where(hit, lo_k[d + 1], lo_k[d])
            hi_v = jnp.where(hi_k == kk, ninf, hi_v)
    return top, sel


GATE_PITCH = PEER_KEYS + 8
GATE_BATCH = 8
ROUTE_UNROLL = 2


def _build_gates(slot_ref, first, count, stage_ref, g_ref):
    key_iota = lax.broadcasted_iota(jnp.int32, (1, PEER_KEYS, PEER_SLOTS), 1).astype(F32)
    for t0 in range(0, count, GATE_BATCH):
        rows = pl.ds(first + t0, GATE_BATCH)
        a = slot_ref[0, rows, :][:, None, :]
        b = slot_ref[1, rows, :][:, None, :]
        g = slot_ref[2, rows, :][:, None, :]
        one_a = jnp.where(key_iota == a, 1.0, 0.0).astype(BF16)
        g_at_b = jnp.where(key_iota == b, g, 0.0).astype(BF16)
        gates = lax.dot_general(one_a, g_at_b, (((2,), (2,)), ((0,), (0,))),
                                preferred_element_type=F32)
        for t in range(GATE_BATCH):
            r0 = (t0 + t) * GATE_PITCH
            stage_ref[r0:r0 + PEER_KEYS, :] = gates[t]
    for a in range(PEER_KEYS):
        g_ref[a, pl.ds(first, count), :] = stage_ref[
            pl.ds(a, count, stride=GATE_PITCH), :].astype(BF16)


def _route_kernel(om_ref, od_ref, x_ref, gom_ref, god_ref, wom_ref, wod_ref, gffn_ref,
                  wq_ref, kdt_ref,
                  h_ref, xn_ref, g_ref,
                  sc_ref, ranked_ref, slots_ref, stage_ref):
    step = pl.program_id(0)
    tile = x_ref.shape[0]

    @pl.when(step == 0)
    def _():
        slots_ref[...] = jnp.zeros_like(slots_ref)

    prev_slots = slots_ref.at[(step + 1) % 2]
    nm = _rms(om_ref[...], gom_ref[...]).astype(BF16)
    nd = _rms(od_ref[...], god_ref[...]).astype(BF16)
    h = x_ref[...] + _dot(nm, wom_ref[...]) + _dot(nd, wod_ref[...])
    h_ref[...] = h
    xn = _rms(h, gffn_ref[...]).astype(BF16)
    xn_ref[...] = xn
    q = _dot(xn, wq_ref[...])
    kdt = kdt_ref[...]
    columns = tile // LANES
    units = PEER_HEADS * columns
    for hd in range(PEER_HEADS):
        qh = q[:, hd * LANES:(hd + 1) * LANES].astype(BF16)
        for c in range(columns):
            sc_ref[hd * columns + c] = _dot_nt(kdt, qh[c * LANES:(c + 1) * LANES])

    r8 = lax.broadcasted_iota(jnp.int32, (8, LANES), 0).astype(F32)
    tokens_per_unit = tile // units

    def rank_and_build(unit, carry):
        def key_blocks(first):
            return [sc_ref[unit, first + 8 * d:first + 8 * d + 8, :] for d in range(PEER_KEYS // 8)]

        v1, i1 = _top16(key_blocks(0), r8)
        v2, i2 = _top16(key_blocks(PEER_KEYS), r8)
        top, sel = _pair_top16(v1, i1, v2, i2, r8)
        e_lo = jnp.exp(top.lo - top.rows[0])
        e_hi = jnp.exp(top.hi - top.rows[0])
        inv = 1.0 / (jnp.sum(e_lo, axis=0, keepdims=True) + jnp.sum(e_hi, axis=0, keepdims=True))
        for half, (s, e) in enumerate(((sel.lo, e_lo), (sel.hi, e_hi))):
            rows = slice(half * 8, half * 8 + 8)
            expert = s - jnp.floor(s * (1.0 / _POS_SHIFT)) * _POS_SHIFT
            a = jnp.floor(expert * (1.0 / PEER_KEYS))
            ranked_ref[0, unit, rows, :] = a
            ranked_ref[1, unit, rows, :] = expert - a * PEER_KEYS
            ranked_ref[2, unit, rows, :] = e * inv
        first = pl.multiple_of(unit * tokens_per_unit, tokens_per_unit)
        _build_gates(prev_slots, first, tokens_per_unit, stage_ref, g_ref)
        return carry

    lax.fori_loop(0, units, rank_and_build, 0, unroll=ROUTE_UNROLL)
    slots = slots_ref.at[step % 2]
    for k in range(3):
        for c in range(columns):
            by_slot = jnp.concatenate(
                [ranked_ref[k, hd * columns + c] for hd in range(PEER_HEADS)], axis=0)
            slots[k, c * LANES:(c + 1) * LANES, :] = by_slot.T


def _route(o_mla, o_dil, x2, g_om, g_od, w_om, w_od, g_ffn, w_q, kdt, tile):
    T = x2.shape[0]
    steps = T // tile
    full = lambda a: pl.BlockSpec(a.shape, lambda i: (0,) * a.ndim)
    row = lambda w: pl.BlockSpec((tile, w), lambda i: (jnp.minimum(i, steps - 1), 0))
    units = PEER_HEADS * (tile // LANES)
    return pl.pallas_call(
        _route_kernel,
        grid=(steps + 1,),
        in_specs=[row(o_mla.shape[1]), row(o_dil.shape[1]), row(D_MODEL), full(g_om), full(g_od),
                  full(w_om), full(w_od), full(g_ffn), full(w_q), full(kdt)],
        out_specs=[row(D_MODEL), row(D_MODEL),
                   pl.BlockSpec((PEER_KEYS, tile, PEER_KEYS),
                                lambda i: (0, jnp.maximum(i - 1, 0), 0))],
        out_shape=[jax.ShapeDtypeStruct((T, D_MODEL), F32),
                   jax.ShapeDtypeStruct((T, D_MODEL), BF16),
                   jax.ShapeDtypeStruct((PEER_KEYS, T, PEER_KEYS), BF16)],
        scratch_shapes=[pltpu.VMEM((units, 2 * PEER_KEYS, LANES), F32),
                        pltpu.VMEM((3, units, PEER_TOPK, LANES), F32),
                        pltpu.VMEM((2, 3, tile, PEER_SLOTS), F32),
                        pltpu.VMEM((tile // units * GATE_PITCH, PEER_KEYS), F32)],
        compiler_params=_params(("arbitrary",)),
        name="route",
    )(o_mla, o_dil, x2, g_om, g_od, w_om, w_od, g_ffn, w_q, kdt)


def _peer_kernel(xn_ref, u_ref, v_ref, g_ref, h_ref, p_ref, gple_ref, wg_ref, wp_ref, gfin_ref,
                 o_ref, *, a_per_step, last_layer):
    @pl.when(pl.program_id(1) == 0)
    def _():
        o_ref[...] = jnp.zeros_like(o_ref)

    pre = _dot_nt(xn_ref[...], u_ref[...].astype(BF16))
    act = 0.5 * pre * (1.0 + lax.erf(pre * float(1.0 / np.sqrt(2.0))))
    gate = jnp.concatenate([g_ref[a] for a in range(a_per_step)], axis=1)
    o_ref[...] += _dot((act * gate.astype(F32)).astype(BF16), v_ref[...].astype(BF16))

    @pl.when(pl.program_id(1) == pl.num_programs(1) - 1)
    def _():
        h = h_ref[...] + o_ref[...]
        gate_ple = jax.nn.sigmoid(_dot(_rms(h, gple_ref[...]).astype(BF16), wg_ref[...]))
        h = h + gate_ple * _dot(p_ref[...].astype(BF16), wp_ref[...])
        o_ref[...] = _rms(h, gfin_ref[...]) if last_layer else h


def _peer(xn, u, v, gates, h1, p2, g_ple, w_gate, w_proj, g_final, tile, a_per_step, last_layer):
    T = xn.shape[0]
    et = a_per_step * PEER_KEYS
    full = lambda a: pl.BlockSpec(a.shape, lambda i, j: (0,) * a.ndim)
    return pl.pallas_call(
        functools.partial(_peer_kernel, a_per_step=a_per_step, last_layer=last_layer),
        grid=(T // tile, PEER_EXPERTS // et),
        in_specs=[pl.BlockSpec((tile, D_MODEL), lambda i, j: (i, 0)),
                  pl.BlockSpec((et, D_MODEL), lambda i, j: (j, 0)),
                  pl.BlockSpec((et, D_MODEL), lambda i, j: (j, 0)),
                  pl.BlockSpec((a_per_step, tile, PEER_KEYS), lambda i, j: (j, i, 0)),
                  pl.BlockSpec((tile, D_MODEL), lambda i, j: (i, 0)),
                  pl.BlockSpec((tile, PLE_DIM), lambda i, j: (i, 0)),
                  full(g_ple), full(w_gate), full(w_proj), full(g_final)],
        out_specs=pl.BlockSpec((tile, D_MODEL), lambda i, j: (i, 0)),
        out_shape=jax.ShapeDtypeStruct((T, D_MODEL), F32),
        compiler_params=pltpu.CompilerParams(dimension_semantics=("parallel", "arbitrary"),
                                             vmem_limit_bytes=PEER_VMEM_LIMIT),
        name="peer",
    )(xn, u, v, gates, h1, p2, g_ple, w_gate, w_proj, g_final)


def _ple_final_kernel(h_ref, y_ref, p_ref, gple_ref, wg_ref, wp_ref, gfin_ref, o_ref, *, last_layer):
    h = h_ref[...] + y_ref[...]
    gate = jax.nn.sigmoid(_dot(_rms(h, gple_ref[...]).astype(BF16), wg_ref[...]))
    h = h + gate * _dot(p_ref[...].astype(BF16), wp_ref[...])
    o_ref[...] = _rms(h, gfin_ref[...]) if last_layer else h


def _ple_final(h1, y, p2, g_ple, w_gate, w_proj, g_final, tile, last_layer):
    T = h1.shape[0]
    full = lambda a: pl.BlockSpec(a.shape, lambda i: (0,) * a.ndim)
    row = lambda w: pl.BlockSpec((tile, w), lambda i: (i, 0))
    return pl.pallas_call(
        functools.partial(_ple_final_kernel, last_layer=last_layer),
        grid=(T // tile,),
        in_specs=[row(D_MODEL), row(D_MODEL), row(PLE_DIM), full(g_ple), full(w_gate),
                  full(w_proj), full(g_final)],
        out_specs=row(D_MODEL),
        out_shape=jax.ShapeDtypeStruct((T, D_MODEL), F32),
        compiler_params=_params(("parallel",)),
        name="ple_final",
    )(h1, y, p2, g_ple, w_gate, w_proj, g_final)


def _rope_lane_tables(seq, rot_dim, first_lane, period, scale):
    half = rot_dim // 2
    inv = ROPE_THETA ** (-jnp.arange(0, rot_dim, 2, dtype=F32) / rot_dim)
    ang = jnp.arange(seq, dtype=F32)[:, None] * inv[None, :]
    cos, sin = jnp.cos(ang), jnp.sin(ang)
    c = jnp.ones((seq, period), F32)
    c = c.at[:, first_lane:first_lane + half].set(cos).at[:, first_lane + half:first_lane + rot_dim].set(cos)
    s_lo = jnp.zeros((seq, period), F32).at[:, first_lane + half:first_lane + rot_dim].set(sin)
    s_hi = jnp.zeros((seq, period), F32).at[:, first_lane:first_lane + half].set(-sin)
    tabs = jnp.stack([c, s_lo, s_hi]) * scale
    return jnp.tile(tabs, (1, 1, LANES // period))


def _pick_tile(n, want):
    t = min(n, want)
    assert n % t == 0, (n, t)
    return t


def kernel(x, p, g_mix, w_in, g_cq, w_uq, g_ckv, w_ukv, g_out_mla, g_out_dil, w_out, g_ffn,
           w_peer_q, peer_keys1, peer_keys2, peer_u, peer_v, g_ple, w_ple_gate, w_ple_proj, g_final):
    batch, seq, _ = x.shape
    depth = p.shape[0]
    T = batch * seq
    mla_w = MLA_HEADS * MLA_V

    tabs = (_rope_lane_tables(seq, MLA_ROPE, MLA_NOPE, LANES, MLA_QK ** -0.5),
            _rope_lane_tables(seq, MLA_ROPE, MLA_NOPE, LANES, 1.0),
            _rope_lane_tables(seq, DIL_ROT, 0, DIL_HEAD_DIM, DIL_HEAD_DIM ** -0.5),
            _rope_lane_tables(seq, DIL_ROT, 0, DIL_HEAD_DIM, 1.0))

    h = x.reshape(T, D_MODEL)
    for i in range(depth):
        wi = w_in[i]
        o1 = MLA_Q_RANK + MLA_KV_RANK
        k_r_cols = jnp.pad(wi[:, o1:o1 + MLA_ROPE], ((0, 0), (MLA_NOPE, LANES - MLA_QK)))
        w_in_p = jnp.concatenate([wi[:, :o1], k_r_cols, wi[:, o1 + MLA_ROPE:]], axis=1).astype(BF16)
        w_uq_p = jnp.pad(w_uq[i].reshape(MLA_Q_RANK, MLA_HEADS, MLA_QK),
                         ((0, 0), (0, 0), (0, LANES - MLA_QK))).reshape(MLA_Q_RANK, -1).astype(BF16)
        w_ukv3 = w_ukv[i].reshape(MLA_KV_RANK, MLA_HEADS, MLA_NOPE + MLA_V)
        w_uk_p = jnp.pad(w_ukv3[:, :, :MLA_NOPE],
                         ((0, 0), (0, 0), (0, LANES - MLA_NOPE))).reshape(MLA_KV_RANK, -1).astype(BF16)
        w_uv = w_ukv3[:, :, MLA_NOPE:].reshape(MLA_KV_RANK, -1).astype(BF16)
        zeros = jnp.zeros((PEER_KEYS, PEER_HALF), F32)
        kdt = jnp.concatenate([jnp.concatenate([peer_keys1[i], zeros], axis=1),
                               jnp.concatenate([zeros, peer_keys2[i]], axis=1)], axis=0).astype(BF16)

        qm, km, vm, qd, kd, vd = _in_proj(
            h, g_mix[i][None], w_in_p, g_cq[i][None], w_uq_p, g_ckv[i][None], w_uk_p, w_uv,
            tabs, seq, _pick_tile(seq, PROJ_TILE))
        o_mla = _mla_attn(qm, km, vm, batch, seq, _pick_tile(seq, MLA_Q_TILE)).reshape(T, mla_w)
        o_dil = _dil_attn(qd, kd, vd, batch, seq).reshape(T, DIL_WIDTH)

        h1, xn, gates = _route(
            o_mla, o_dil, h, g_out_mla[i][None], g_out_dil[i][None],
            w_out[i][:mla_w].astype(BF16), w_out[i][mla_w:].astype(BF16), g_ffn[i][None],
            w_peer_q[i].astype(BF16), kdt, _pick_tile(T, ROUTE_TILE))
        h = _peer(xn, peer_u[i], peer_v[i], gates, h1, p[i].reshape(T, PLE_DIM), g_ple[i][None],
                  w_ple_gate[i].astype(BF16), w_ple_proj[i].astype(BF16), g_final[None],
                  _pick_tile(T, PEER_TILE), PEER_A_PER_STEP, last_layer=(i == depth - 1))
    return h.reshape(batch, seq, D_MODEL)
```
